```python
import math
import jax, jax.numpy as jnp
from jax import lax
import numpy as np

D_MODEL = 1024
BATCH = 4
SEQ = 4096
DEPTH = 1

CHUNK = 64
N_PREV_CHUNKS = 8
HEAD_DIM = 64
N_HEADS_A = 8
REL_CLIP = 128
N_HEADS_B = 8
N_KV_B = 2
N_IDX_HEADS = 8
IDX_DIM = 64
TOPK_MAX = 256
Q_BLOCK = 128
ROPE_THETA = 500000.0
ROT_DIM = HEAD_DIM // 4
D_FF = 2816
EPS = 1e-6

WIDTH_A = N_HEADS_A * HEAD_DIM
WIDTH_B = N_HEADS_B * HEAD_DIM
WIDTH_KV_B = N_KV_B * HEAD_DIM
SPLIT_SIZES = (WIDTH_A, WIDTH_A, WIDTH_A,
               WIDTH_B, WIDTH_KV_B, WIDTH_KV_B,
               N_IDX_HEADS * IDX_DIM, IDX_DIM,
               N_IDX_HEADS,
               D_MODEL, D_MODEL)
D_IN = sum(SPLIT_SIZES)

kernel_name = 'hybrid_chunked_relbias_dsa_macaron'


def _rmsnorm(x, g):
    xf = x.astype(jnp.float32)
    y = xf * lax.rsqrt(jnp.mean(xf * xf, axis=-1, keepdims=True) + EPS)
    return (y * g.astype(jnp.float32)).astype(x.dtype)


def _swiglu(x, w_in, w_out):
    gate, up = jnp.split(x @ w_in, 2, axis=-1)
    return (jax.nn.silu(gate) * up) @ w_out


def _rope_tables(seq):
    inv_freq = jnp.power(jnp.float32(ROPE_THETA), -jnp.arange(0, ROT_DIM, 2, dtype=jnp.float32) / ROT_DIM)
    ang = jnp.arange(seq, dtype=jnp.float32)[:, None] * inv_freq[None, :]
    return jnp.cos(ang)[:, None, :], jnp.sin(ang)[:, None, :]


def _partial_rope(t, cos, sin):
    tr = t[..., :ROT_DIM].astype(jnp.float32)
    t1, t2 = tr[..., :ROT_DIM // 2], tr[..., ROT_DIM // 2:]
    rot = jnp.concatenate([t1 * cos - t2 * sin, t2 * cos + t1 * sin], axis=-1)
    return jnp.concatenate([rot.astype(t.dtype), t[..., ROT_DIM:]], axis=-1)


def _mixer_a(q, k, v, rel_bias):
    B, S, H, Dh = q.shape
    nc = S // CHUNK
    band_len = (N_PREV_CHUNKS + 1) * CHUNK
    qc = q.reshape(B, nc, CHUNK, H, Dh)

    def band(t):
        tp = jnp.pad(t, ((0, 0), (N_PREV_CHUNKS * CHUNK, 0), (0, 0), (0, 0)))
        tp = tp.reshape(B, nc + N_PREV_CHUNKS, CHUNK, H, Dh)
        return jnp.concatenate([tp[:, j:j + nc] for j in range(N_PREV_CHUNKS + 1)], axis=2)

    kb, vb = band(k), band(v)
    s = jnp.einsum('bnqhd,bnkhd->bnhqk', qc, kb).astype(jnp.float32) * (Dh ** -0.5)
    qi = jnp.arange(CHUNK)
    kj = jnp.arange(band_len)
    dist = N_PREV_CHUNKS * CHUNK + qi[:, None] - kj[None, :]
    bias = rel_bias[:, jnp.clip(dist, -REL_CLIP, REL_CLIP) + REL_CLIP].astype(jnp.float32)
    key_pos = (jnp.arange(nc)[:, None] - N_PREV_CHUNKS) * CHUNK + kj[None, :]
    valid = (key_pos >= 0)[None, :, None, None, :]
    s = jnp.where(valid, s + bias[None, None], -jnp.inf)
    p = jax.nn.softmax(s, axis=-1).astype(vb.dtype)
    o = jnp.einsum('bnhqk,bnkhd->bnqhd', p, vb)
    return o.reshape(B, S, H * Dh)


def _to_blocks(a):
    B, S = a.shape[:2]
    return jnp.moveaxis(a.reshape((B, S // Q_BLOCK, Q_BLOCK) + a.shape[2:]), 1, 0)


def _mixer_b(q, k, v, q_idx, k_idx, w_idx):
    B, S, H, Dh = q.shape
    G = k.shape[2]
    R = H // G
    topk = min(TOPK_MAX, S // 4)
    nb = S // Q_BLOCK
    key_pos = jnp.arange(S)
    w_scaled = w_idx * (N_IDX_HEADS ** -0.5)

    def one_block(args):
        qb, qib, wb, pos_q = args
        sc = jnp.einsum('bqhd,bsd->bqhs', qib, k_idx).astype(jnp.float32) * (IDX_DIM ** -0.5)
        score = jnp.einsum('bqh,bqhs->bqs', wb.astype(jnp.float32), jax.nn.relu(sc))
        limit = (pos_q // CHUNK + 1) * CHUNK
        adm = key_pos[None, :] < limit[:, None]
        score = jnp.where(adm[None], score, -jnp.inf)
        _, idx = lax.top_k(score, topk)
        sel_ok = idx < limit[None, :, None]
        kg = jax.vmap(lambda t, i: t[i])(k, idx)
        vg = jax.vmap(lambda t, i: t[i])(v, idx)
        qg = qb.reshape(B, Q_BLOCK, G, R, Dh)
        s = jnp.einsum('bqgrd,bqkgd->bqgrk', qg, kg).astype(jnp.float32) * (Dh ** -0.5)
        s = jnp.where(sel_ok[:, :, None, None, :], s, -jnp.inf)
        p = jax.nn.softmax(s, axis=-1).astype(vg.dtype)
        o = jnp.einsum('bqgrk,bqkgd->bqgrd', p, vg)
        return o.reshape(B, Q_BLOCK, H * Dh)

    out = lax.map(one_block, (_to_blocks(q), _to_blocks(q_idx), _to_blocks(w_scaled),
                              key_pos.reshape(nb, Q_BLOCK)))
    return jnp.moveaxis(out, 0, 1).reshape(B, S, H * Dh)


def setup_inputs(seed: int = 0) -> dict:
    key = jax.random.key(seed)
    ks = jax.random.split(key, 16)
    f32 = jnp.float32

    def w(k, shape, fan_in):
        return jax.random.normal(k, shape, f32) * (fan_in ** -0.5)

    def gain(k):
        return 1.0 + 0.02 * jax.random.normal(k, (DEPTH, D_MODEL), f32)

    return {
        'x': jax.random.normal(ks[0], (BATCH, SEQ, D_MODEL), f32),
        'n1_g': gain(ks[1]),
        'ffn1_w_in': w(ks[2], (DEPTH, D_MODEL, 2 * D_FF), D_MODEL),
        'ffn1_w_out': w(ks[3], (DEPTH, D_FF, D_MODEL), D_FF),
        'n2_g': gain(ks[4]),
        'w_in': w(ks[5], (DEPTH, D_MODEL, D_IN), D_MODEL),
        'rel_bias': 0.2 * jax.random.normal(ks[6], (DEPTH, N_HEADS_A, 2 * REL_CLIP + 1), f32),
        'w_branch_a': w(ks[7], (DEPTH, WIDTH_A, D_MODEL), WIDTH_A),
        'w_branch_b': w(ks[8], (DEPTH, WIDTH_B, D_MODEL), WIDTH_B),
        'w_out': w(ks[9], (DEPTH, D_MODEL, D_MODEL), D_MODEL),
        'n3_g': gain(ks[10]),
        'ffn2_w_in': w(ks[11], (DEPTH, D_MODEL, 2 * D_FF), D_MODEL),
        'ffn2_w_out': w(ks[12], (DEPTH, D_FF, D_MODEL), D_FF),
        'nf_g': 1.0 + 0.02 * jax.random.normal(ks[13], (D_MODEL,), f32),
    }


def reference(x, n1_g, ffn1_w_in, ffn1_w_out, n2_g, w_in, rel_bias, w_branch_a, w_branch_b,
              w_out, n3_g, ffn2_w_in, ffn2_w_out, nf_g):
    B, S, _ = x.shape
    cos, sin = _rope_tables(S)
    split_points = list(np.cumsum(SPLIT_SIZES)[:-1])
    for l in range(DEPTH):
        x = x + 0.5 * _swiglu(_rmsnorm(x, n1_g[l]), ffn1_w_in[l], ffn1_w_out[l])
        h = _rmsnorm(x, n2_g[l])
        qa, ka, va, qb, kb, vb, qi, ki, wi, ga, gb = jnp.split(h @ w_in[l], split_points, axis=-1)
        qa = qa.reshape(B, S, N_HEADS_A, HEAD_DIM)
        ka = ka.reshape(B, S, N_HEADS_A, HEAD_DIM)
        va = va.reshape(B, S, N_HEADS_A, HEAD_DIM)
        qb = _partial_rope(qb.reshape(B, S, N_HEADS_B, HEAD_DIM), cos, sin)
        kb = _partial_rope(kb.reshape(B, S, N_KV_B, HEAD_DIM), cos, sin)
        vb = vb.reshape(B, S, N_KV_B, HEAD_DIM)
        qi = _partial_rope(qi.reshape(B, S, N_IDX_HEADS, IDX_DIM), cos, sin)
        ki = _partial_rope(ki.reshape(B, S, 1, IDX_DIM), cos, sin)[:, :, 0]
        o_a = _mixer_a(qa, ka, va, rel_bias[l])
        o_b = _mixer_b(qb, kb, vb, qi, ki, wi)
        merged = jax.nn.sigmoid(ga) * (o_a @ w_branch_a[l]) + jax.nn.sigmoid(gb) * (o_b @ w_branch_b[l])
        x = x + merged @ w_out[l]
        x = x + 0.5 * _swiglu(_rmsnorm(x, n3_g[l]), ffn2_w_in[l], ffn2_w_out[l])
    return _rmsnorm(x, nf_g)
```

```python
import functools

import jax
import jax.numpy as jnp
import numpy as np
from jax import lax
from jax.experimental import pallas as pl
from jax.experimental.pallas import tpu as pltpu

F32 = jnp.float32
BF16 = jnp.bfloat16

D_MODEL = 1024
D_FF = 2816
HEAD_DIM = 64
CHUNK = 64
N_PREV_CHUNKS = 8
N_HEADS_A = 8
REL_CLIP = 128
N_HEADS_B = 8
N_KV_B = 2
N_IDX_HEADS = 8
IDX_DIM = 64
TOPK_MAX = 256
ROPE_THETA = 500000.0
ROT_DIM = HEAD_DIM // 4
ROT_HALF = ROT_DIM // 2
EPS = 1e-6
WIDTH_A = N_HEADS_A * HEAD_DIM
WIDTH_B = N_HEADS_B * HEAD_DIM
WIDTH_KV_B = N_KV_B * HEAD_DIM
WIDTH_IDX = N_IDX_HEADS * IDX_DIM

LANES = 128
QK_SCALE = HEAD_DIM ** -0.5
IDX_SCALE = IDX_DIM ** -0.5

VMEM_LIMIT = 56 * 1024 * 1024

FFN_TM = 512
FFN_TF = D_FF // 2
PROJ_TM = 512
QB = 256
KT = 256
BAND_BLOCKS = N_PREV_CHUNKS * CHUNK // QB + 1
WI_ROWS = 16
NEG_BIG = -1e30


def _dot(a, b):
    return jnp.dot(a, b, preferred_element_type=F32)


def _dot_nt(a, b):
    return lax.dot_general(a, b, (((1,), (1,)), ((), ())), preferred_element_type=F32)


def _dot_tn(a, b):
    return lax.dot_general(a, b, (((0,), (0,)), ((), ())), preferred_element_type=F32)


def _rmsnorm(x, g):
    ms = jnp.mean(x * x, axis=-1, keepdims=True)
    return x * lax.rsqrt(ms + EPS) * g


def _ffn_kernel(x_ref, g_ref, wg_ref, wu_ref, wo_ref, gf_ref, o_ref, h_scr, acc_scr, *, final_norm):
    j = pl.program_id(1)

    @pl.when(j == 0)
    def _():
        h_scr[...] = _rmsnorm(x_ref[...], g_ref[...]).astype(BF16)
        acc_scr[...] = jnp.zeros_like(acc_scr)

    h = h_scr[...]
    gate = _dot(h, wg_ref[...])
    up = _dot(h, wu_ref[...])
    a = (gate * jax.nn.sigmoid(gate) * up).astype(BF16)
    acc_scr[...] += _dot(a, wo_ref[...])

    @pl.when(j == pl.num_programs(1) - 1)
    def _():
        y = x_ref[...] + 0.5 * acc_scr[...]
        if final_norm:
            y = _rmsnorm(y, gf_ref[...])
        o_ref[...] = y


def _ffn(x2d, g, w_in_bf, w_out_bf, gf, final_norm):
    T = x2d.shape[0]
    nf = D_FF // FFN_TF
    return pl.pallas_call(
        functools.partial(_ffn_kernel, final_norm=final_norm),
        grid=(T // FFN_TM, nf),
        in_specs=[
            pl.BlockSpec((FFN_TM, D_MODEL), lambda i, j: (i, 0)),
            pl.BlockSpec((1, D_MODEL), lambda i, j: (0, 0)),
            pl.BlockSpec((D_MODEL, FFN_TF), lambda i, j: (0, j)),
            pl.BlockSpec((D_MODEL, FFN_TF), lambda i, j: (0, j + nf)),
            pl.BlockSpec((FFN_TF, D_MODEL), lambda i, j: (j, 0)),
            pl.BlockSpec((1, D_MODEL), lambda i, j: (0, 0)),
        ],
        out_specs=pl.BlockSpec((FFN_TM, D_MODEL), lambda i, j: (i, 0)),
        out_shape=jax.ShapeDtypeStruct((T, D_MODEL), F32),
        scratch_shapes=[pltpu.VMEM((FFN_TM, D_MODEL), BF16), pltpu.VMEM((FFN_TM, D_MODEL), F32)],
        compiler_params=pltpu.CompilerParams(
            dimension_semantics=("parallel", "arbitrary"), vmem_limit_bytes=VMEM_LIMIT),
        name="ffn_final" if final_norm else "ffn",
    )(x2d, g, w_in_bf, w_in_bf, w_out_bf, gf)


def _rope_rows(x, cos, sin, n_heads):
    pieces = []
    for h in range(n_heads):
        r0 = h * HEAD_DIM
        t1 = x[r0:r0 + ROT_HALF]
        t2 = x[r0 + ROT_HALF:r0 + ROT_DIM]
        pieces.append(t1 * cos - t2 * sin)
        pieces.append(t2 * cos + t1 * sin)
        pieces.append(x[r0 + ROT_DIM:r0 + HEAD_DIM])
    return jnp.concatenate(pieces, axis=0)


def _rope_lanes(x, c, sa, sb):
    return x * c + pltpu.roll(x, LANES - ROT_HALF, 1) * sa + pltpu.roll(x, ROT_HALF, 1) * sb


def _inproj_kernel(x_ref, g_ref, wt_ref, wk_ref, wg_ref, cos_ref, sin_ref, c_ref, sa_ref, sb_ref,
                   qat_ref, vat_ref, qbt_ref, vbt_ref, qit_ref, wit_ref,
                   ka_ref, kb_ref, ki_ref, ga_ref, gb_ref):
    h = _rmsnorm(x_ref[...], g_ref[...]).astype(BF16)
    cos = cos_ref[...]
    sin = sin_ref[...]

    r = 0
    t = _dot_nt(wt_ref[r:r + WIDTH_A, :], h)
    qat_ref[...] = (t * QK_SCALE).astype(BF16)
    r += WIDTH_A
    vat_ref[...] = _dot_nt(wt_ref[r:r + WIDTH_A, :], h).astype(BF16)
    r += WIDTH_A
    t = _dot_nt(wt_ref[r:r + WIDTH_B, :], h)
    qbt_ref[...] = (_rope_rows(t, cos, sin, N_HEADS_B) * QK_SCALE).astype(BF16)
    r += WIDTH_B
    vbt_ref[...] = _dot_nt(wt_ref[r:r + WIDTH_KV_B, :], h).astype(BF16)
    r += WIDTH_KV_B
    t = _dot_nt(wt_ref[r:r + WIDTH_IDX, :], h)
    qit_ref[...] = (_rope_rows(t, cos, sin, N_IDX_HEADS) * IDX_SCALE).astype(BF16)
    r += WIDTH_IDX
    wit_ref[...] = _dot_nt(wt_ref[r:r + WI_ROWS, :], h) * (N_IDX_HEADS ** -0.5)

    ka_ref[...] = _dot(h, wk_ref[:, 0:WIDTH_A]).astype(BF16)
    c, sa, sb = c_ref[...], sa_ref[...], sb_ref[...]
    t = _dot(h, wk_ref[:, WIDTH_A:WIDTH_A + LANES])
    kb_ref[...] = _rope_lanes(t, c, sa, sb).astype(BF16)
    t = _dot(h, wk_ref[:, WIDTH_A + LANES:WIDTH_A + 2 * LANES])
    ki_ref[...] = _rope_lanes(t, c, sa, sb).astype(BF16)

    ga_ref[...] = _dot(h, wg_ref[:, 0:D_MODEL])
    gb_ref[...] = _dot(h, wg_ref[:, D_MODEL:2 * D_MODEL])


def _inproj(x3d, g, wt, wk, wg, cos_t, sin_t, c_tab, sa_tab, sb_tab):
    B, S, _ = x3d.shape
    tm = PROJ_TM
    full = lambda shape: pl.BlockSpec(shape, lambda b, s: (0,) * len(shape))
    tmaj = lambda w: pl.BlockSpec((None, tm, w), lambda b, s: (b, s, 0))
    fmaj = lambda w: pl.BlockSpec((None, w, tm), lambda b, s: (b, 0, s))
    tshape = lambda w, dt: jax.ShapeDtypeStruct((B, S, w), dt)
    fshape = lambda w, dt: jax.ShapeDtypeStruct((B, w, S), dt)
    return pl.pallas_call(
        _inproj_kernel,
        grid=(B, S // tm),
        in_specs=[
            tmaj(D_MODEL), full((1, D_MODEL)), full(wt.shape), full(wk.shape), full(wg.shape),
            pl.BlockSpec((ROT_HALF, tm), lambda b, s: (0, s)),
            pl.BlockSpec((ROT_HALF, tm), lambda b, s: (0, s)),
            pl.BlockSpec((tm, LANES), lambda b, s: (s, 0)),
            pl.BlockSpec((tm, LANES), lambda b, s: (s, 0)),
            pl.BlockSpec((tm, LANES), lambda b, s: (s, 0)),
        ],
        out_specs=[fmaj(WIDTH_A), fmaj(WIDTH_A), fmaj(WIDTH_B), fmaj(WIDTH_KV_B), fmaj(WIDTH_IDX),
                   fmaj(WI_ROWS), tmaj(WIDTH_A), tmaj(LANES), tmaj(LANES), tmaj(D_MODEL), tmaj(D_MODEL)],
        out_shape=[fshape(WIDTH_A, BF16), fshape(WIDTH_A, BF16), fshape(WIDTH_B, BF16),
                   fshape(WIDTH_KV_B, BF16), fshape(WIDTH_IDX, BF16), fshape(WI_ROWS, F32),
                   tshape(WIDTH_A, BF16), tshape(LANES, BF16), tshape(LANES, BF16),
                   tshape(D_MODEL, F32), tshape(D_MODEL, F32)],
        compiler_params=pltpu.CompilerParams(
            dimension_semantics=("parallel", "parallel"), vmem_limit_bytes=VMEM_LIMIT),
        name="inproj",
    )(x3d, g, wt, wk, wg, cos_t, sin_t, c_tab, sa_tab, sb_tab)


def _padded_rhs(qt_h, slot):
    z = jnp.zeros_like(qt_h)
    return jnp.concatenate([qt_h, z] if slot == 0 else [z, qt_h], axis=0)


def _mixer_a_kernel(qt_ref, k0_ref, k1_ref, k2_ref, v0_ref, v1_ref, v2_ref, bias_ref, o_ref):
    jb = pl.program_id(1)
    k_refs = (k0_ref, k1_ref, k2_ref)
    v_refs = (v0_ref, v1_ref, v2_ref)
    for h in range(N_HEADS_A):
        r0 = h * HEAD_DIM
        rhs = _padded_rhs(qt_ref[r0:r0 + HEAD_DIM, :], h % 2)
        lane0 = (h // 2) * LANES
        s = []
        for m in range(BAND_BLOCKS):
            sm = _dot(k_refs[m][:, lane0:lane0 + LANES], rhs) + bias_ref[h, m * QB:(m + 1) * QB, :]
            if m < BAND_BLOCKS - 1:
                sm = jnp.where(jb + m >= BAND_BLOCKS - 1, sm, -jnp.inf)
            s.append(sm)
        mx = s[0].max(axis=0, keepdims=True)
        for m in range(1, BAND_BLOCKS):
            mx = jnp.maximum(mx, s[m].max(axis=0, keepdims=True))
        l = jnp.zeros_like(mx)
        acc = jnp.zeros((HEAD_DIM, QB), F32)
        for m in range(BAND_BLOCKS):
            p = jnp.exp(s[m] - mx)
            l = l + p.sum(axis=0, keepdims=True)
            acc = acc + _dot(v_refs[m][r0:r0 + HEAD_DIM, :], p.astype(BF16))
        o_ref[r0:r0 + HEAD_DIM, :] = (acc / l).astype(BF16)


def _mixer_a(qat, ka, vat, bias_t):
    B, _, S = qat.shape
    nb = BAND_BLOCKS - 1
    kspec = lambda m: pl.BlockSpec((None, QB, WIDTH_A), lambda b, j: (b, jnp.maximum(j + m - nb, 0), 0))
    vspec = lambda m: pl.BlockSpec((None, WIDTH_A, QB), lambda b, j: (b, 0, jnp.maximum(j + m - nb, 0)))
    return pl.pallas_call(
        _mixer_a_kernel,
        grid=(B, S // QB),
        in_specs=[pl.BlockSpec((None, WIDTH_A, QB), lambda b, j: (b, 0, j)),
                  kspec(0), kspec(1), kspec(2), vspec(0), vspec(1), vspec(2),
                  pl.BlockSpec(bias_t.shape, lambda b, j: (0, 0, 0))],
        out_specs=pl.BlockSpec((None, WIDTH_A, QB), lambda b, j: (b, 0, j)),
        out_shape=jax.ShapeDtypeStruct((B, WIDTH_A, S), BF16),
        compiler_params=pltpu.CompilerParams(
            dimension_semantics=("parallel", "parallel"), vmem_limit_bytes=VMEM_LIMIT),
        name="mixer_a",
    )(qat, ka, ka, ka, vat, vat, vat, bias_t)


def _ordered_bits_to_f32(u):
    bits = jnp.where(u < 0, u ^ jnp.int32(-2 ** 31), ~u)
    return lax.bitcast_convert_type(bits, F32)


def _mixer_b_kernel(qit_ref, wit_ref, qbt_ref, ki_ref, kb_ref, vbt_ref, o_ref,
                    score_scr, acc_scr, m_scr, l_scr, *, topk):
    jb = pl.program_id(1)
    n_tiles = jb + 1
    q_chunk = lax.broadcasted_iota(jnp.int32, (1, QB), 1) // CHUNK
    limit = jb * QB + (q_chunk + 1) * CHUNK
    row_iota = lax.broadcasted_iota(jnp.int32, (KT, QB), 0)

    rhs_i = [_padded_rhs(qit_ref[h * IDX_DIM:(h + 1) * IDX_DIM, :], 0) for h in range(N_IDX_HEADS)]
    w = wit_ref[...]

    def score_tile(t, carry):
        k0 = pl.multiple_of(t * KT, KT)
        ki_t = ki_ref[pl.ds(k0, KT), :]
        acc = jnp.zeros((KT, QB), F32)
        for h in range(N_IDX_HEADS):
            acc = acc + w[h:h + 1, :] * jnp.maximum(_dot(ki_t, rhs_i[h]), 0.0)
        adm = (row_iota + k0) < limit
        score_scr[pl.ds(k0, KT), :] = jnp.where(adm, acc, -jnp.inf)
        return carry

    lax.fori_loop(0, n_tiles, score_tile, 0)

    def count_ge(cand):
        def body(t, cnt):
            k0 = pl.multiple_of(t * KT, KT)
            hit = jnp.where(score_scr[pl.ds(k0, KT), :] >= cand, 1.0, 0.0)
            return cnt + hit.reshape(KT // 8, 8, QB).sum(axis=0)
        cnt = lax.fori_loop(0, n_tiles, body, jnp.zeros((8, QB), F32))
        return cnt.sum(axis=0, keepdims=True)

    neg_inf_code = jnp.int32(0x007FFFFF)

    def bit_step(i, t_u):
        cand_u = t_u | lax.shift_left(jnp.int32(1), 31 - i)
        cnt = count_ge(_ordered_bits_to_f32(cand_u))
        below = (cand_u >= 0) & (cand_u <= neg_inf_code)
        return jnp.where((cnt >= topk) | below, cand_u, t_u)

    t_u = lax.fori_loop(0, 32, bit_step, jnp.zeros((1, QB), jnp.int32))
    thr = _ordered_bits_to_f32(t_u)

    rhs_b = [_padded_rhs(qbt_ref[h * HEAD_DIM:(h + 1) * HEAD_DIM, :], h // (N_HEADS_B // N_KV_B))
             for h in range(N_HEADS_B)]
    tri = jnp.where(lax.broadcasted_iota(jnp.int32, (KT, KT), 1)
                    < lax.broadcasted_iota(jnp.int32, (KT, KT), 0), 1.0, 0.0).astype(BF16)
    m_scr[...] = jnp.full_like(m_scr, NEG_BIG)
    l_scr[...] = jnp.zeros_like(l_scr)
    acc_scr[...] = jnp.zeros_like(acc_scr)

    def attn_tile(t, need):
        k0 = pl.multiple_of(t * KT, KT)
        sc = score_scr[pl.ds(k0, KT), :]
        adm = (row_iota + k0) < limit
        gt = sc > thr
        eq = sc == thr
        eq_f = jnp.where(eq, 1.0, 0.0)
        before = _dot(tri, eq_f.astype(BF16))
        sel = (gt | (eq & (before < need))) & adm
        kb_t = kb_ref[pl.ds(k0, KT), :]
        for h in range(N_HEADS_B):
            g = h // (N_HEADS_B // N_KV_B)
            s = jnp.where(sel, _dot(kb_t, rhs_b[h]), -jnp.inf)
            m_old = m_scr[h:h + 1, :]
            m_new = jnp.maximum(m_old, s.max(axis=0, keepdims=True))
            alpha = jnp.exp(m_old - m_new)
            p = jnp.exp(s - m_new)
            l_scr[h:h + 1, :] = alpha * l_scr[h:h + 1, :] + p.sum(axis=0, keepdims=True)
            pv = _dot(vbt_ref[g * HEAD_DIM:(g + 1) * HEAD_DIM, pl.ds(k0, KT)], p.astype(BF16))
            r0 = h * HEAD_DIM
            acc_scr[r0:r0 + HEAD_DIM, :] = alpha * acc_scr[r0:r0 + HEAD_DIM, :] + pv
            m_scr[h:h + 1, :] = m_new
        return need - eq_f.sum(axis=0, keepdims=True)

    def count_gt():
        def body(t, cnt):
            k0 = pl.multiple_of(t * KT, KT)
            hit = jnp.where(score_scr[pl.ds(k0, KT), :] > thr, 1.0, 0.0)
            return cnt + hit.reshape(KT // 8, 8, QB).sum(axis=0)
        cnt = lax.fori_loop(0, n_tiles, body, jnp.zeros((8, QB), F32))
        return cnt.sum(axis=0, keepdims=True)

    lax.fori_loop(0, n_tiles, attn_tile, topk - count_gt())

    for h in range(N_HEADS_B):
        r0 = h * HEAD_DIM
        o_ref[r0:r0 + HEAD_DIM, :] = (acc_scr[r0:r0 + HEAD_DIM, :] / l_scr[h:h + 1, :]).astype(BF16)


def _mixer_b(qit, wit, qbt, ki, kb, vbt):
    B, _, S = qbt.shape
    topk = min(TOPK_MAX, S // 4)
    qspec = lambda w: pl.BlockSpec((None, w, QB), lambda b, j: (b, 0, j))
    return pl.pallas_call(
        functools.partial(_mixer_b_kernel, topk=float(topk)),
        grid=(B, S // QB),
        in_specs=[qspec(WIDTH_IDX), qspec(WI_ROWS), qspec(WIDTH_B),
                  pl.BlockSpec((None, S, LANES), lambda b, j: (b, 0, 0)),
                  pl.BlockSpec((None, S, LANES), lambda b, j: (b, 0, 0)),
                  pl.BlockSpec((None, WIDTH_KV_B, S), lambda b, j: (b, 0, 0))],
        out_specs=qspec(WIDTH_B),
        out_shape=jax.ShapeDtypeStruct((B, WIDTH_B, S), BF16),
        scratch_shapes=[pltpu.VMEM((S, QB), F32), pltpu.VMEM((WIDTH_B, QB), F32),
                        pltpu.VMEM((N_HEADS_B, QB), F32), pltpu.VMEM((N_HEADS_B, QB), F32)],
        compiler_params=pltpu.CompilerParams(
            dimension_semantics=("parallel", "arbitrary"), vmem_limit_bytes=VMEM_LIMIT),
        name="mixer_b",
    )(qit, wit, qbt, ki, kb, vbt)


def _merge_kernel(x_ref, oat_ref, obt_ref, ga_ref, gb_ref, wa_ref, wb_ref, wo_ref, o_ref):
    ya = _dot_tn(oat_ref[...], wa_ref[...])
    yb = _dot_tn(obt_ref[...], wb_ref[...])
    merged = jax.nn.sigmoid(ga_ref[...]) * ya + jax.nn.sigmoid(gb_ref[...]) * yb
    o_ref[...] = x_ref[...] + _dot(merged.astype(BF16), wo_ref[...])


def _merge(x3d, oat, obt, ga, gb, wa, wb, wo):
    B, S, _ = x3d.shape
    tm = PROJ_TM
    tmaj = pl.BlockSpec((None, tm, D_MODEL), lambda b, s: (b, s, 0))
    fmaj = lambda w: pl.BlockSpec((None, w, tm), lambda b, s: (b, 0, s))
    full = lambda a: pl.BlockSpec(a.shape, lambda b, s: (0, 0))
    return pl.pallas_call(
        _merge_kernel,
        grid=(B, S // tm),
        in_specs=[tmaj, fmaj(WIDTH_A), fmaj(WIDTH_B), tmaj, tmaj, full(wa), full(wb), full(wo)],
        out_specs=tmaj,
        out_shape=jax.ShapeDtypeStruct((B, S, D_MODEL), F32),
        compiler_params=pltpu.CompilerParams(
            dimension_semantics=("parallel", "parallel"), vmem_limit_bytes=VMEM_LIMIT),
        name="merge",
    )(x3d, oat, obt, ga, gb, wa, wb, wo)


def _rope_tables(seq):
    inv_freq = jnp.power(jnp.float32(ROPE_THETA), -jnp.arange(0, ROT_DIM, 2, dtype=F32) / ROT_DIM)
    ang = jnp.arange(seq, dtype=F32)[:, None] * inv_freq[None, :]
    cos, sin = jnp.cos(ang), jnp.sin(ang)
    ones = jnp.ones((seq, HEAD_DIM - ROT_DIM), F32)
    zeros = jnp.zeros((seq, HEAD_DIM - ROT_DIM), F32)
    zh = jnp.zeros((seq, ROT_HALF), F32)
    c = jnp.concatenate([cos, cos, ones], axis=1)
    sa = jnp.concatenate([-sin, zh, zeros], axis=1)
    sb = jnp.concatenate([zh, sin, zeros], axis=1)
    rep = LANES // HEAD_DIM
    return cos.T, sin.T, jnp.tile(c, (1, rep)), jnp.tile(sa, (1, rep)), jnp.tile(sb, (1, rep))


def _band_bias_table(rel_bias):
    n_keys = BAND_BLOCKS * QB
    kk = jnp.arange(n_keys)[:, None]
    q = jnp.arange(QB)[None, :]
    dist = (n_keys - QB) + q - kk
    kc = kk // CHUNK - (n_keys - QB) // CHUNK + N_PREV_CHUNKS
    qc = q // CHUNK
    valid = (kc >= qc) & (kc <= qc + N_PREV_CHUNKS)
    bias = rel_bias[:, jnp.clip(dist, -REL_CLIP, REL_CLIP) + REL_CLIP]
    return jnp.where(valid[None], bias, -jnp.inf).astype(F32)


def kernel(x, n1_g, ffn1_w_in, ffn1_w_out, n2_g, w_in, rel_bias, w_branch_a, w_branch_b, w_out,
           n3_g, ffn2_w_in, ffn2_w_out, nf_g):
    B, S, D = x.shape
    depth = n1_g.shape[0]
    cos_t, sin_t, c_tab, sa_tab, sb_tab = _rope_tables(S)
    nf = nf_g.reshape(1, D)
    offs = np.cumsum([0, WIDTH_A, WIDTH_A, WIDTH_A, WIDTH_B, WIDTH_KV_B, WIDTH_KV_B,
                      WIDTH_IDX, IDX_DIM, N_IDX_HEADS, D_MODEL, D_MODEL])
    seg = lambda w, i: w[:, offs[i]:offs[i + 1]]

    for l in range(depth):
        w = w_in[l]
        wt = jnp.concatenate(
            [seg(w, 0), seg(w, 2), seg(w, 3), seg(w, 5), seg(w, 6),
             jnp.pad(seg(w, 8), ((0, 0), (0, WI_ROWS - N_IDX_HEADS)))], axis=1).T.astype(BF16)
        wk = jnp.concatenate(
            [seg(w, 1), seg(w, 4), jnp.pad(seg(w, 7), ((0, 0), (0, LANES - IDX_DIM)))], axis=1).astype(BF16)
        wg = jnp.concatenate([seg(w, 9), seg(w, 10)], axis=1).astype(BF16)

        x2d = _ffn(x.reshape(B * S, D), n1_g[l].reshape(1, D), ffn1_w_in[l].astype(BF16),
                   ffn1_w_out[l].astype(BF16), nf, False)
        x = x2d.reshape(B, S, D)
        (qat, vat, qbt, vbt, qit, wit, ka, kb, ki, ga, gb) = _inproj(
            x, n2_g[l].reshape(1, D), wt, wk, wg, cos_t, sin_t, c_tab, sa_tab, sb_tab)
        oat = _mixer_a(qat, ka, vat, _band_bias_table(rel_bias[l]))
        obt = _mixer_b(qit, wit, qbt, ki, kb, vbt)
        x = _merge(x, oat, obt, ga, gb, w_branch_a[l].astype(BF16), w_branch_b[l].astype(BF16),
                   w_out[l].astype(BF16))
        last = l == depth - 1
        x2d = _ffn(x.reshape(B * S, D), n3_g[l].reshape(1, D), ffn2_w_in[l].astype(BF16),
                   ffn2_w_out[l].astype(BF16), nf, last)
        x = x2d.reshape(B, S, D)
    return x
```

```python
import functools

import jax
import jax.numpy as jnp
import numpy as np
from jax import lax
from jax.experimental import pallas as pl
from jax.experimental.pallas import tpu as pltpu

F32 = jnp.float32
BF16 = jnp.bfloat16

D_MODEL = 1024
D_FF = 2816
HEAD_DIM = 64
CHUNK = 64
N_PREV_CHUNKS = 8
N_HEADS_A = 8
REL_CLIP = 128
N_HEADS_B = 8
N_KV_B = 2
N_IDX_HEADS = 8
IDX_DIM = 64
TOPK_MAX = 256
ROPE_THETA = 500000.0
ROT_DIM = HEAD_DIM // 4
ROT_HALF = ROT_DIM // 2
EPS = 1e-6
WIDTH_A = N_HEADS_A * HEAD_DIM
WIDTH_B = N_HEADS_B * HEAD_DIM
WIDTH_KV_B = N_KV_B * HEAD_DIM
WIDTH_IDX = N_IDX_HEADS * IDX_DIM

LANES = 128
QK_SCALE = HEAD_DIM ** -0.5
IDX_SCALE = IDX_DIM ** -0.5

VMEM_LIMIT = 56 * 1024 * 1024

FFN_TM = 512
FFN_TF = D_FF // 2
PROJ_TM = 512
QB = 256
KT = 256
BAND_BLOCKS = N_PREV_CHUNKS * CHUNK // QB + 1
WI_ROWS = 16
ONES_ROWS = 16
NEG_BIG = -1e30


def _dot(a, b):
    return jnp.dot(a, b, preferred_element_type=F32)


def _dot_nt(a, b):
    return lax.dot_general(a, b, (((1,), (1,)), ((), ())), preferred_element_type=F32)


def _dot_tn(a, b):
    return lax.dot_general(a, b, (((0,), (0,)), ((), ())), preferred_element_type=F32)


def _rmsnorm(x, g):
    ms = jnp.mean(x * x, axis=-1, keepdims=True)
    return x * lax.rsqrt(ms + EPS) * g


def _ffn_kernel(x_ref, g_ref, wg_ref, wu_ref, wo_ref, gf_ref, o_ref, h_scr, acc_scr, *, final_norm):
    j = pl.program_id(1)

    @pl.when(j == 0)
    def _():
        h_scr[...] = _rmsnorm(x_ref[...], g_ref[...]).astype(BF16)
        acc_scr[...] = jnp.zeros_like(acc_scr)

    h = h_scr[...]
    gate = _dot(h, wg_ref[...])
    up = _dot(h, wu_ref[...])
    a = (gate * jax.nn.sigmoid(gate) * up).astype(BF16)
    acc_scr[...] += _dot(a, wo_ref[...])

    @pl.when(j == pl.num_programs(1) - 1)
    def _():
        y = x_ref[...] + 0.5 * acc_scr[...]
        if final_norm:
            y = _rmsnorm(y, gf_ref[...])
        o_ref[...] = y


def _ffn(x2d, g, w_in_bf, w_out_bf, gf, final_norm):
    T = x2d.shape[0]
    nf = D_FF // FFN_TF
    return pl.pallas_call(
        functools.partial(_ffn_kernel, final_norm=final_norm),
        grid=(T // FFN_TM, nf),
        in_specs=[
            pl.BlockSpec((FFN_TM, D_MODEL), lambda i, j: (i, 0)),
            pl.BlockSpec((1, D_MODEL), lambda i, j: (0, 0)),
            pl.BlockSpec((D_MODEL, FFN_TF), lambda i, j: (0, j)),
            pl.BlockSpec((D_MODEL, FFN_TF), lambda i, j: (0, j + nf)),
            pl.BlockSpec((FFN_TF, D_MODEL), lambda i, j: (j, 0)),
            pl.BlockSpec((1, D_MODEL), lambda i, j: (0, 0)),
        ],
        out_specs=pl.BlockSpec((FFN_TM, D_MODEL), lambda i, j: (i, 0)),
        out_shape=jax.ShapeDtypeStruct((T, D_MODEL), F32),
        scratch_shapes=[pltpu.VMEM((FFN_TM, D_MODEL), BF16), pltpu.VMEM((FFN_TM, D_MODEL), F32)],
        compiler_params=pltpu.CompilerParams(
            dimension_semantics=("parallel", "arbitrary"), vmem_limit_bytes=VMEM_LIMIT),
        name="ffn_final" if final_norm else "ffn",
    )(x2d, g, w_in_bf, w_in_bf, w_out_bf, gf)


def _rope_rows(x, cos, sin, n_heads):
    pieces = []
    for h in range(n_heads):
        r0 = h * HEAD_DIM
        t1 = x[r0:r0 + ROT_HALF]
        t2 = x[r0 + ROT_HALF:r0 + ROT_DIM]
        pieces.append(t1 * cos - t2 * sin)
        pieces.append(t2 * cos + t1 * sin)
        pieces.append(x[r0 + ROT_DIM:r0 + HEAD_DIM])
    return jnp.concatenate(pieces, axis=0)


def _rope_lanes(x, c, sa, sb):
    return x * c + pltpu.roll(x, LANES - ROT_HALF, 1) * sa + pltpu.roll(x, ROT_HALF, 1) * sb


def _inproj_kernel(x_ref, g_ref, wt_ref, wk_ref, wg_ref, cos_ref, sin_ref, c_ref, sa_ref, sb_ref,
                   qat_ref, vat_ref, qbt_ref, vbt_ref, qit_ref, wit_ref,
                   ka_ref, kb_ref, ki_ref, ga_ref, gb_ref):
    h = _rmsnorm(x_ref[...], g_ref[...]).astype(BF16)
    cos = cos_ref[...]
    sin = sin_ref[...]

    r = 0
    t = _dot_nt(wt_ref[r:r + WIDTH_A, :], h)
    qat_ref[...] = (t * QK_SCALE).astype(BF16)
    r += WIDTH_A
    vat_ref[...] = _dot_nt(wt_ref[r:r + WIDTH_A, :], h).astype(BF16)
    r += WIDTH_A
    t = _dot_nt(wt_ref[r:r + WIDTH_B, :], h)
    qbt_ref[...] = (_rope_rows(t, cos, sin, N_HEADS_B) * QK_SCALE).astype(BF16)
    r += WIDTH_B
    vbt_ref[...] = _dot_nt(wt_ref[r:r + WIDTH_KV_B, :], h).astype(BF16)
    r += WIDTH_KV_B
    t = _dot_nt(wt_ref[r:r + WIDTH_IDX, :], h)
    qit_ref[...] = (_rope_rows(t, cos, sin, N_IDX_HEADS) * IDX_SCALE).astype(BF16)
    r += WIDTH_IDX
    wit_ref[...] = _dot_nt(wt_ref[r:r + WI_ROWS, :], h) * (N_IDX_HEADS ** -0.5)

    ka_ref[...] = _dot(h, wk_ref[:, 0:WIDTH_A]).astype(BF16)
    c, sa, sb = c_ref[...], sa_ref[...], sb_ref[...]
    t = _dot(h, wk_ref[:, WIDTH_A:WIDTH_A + LANES])
    kb_ref[...] = _rope_lanes(t, c, sa, sb).astype(BF16)
    t = _dot(h, wk_ref[:, WIDTH_A + LANES:WIDTH_A + 2 * LANES])
    ki_ref[...] = _rope_lanes(t, c, sa, sb).astype(BF16)

    ga_ref[...] = _dot(h, wg_ref[:, 0:D_MODEL])
    gb_ref[...] = _dot(h, wg_ref[:, D_MODEL:2 * D_MODEL])


def _inproj(x3d, g, wt, wk, wg, cos_t, sin_t, c_tab, sa_tab, sb_tab):
    B, S, _ = x3d.shape
    tm = PROJ_TM
    full = lambda shape: pl.BlockSpec(shape, lambda b, s: (0,) * len(shape))
    tmaj = lambda w: pl.BlockSpec((None, tm, w), lambda b, s: (b, s, 0))
    fmaj = lambda w: pl.BlockSpec((None, w, tm), lambda b, s: (b, 0, s))
    tshape = lambda w, dt: jax.ShapeDtypeStruct((B, S, w), dt)
    fshape = lambda w, dt: jax.ShapeDtypeStruct((B, w, S), dt)
    return pl.pallas_call(
        _inproj_kernel,
        grid=(B, S // tm),
        in_specs=[
            tmaj(D_MODEL), full((1, D_MODEL)), full(wt.shape), full(wk.shape), full(wg.shape),
            pl.BlockSpec((ROT_HALF, tm), lambda b, s: (0, s)),
            pl.BlockSpec((ROT_HALF, tm), lambda b, s: (0, s)),
            pl.BlockSpec((tm, LANES), lambda b, s: (s, 0)),
            pl.BlockSpec((tm, LANES), lambda b, s: (s, 0)),
            pl.BlockSpec((tm, LANES), lambda b, s: (s, 0)),
        ],
        out_specs=[fmaj(WIDTH_A), fmaj(WIDTH_A), fmaj(WIDTH_B), fmaj(WIDTH_KV_B), fmaj(WIDTH_IDX),
                   fmaj(WI_ROWS), tmaj(WIDTH_A), tmaj(LANES), tmaj(LANES), tmaj(D_MODEL), tmaj(D_MODEL)],
        out_shape=[fshape(WIDTH_A, BF16), fshape(WIDTH_A, BF16), fshape(WIDTH_B, BF16),
                   fshape(WIDTH_KV_B, BF16), fshape(WIDTH_IDX, BF16), fshape(WI_ROWS, F32),
                   tshape(WIDTH_A, BF16), tshape(LANES, BF16), tshape(LANES, BF16),
                   tshape(D_MODEL, F32), tshape(D_MODEL, F32)],
        compiler_params=pltpu.CompilerParams(
            dimension_semantics=("parallel", "parallel"), vmem_limit_bytes=VMEM_LIMIT),
        name="inproj",
    )(x3d, g, wt, wk, wg, cos_t, sin_t, c_tab, sa_tab, sb_tab)


def _padded_rhs(qt_h, slot):
    z = jnp.zeros_like(qt_h)
    return jnp.concatenate([qt_h, z] if slot == 0 else [z, qt_h], axis=0)


def _mixer_a_kernel(qt_ref, k0_ref, k1_ref, k2_ref, v0_ref, v1_ref, v2_ref, bias_ref, o_ref):
    jb = pl.program_id(1)
    k_refs = (k0_ref, k1_ref, k2_ref)
    v_refs = (v0_ref, v1_ref, v2_ref)
    for h in range(N_HEADS_A):
        r0 = h * HEAD_DIM
        rhs = _padded_rhs(qt_ref[r0:r0 + HEAD_DIM, :], h % 2)
        lane0 = (h // 2) * LANES
        s = []
        for m in range(BAND_BLOCKS):
            sm = _dot(k_refs[m][:, lane0:lane0 + LANES], rhs) + bias_ref[h, m * QB:(m + 1) * QB, :]
            if m < BAND_BLOCKS - 1:
                sm = jnp.where(jb + m >= BAND_BLOCKS - 1, sm, -jnp.inf)
            s.append(sm)
        mx = s[0].max(axis=0, keepdims=True)
        for m in range(1, BAND_BLOCKS):
            mx = jnp.maximum(mx, s[m].max(axis=0, keepdims=True))
        l = jnp.zeros_like(mx)
        acc = jnp.zeros((HEAD_DIM, QB), F32)
        for m in range(BAND_BLOCKS):
            p = jnp.exp(s[m] - mx)
            l = l + p.sum(axis=0, keepdims=True)
            acc = acc + _dot(v_refs[m][r0:r0 + HEAD_DIM, :], p.astype(BF16))
        o_ref[r0:r0 + HEAD_DIM, :] = (acc / l).astype(BF16)


def _mixer_a(qat, ka, vat, bias_t):
    B, _, S = qat.shape
    nb = BAND_BLOCKS - 1
    kspec = lambda m: pl.BlockSpec((None, QB, WIDTH_A), lambda b, j: (b, jnp.maximum(j + m - nb, 0), 0))
    vspec = lambda m: pl.BlockSpec((None, WIDTH_A, QB), lambda b, j: (b, 0, jnp.maximum(j + m - nb, 0)))
    return pl.pallas_call(
        _mixer_a_kernel,
        grid=(B, S // QB),
        in_specs=[pl.BlockSpec((None, WIDTH_A, QB), lambda b, j: (b, 0, j)),
                  kspec(0), kspec(1), kspec(2), vspec(0), vspec(1), vspec(2),
                  pl.BlockSpec(bias_t.shape, lambda b, j: (0, 0, 0))],
        out_specs=pl.BlockSpec((None, WIDTH_A, QB), lambda b, j: (b, 0, j)),
        out_shape=jax.ShapeDtypeStruct((B, WIDTH_A, S), BF16),
        compiler_params=pltpu.CompilerParams(
            dimension_semantics=("parallel", "parallel"), vmem_limit_bytes=VMEM_LIMIT),
        name="mixer_a",
    )(qat, ka, ka, ka, vat, vat, vat, bias_t)


def _ordered_bits_to_f32(u):
    bits = jnp.where(u < 0, u ^ jnp.int32(-2 ** 31), ~u)
    return lax.bitcast_convert_type(bits, F32)


def _mixer_b_kernel(qit_ref, wit_ref, qbt_ref, ki_ref, kb_ref, vbt_ref, o_ref,
                    score_scr, s_scr, rhs_scr, acc_scr, m_scr, l_scr, alpha_scr, *, topk):
    jb = pl.program_id(1)
    n_tiles = jb + 1
    diag0 = pl.multiple_of(jb * KT, KT)
    q_chunk = lax.broadcasted_iota(jnp.int32, (1, QB), 1) // CHUNK
    adm_diag = lax.broadcasted_iota(jnp.int32, (KT, QB), 0) < (q_chunk + 1) * CHUNK
    zero_rows = jnp.zeros((HEAD_DIM, QB), BF16)

    def tile_loop(body, init):
        def wrapped(t, carry):
            return body(pl.multiple_of(t * KT, KT), carry)
        return lax.fori_loop(0, n_tiles, wrapped, init)

    def col_count(hit):
        return hit.reshape(KT // 8, 8, QB).sum(axis=0)

    for h in range(N_IDX_HEADS):
        rhs_scr[h, 0:IDX_DIM, :] = qit_ref[h * IDX_DIM:(h + 1) * IDX_DIM, :]
        rhs_scr[h, IDX_DIM:, :] = zero_rows
    w = wit_ref[...]

    def score_tile(k0, carry):
        ki_t = ki_ref[pl.ds(k0, KT), :]
        acc = jnp.zeros((KT, QB), F32)
        for h in range(N_IDX_HEADS):
            acc = acc + w[h:h + 1, :] * jnp.maximum(_dot(ki_t, rhs_scr[h]), 0.0)
        score_scr[pl.ds(k0, KT), :] = acc
        return carry

    tile_loop(score_tile, 0)
    score_scr[pl.ds(diag0, KT), :] = jnp.where(adm_diag, score_scr[pl.ds(diag0, KT), :], -jnp.inf)

    def count_ge(cand):
        def body(k0, cnt):
            return cnt + col_count(jnp.where(score_scr[pl.ds(k0, KT), :] >= cand, 1.0, 0.0))
        return tile_loop(body, jnp.zeros((8, QB), F32)).sum(axis=0, keepdims=True)

    neg_inf_code = jnp.int32(0x007FFFFF)

    def bit_step(i, t_u):
        cand_u = t_u | lax.shift_left(jnp.int32(1), 31 - i)
        cnt = count_ge(_ordered_bits_to_f32(cand_u))
        below = (cand_u >= 0) & (cand_u <= neg_inf_code)
        return jnp.where((cnt >= topk) | below, cand_u, t_u)

    t_u = lax.fori_loop(0, 32, bit_step, jnp.zeros((1, QB), jnp.int32))
    thr = _ordered_bits_to_f32(t_u)

    def count_both(k0, carry):
        ge, gt = carry
        sc = score_scr[pl.ds(k0, KT), :]
        return (ge + col_count(jnp.where(sc >= thr, 1.0, 0.0)),
                gt + col_count(jnp.where(sc > thr, 1.0, 0.0)))

    ge8, gt8 = tile_loop(count_both, (jnp.zeros((8, QB), F32), jnp.zeros((8, QB), F32)))
    cnt_ge = ge8.sum(axis=0, keepdims=True)
    cnt_gt = gt8.sum(axis=0, keepdims=True)
    n_inadm = (KT - (q_chunk + 1) * CHUNK).astype(F32)
    n_sel = cnt_ge - jnp.where(thr == -jnp.inf, n_inadm, 0.0)
    has_ties = jnp.max(jnp.where(n_sel > topk, 1.0, 0.0)) > 0.0

    @pl.when(jnp.logical_not(has_ties))
    def _():
        def body(k0, carry):
            score_scr[pl.ds(k0, KT), :] = jnp.where(score_scr[pl.ds(k0, KT), :] >= thr, 0.0, -jnp.inf)
            return carry
        tile_loop(body, 0)

    @pl.when(has_ties)
    def _():
        tri = jnp.where(lax.broadcasted_iota(jnp.int32, (KT, KT), 1)
                        < lax.broadcasted_iota(jnp.int32, (KT, KT), 0), 1.0, 0.0).astype(BF16)

        def body(k0, need):
            sc = score_scr[pl.ds(k0, KT), :]
            eq_f = jnp.where(sc == thr, 1.0, 0.0)
            before = _dot(tri, eq_f.astype(BF16))
            take = jnp.where(sc > thr, 1.0, jnp.where(before < need, eq_f, 0.0))
            score_scr[pl.ds(k0, KT), :] = jnp.where(take > 0.0, 0.0, -jnp.inf)
            return need - eq_f.sum(axis=0, keepdims=True)
        tile_loop(body, topk - cnt_gt)

    score_scr[pl.ds(diag0, KT), :] = jnp.where(adm_diag, score_scr[pl.ds(diag0, KT), :], -jnp.inf)

    for h in range(N_HEADS_B):
        g = h // (N_HEADS_B // N_KV_B)
        q_h = qbt_ref[h * HEAD_DIM:(h + 1) * HEAD_DIM, :]
        rhs_scr[h, 0:HEAD_DIM, :] = q_h if g == 0 else zero_rows
        rhs_scr[h, HEAD_DIM:, :] = zero_rows if g == 0 else q_h
    m_scr[...] = jnp.full_like(m_scr, NEG_BIG)
    l_scr[...] = jnp.zeros_like(l_scr)
    acc_scr[...] = jnp.zeros_like(acc_scr)
    ones_rows = jnp.ones((ONES_ROWS, KT), BF16)

    def attn_tile(k0, carry):
        neg = score_scr[pl.ds(k0, KT), :]
        kb_t = kb_ref[pl.ds(k0, KT), :]
        v_ext = [jnp.concatenate([vbt_ref[g * HEAD_DIM:(g + 1) * HEAD_DIM, pl.ds(k0, KT)], ones_rows], axis=0)
                 for g in range(N_KV_B)]
        for h in range(N_HEADS_B):
            s = _dot(kb_t, rhs_scr[h]) + neg
            s_scr[h] = s
            m_old = m_scr[h:h + 1, :]
            m_new = jnp.maximum(m_old, s.max(axis=0, keepdims=True))
            m_scr[h:h + 1, :] = m_new
            alpha_scr[h:h + 1, :] = jnp.exp(m_old - m_new)
        for h in range(N_HEADS_B):
            g = h // (N_HEADS_B // N_KV_B)
            p = jnp.exp(s_scr[h] - m_scr[h:h + 1, :]).astype(BF16)
            pv = _dot(v_ext[g], p)
            alpha = alpha_scr[h:h + 1, :]
            r0 = h * HEAD_DIM
            acc_scr[r0:r0 + HEAD_DIM, :] = alpha * acc_scr[r0:r0 + HEAD_DIM, :] + pv[0:HEAD_DIM]
            l_scr[h:h + 1, :] = alpha * l_scr[h:h + 1, :] + pv[HEAD_DIM:HEAD_DIM + 1]
        return carry

    tile_loop(attn_tile, 0)

    for h in range(N_HEADS_B):
        r0 = h * HEAD_DIM
        o_ref[r0:r0 + HEAD_DIM, :] = (acc_scr[r0:r0 + HEAD_DIM, :] / l_scr[h:h + 1, :]).astype(BF16)


def _mixer_b(qit, wit, qbt, ki, kb, vbt):
    B, _, S = qbt.shape
    topk = min(TOPK_MAX, S // 4)
    qspec = lambda w: pl.BlockSpec((None, w, QB), lambda b, j: (b, 0, j))
    return pl.pallas_call(
        functools.partial(_mixer_b_kernel, topk=float(topk)),
        grid=(B, S // QB),
        in_specs=[qspec(WIDTH_IDX), qspec(WI_ROWS), qspec(WIDTH_B),
                  pl.BlockSpec((None, S, LANES), lambda b, j: (b, 0, 0)),
                  pl.BlockSpec((None, S, LANES), lambda b, j: (b, 0, 0)),
                  pl.BlockSpec((None, WIDTH_KV_B, S), lambda b, j: (b, 0, 0))],
        out_specs=qspec(WIDTH_B),
        out_shape=jax.ShapeDtypeStruct((B, WIDTH_B, S), BF16),
        scratch_shapes=[pltpu.VMEM((S, QB), F32),
                        pltpu.VMEM((N_HEADS_B, KT, QB), F32),
                        pltpu.VMEM((N_HEADS_B, 2 * HEAD_DIM, QB), BF16),
                        pltpu.VMEM((WIDTH_B, QB), F32),
                        pltpu.VMEM((N_HEADS_B, QB), F32),
                        pltpu.VMEM((N_HEADS_B, QB), F32),
                        pltpu.VMEM((N_HEADS_B, QB), F32)],
        compiler_params=pltpu.CompilerParams(
            dimension_semantics=("parallel", "arbitrary"), vmem_limit_bytes=VMEM_LIMIT),
        name="mixer_b",
    )(qit, wit, qbt, ki, kb, vbt)


def _merge_kernel(x_ref, oat_ref, obt_ref, ga_ref, gb_ref, wa_ref, wb_ref, wo_ref, o_ref):
    ya = _dot_tn(oat_ref[...], wa_ref[...])
    yb = _dot_tn(obt_ref[...], wb_ref[...])
    merged = jax.nn.sigmoid(ga_ref[...]) * ya + jax.nn.sigmoid(gb_ref[...]) * yb
    o_ref[...] = x_ref[...] + _dot(merged.astype(BF16), wo_ref[...])


def _merge(x3d, oat, obt, ga, gb, wa, wb, wo):
    B, S, _ = x3d.shape
    tm = PROJ_TM
    tmaj = pl.BlockSpec((None, tm, D_MODEL), lambda b, s: (b, s, 0))
    fmaj = lambda w: pl.BlockSpec((None, w, tm), lambda b, s: (b, 0, s))
    full = lambda a: pl.BlockSpec(a.shape, lambda b, s: (0, 0))
    return pl.pallas_call(
        _merge_kernel,
        grid=(B, S // tm),
        in_specs=[tmaj, fmaj(WIDTH_A), fmaj(WIDTH_B), tmaj, tmaj, full(wa), full(wb), full(wo)],
        out_specs=tmaj,
        out_shape=jax.ShapeDtypeStruct((B, S, D_MODEL), F32),
        compiler_params=pltpu.CompilerParams(
            dimension_semantics=("parallel", "parallel"), vmem_limit_bytes=VMEM_LIMIT),
        name="merge",
    )(x3d, oat, obt, ga, gb, wa, wb, wo)


def _rope_tables(seq):
    inv_freq = jnp.power(jnp.float32(ROPE_THETA), -jnp.arange(0, ROT_DIM, 2, dtype=F32) / ROT_DIM)
    ang = jnp.arange(seq, dtype=F32)[:, None] * inv_freq[None, :]
    cos, sin = jnp.cos(ang), jnp.sin(ang)
    ones = jnp.ones((seq, HEAD_DIM - ROT_DIM), F32)
    zeros = jnp.zeros((seq, HEAD_DIM - ROT_DIM), F32)
    zh = jnp.zeros((seq, ROT_HALF), F32)
    c = jnp.concatenate([cos, cos, ones], axis=1)
    sa = jnp.concatenate([-sin, zh, zeros], axis=1)
    sb = jnp.concatenate([zh, sin, zeros], axis=1)
    rep = LANES // HEAD_DIM
    return cos.T, sin.T, jnp.tile(c, (1, rep)), jnp.tile(sa, (1, rep)), jnp.tile(sb, (1, rep))


def _band_bias_table(rel_bias):
    n_keys = BAND_BLOCKS * QB
    span = n_keys + QB
    n_h = rel_bias.shape[0]
    lo = jnp.broadcast_to(rel_bias[:, :1], (n_h, QB - 1 - REL_CLIP))
    hi = jnp.broadcast_to(rel_bias[:, -1:], (n_h, span - (QB - 1 - REL_CLIP) - (2 * REL_CLIP + 1)))
    ext = jnp.concatenate([lo, rel_bias, hi], axis=1)
    ext = jnp.roll(ext, -(n_keys - 1), axis=1)
    flat = jnp.tile(ext, (1, n_keys))[:, :n_keys * (span - 1)]
    bias = flat.reshape(n_h, n_keys, span - 1)[:, :, :QB]
    kc = jnp.arange(n_keys)[:, None] // CHUNK
    qc = jnp.arange(QB)[None, :] // CHUNK
    valid = (kc >= qc) & (kc <= qc + N_PREV_CHUNKS)
    return jnp.where(valid[None], bias, -jnp.inf).astype(F32)


def kernel(x, n1_g, ffn1_w_in, ffn1_w_out, n2_g, w_in, rel_bias, w_branch_a, w_branch_b, w_out,
           n3_g, ffn2_w_in, ffn2_w_out, nf_g):
    B, S, D = x.shape
    depth = n1_g.shape[0]
    cos_t, sin_t, c_tab, sa_tab, sb_tab = _rope_tables(S)
    nf = nf_g.reshape(1, D)
    offs = np.cumsum([0, WIDTH_A, WIDTH_A, WIDTH_A, WIDTH_B, WIDTH_KV_B, WIDTH_KV_B,
                      WIDTH_IDX, IDX_DIM, N_IDX_HEADS, D_MODEL, D_MODEL])
    seg = lambda w, i: w[:, offs[i]:offs[i + 1]]

    for l in range(depth):
        w = w_in[l]
        wt = jnp.concatenate(
            [seg(w, 0), seg(w, 2), seg(w, 3), seg(w, 5), seg(w, 6),
             jnp.pad(seg(w, 8), ((0, 0), (0, WI_ROWS - N_IDX_HEADS)))], axis=1).T.astype(BF16)
        wk = jnp.concatenate(
            [seg(w, 1), seg(w, 4), jnp.pad(seg(w, 7), ((0, 0), (0, LANES - IDX_DIM)))], axis=1).astype(BF16)
        wg = jnp.concatenate([seg(w, 9), seg(w, 10)], axis=1).astype(BF16)

        x2d = _ffn(x.reshape(B * S, D), n1_g[l].reshape(1, D), ffn1_w_in[l].astype(BF16),
                   ffn1_w_out[l].astype(BF16), nf, False)
        x = x2d.reshape(B, S, D)
        (qat, vat, qbt, vbt, qit, wit, ka, kb, ki, ga, gb) = _inproj(
            x, n2_g[l].reshape(1, D), wt, wk, wg, cos_t, sin_t, c_tab, sa_tab, sb_tab)
        oat = _mixer_a(qat, ka, vat, _band_bias_table(rel_bias[l]))
        obt = _mixer_b(qit, wit, qbt, ki, kb, vbt)
        x = _merge(x, oat, obt, ga, gb, w_branch_a[l].astype(BF16), w_branch_b[l].astype(BF16),
                   w_out[l].astype(BF16))
        last = l == depth - 1
        x2d = _ffn(x.reshape(B * S, D), n3_g[l].reshape(1, D), ffn2_w_in[l].astype(BF16),
                   ffn2_w_out[l].astype(BF16), nf, last)
        x = x2d.reshape(B, S, D)
    return x
```

```python
import functools

import jax
import jax.numpy as jnp
import numpy as np
from jax import lax
from jax.experimental import pallas as pl
from jax.experimental.pallas import tpu as pltpu

F32 = jnp.float32
BF16 = jnp.bfloat16

D_MODEL = 1024
D_FF = 2816
HEAD_DIM = 64
CHUNK = 64
N_PREV_CHUNKS = 8
N_HEADS_A = 8
REL_CLIP = 128
N_HEADS_B = 8
N_KV_B = 2
N_IDX_HEADS = 8
IDX_DIM = 64
TOPK_MAX = 256
ROPE_THETA = 500000.0
ROT_DIM = HEAD_DIM // 4
ROT_HALF = ROT_DIM // 2
EPS = 1e-6
WIDTH_A = N_HEADS_A * HEAD_DIM
WIDTH_B = N_HEADS_B * HEAD_DIM
WIDTH_KV_B = N_KV_B * HEAD_DIM
WIDTH_IDX = N_IDX_HEADS * IDX_DIM

LANES = 128
LOG2E = 1.4426950408889634
QK_SCALE = HEAD_DIM ** -0.5 * LOG2E
IDX_SCALE = IDX_DIM ** -0.5

VMEM_LIMIT = 56 * 1024 * 1024

FFN_TM = 512
FFN_TF = D_FF // 2
PROJ_TM = 512
QB = 256
KT = 256
BAND_BLOCKS = N_PREV_CHUNKS * CHUNK // QB + 1
BIAS_SPAN = (BAND_BLOCKS + 1) * QB
WI_ROWS = 16
ONES_ROWS = 16
NEG_BIG = -1e30


def _dot(a, b):
    return jnp.dot(a, b, preferred_element_type=F32)


def _dot_nt(a, b):
    return lax.dot_general(a, b, (((1,), (1,)), ((), ())), preferred_element_type=F32)


def _dot_tn(a, b):
    return lax.dot_general(a, b, (((0,), (0,)), ((), ())), preferred_element_type=F32)


def _rmsnorm(x, g):
    ms = jnp.mean(x * x, axis=-1, keepdims=True)
    return x * lax.rsqrt(ms + EPS) * g


def _ffn_kernel(x_ref, g_ref, wg_ref, wu_ref, wo_ref, gf_ref, o_ref, h_scr, acc_scr, *, final_norm):
    j = pl.program_id(1)

    @pl.when(j == 0)
    def _():
        h_scr[...] = _rmsnorm(x_ref[...], g_ref[...]).astype(BF16)
        acc_scr[...] = jnp.zeros_like(acc_scr)

    h = h_scr[...]
    gate = _dot(h, wg_ref[...])
    up = _dot(h, wu_ref[...])
    a = (gate * jax.nn.sigmoid(gate) * up).astype(BF16)
    acc_scr[...] += _dot(a, wo_ref[...])

    @pl.when(j == pl.num_programs(1) - 1)
    def _():
        y = x_ref[...] + 0.5 * acc_scr[...]
        if final_norm:
            y = _rmsnorm(y, gf_ref[...])
        o_ref[...] = y


def _ffn(x2d, g, w_in_bf, w_out_bf, gf, final_norm):
    T = x2d.shape[0]
    nf = D_FF // FFN_TF
    return pl.pallas_call(
        functools.partial(_ffn_kernel, final_norm=final_norm),
        grid=(T // FFN_TM, nf),
        in_specs=[
            pl.BlockSpec((FFN_TM, D_MODEL), lambda i, j: (i, 0)),
            pl.BlockSpec((1, D_MODEL), lambda i, j: (0, 0)),
            pl.BlockSpec((D_MODEL, FFN_TF), lambda i, j: (0, j)),
            pl.BlockSpec((D_MODEL, FFN_TF), lambda i, j: (0, j + nf)),
            pl.BlockSpec((FFN_TF, D_MODEL), lambda i, j: (j, 0)),
            pl.BlockSpec((1, D_MODEL), lambda i, j: (0, 0)),
        ],
        out_specs=pl.BlockSpec((FFN_TM, D_MODEL), lambda i, j: (i, 0)),
        out_shape=jax.ShapeDtypeStruct((T, D_MODEL), F32),
        scratch_shapes=[pltpu.VMEM((FFN_TM, D_MODEL), BF16), pltpu.VMEM((FFN_TM, D_MODEL), F32)],
        compiler_params=pltpu.CompilerParams(
            dimension_semantics=("parallel", "arbitrary"), vmem_limit_bytes=VMEM_LIMIT),
        name="ffn_final" if final_norm else "ffn",
    )(x2d, g, w_in_bf, w_in_bf, w_out_bf, gf)


def _rope_rows(x, cos, sin, n_heads):
    pieces = []
    for h in range(n_heads):
        r0 = h * HEAD_DIM
        t1 = x[r0:r0 + ROT_HALF]
        t2 = x[r0 + ROT_HALF:r0 + ROT_DIM]
        pieces.append(t1 * cos - t2 * sin)
        pieces.append(t2 * cos + t1 * sin)
        pieces.append(x[r0 + ROT_DIM:r0 + HEAD_DIM])
    return jnp.concatenate(pieces, axis=0)


def _rope_lanes(x, c, sa, sb):
    return x * c + pltpu.roll(x, LANES - ROT_HALF, 1) * sa + pltpu.roll(x, ROT_HALF, 1) * sb


def _inproj_kernel(x_ref, g_ref, wt_ref, wk_ref, wg_ref, cos_ref, sin_ref, c_ref, sa_ref, sb_ref,
                   qat_ref, vat_ref, qbt_ref, vbt_ref, qit_ref, wit_ref,
                   ka_ref, kb_ref, ki_ref, ga_ref, gb_ref):
    h = _rmsnorm(x_ref[...], g_ref[...]).astype(BF16)
    cos = cos_ref[...]
    sin = sin_ref[...]

    r = 0
    t = _dot_nt(wt_ref[r:r + WIDTH_A, :], h)
    qat_ref[...] = (t * QK_SCALE).astype(BF16)
    r += WIDTH_A
    vat_ref[...] = _dot_nt(wt_ref[r:r + WIDTH_A, :], h).astype(BF16)
    r += WIDTH_A
    t = _dot_nt(wt_ref[r:r + WIDTH_B, :], h)
    qbt_ref[...] = (_rope_rows(t, cos, sin, N_HEADS_B) * QK_SCALE).astype(BF16)
    r += WIDTH_B
    vbt_ref[...] = _dot_nt(wt_ref[r:r + WIDTH_KV_B, :], h).astype(BF16)
    r += WIDTH_KV_B
    t = _dot_nt(wt_ref[r:r + WIDTH_IDX, :], h)
    qit_ref[...] = (_rope_rows(t, cos, sin, N_IDX_HEADS) * IDX_SCALE).astype(BF16)
    r += WIDTH_IDX
    wit_ref[...] = _dot_nt(wt_ref[r:r + WI_ROWS, :], h) * (N_IDX_HEADS ** -0.5)

    ka_ref[...] = _dot(h, wk_ref[:, 0:WIDTH_A]).astype(BF16)
    c, sa, sb = c_ref[...], sa_ref[...], sb_ref[...]
    t = _dot(h, wk_ref[:, WIDTH_A:WIDTH_A + LANES])
    kb_ref[...] = _rope_lanes(t, c, sa, sb).astype(BF16)
    t = _dot(h, wk_ref[:, WIDTH_A + LANES:WIDTH_A + 2 * LANES])
    ki_ref[...] = _rope_lanes(t, c, sa, sb).astype(BF16)

    ga_ref[...] = _dot(h, wg_ref[:, 0:D_MODEL])
    gb_ref[...] = _dot(h, wg_ref[:, D_MODEL:2 * D_MODEL])


def _inproj(x3d, g, wt, wk, wg, cos_t, sin_t, c_tab, sa_tab, sb_tab):
    B, S, _ = x3d.shape
    tm = PROJ_TM
    full = lambda shape: pl.BlockSpec(shape, lambda b, s: (0,) * len(shape))
    tmaj = lambda w: pl.BlockSpec((None, tm, w), lambda b, s: (b, s, 0))
    fmaj = lambda w: pl.BlockSpec((None, w, tm), lambda b, s: (b, 0, s))
    tshape = lambda w, dt: jax.ShapeDtypeStruct((B, S, w), dt)
    fshape = lambda w, dt: jax.ShapeDtypeStruct((B, w, S), dt)
    return pl.pallas_call(
        _inproj_kernel,
        grid=(B, S // tm),
        in_specs=[
            tmaj(D_MODEL), full((1, D_MODEL)), full(wt.shape), full(wk.shape), full(wg.shape),
            pl.BlockSpec((ROT_HALF, tm), lambda b, s: (0, s)),
            pl.BlockSpec((ROT_HALF, tm), lambda b, s: (0, s)),
            pl.BlockSpec((tm, LANES), lambda b, s: (s, 0)),
            pl.BlockSpec((tm, LANES), lambda b, s: (s, 0)),
            pl.BlockSpec((tm, LANES), lambda b, s: (s, 0)),
        ],
        out_specs=[fmaj(WIDTH_A), fmaj(WIDTH_A), fmaj(WIDTH_B), fmaj(WIDTH_KV_B), fmaj(WIDTH_IDX),
                   fmaj(WI_ROWS), tmaj(WIDTH_A), tmaj(LANES), tmaj(LANES), tmaj(D_MODEL), tmaj(D_MODEL)],
        out_shape=[fshape(WIDTH_A, BF16), fshape(WIDTH_A, BF16), fshape(WIDTH_B, BF16),
                   fshape(WIDTH_KV_B, BF16), fshape(WIDTH_IDX, BF16), fshape(WI_ROWS, F32),
                   tshape(WIDTH_A, BF16), tshape(LANES, BF16), tshape(LANES, BF16),
                   tshape(D_MODEL, F32), tshape(D_MODEL, F32)],
        compiler_params=pltpu.CompilerParams(
            dimension_semantics=("parallel", "parallel"), vmem_limit_bytes=VMEM_LIMIT),
        name="inproj",
    )(x3d, g, wt, wk, wg, cos_t, sin_t, c_tab, sa_tab, sb_tab)


def _padded_rhs(qt_h, slot):
    z = jnp.zeros_like(qt_h)
    return jnp.concatenate([qt_h, z] if slot == 0 else [z, qt_h], axis=0)


def _mixer_a_kernel(qt_ref, k0_ref, k1_ref, k2_ref, v0_ref, v1_ref, v2_ref, ext_ref, o_ref,
                    bias_scr, s_scr):
    jb = pl.program_id(1)
    k_refs = (k0_ref, k1_ref, k2_ref)
    v_refs = (v0_ref, v1_ref, v2_ref)

    @pl.when((pl.program_id(0) == 0) & (jb == 0))
    def _():
        qc = lax.broadcasted_iota(jnp.int32, (QB, QB), 1) // CHUNK
        for m in range(BAND_BLOCKS):
            kc = lax.broadcasted_iota(jnp.int32, (QB, QB), 0) // CHUNK + m * (QB // CHUNK)
            valid = (kc >= qc) & (kc <= qc + N_PREV_CHUNKS)
            for h in range(N_HEADS_A):
                rows = jnp.broadcast_to(ext_ref[h:h + 1, :], (QB, BIAS_SPAN))
                skew = pltpu.roll(rows, m * QB, 1, stride=1, stride_axis=0)
                bias_scr[h, m * QB:(m + 1) * QB, :] = jnp.where(valid, skew[:, 0:QB] * LOG2E, -jnp.inf)

    ones_rows = jnp.ones((ONES_ROWS, QB), BF16)
    for h in range(N_HEADS_A):
        r0 = h * HEAD_DIM
        rhs = _padded_rhs(qt_ref[r0:r0 + HEAD_DIM, :], h % 2)
        lane0 = (h // 2) * LANES
        slot = h % 2
        mx = None
        for m in range(BAND_BLOCKS):
            sm = _dot(k_refs[m][:, lane0:lane0 + LANES], rhs) + bias_scr[h, m * QB:(m + 1) * QB, :]
            if m < BAND_BLOCKS - 1:
                sm = jnp.where(jb + m >= BAND_BLOCKS - 1, sm, -jnp.inf)
            s_scr[slot, m] = sm
            bm = sm.max(axis=0, keepdims=True)
            mx = bm if mx is None else jnp.maximum(mx, bm)
        acc = jnp.zeros((HEAD_DIM + ONES_ROWS, QB), F32)
        for m in range(BAND_BLOCKS):
            p = jnp.exp2(s_scr[slot, m] - mx).astype(BF16)
            v_ext = jnp.concatenate([v_refs[m][r0:r0 + HEAD_DIM, :], ones_rows], axis=0)
            acc = acc + _dot(v_ext, p)
        o_ref[r0:r0 + HEAD_DIM, :] = (acc[0:HEAD_DIM] / acc[HEAD_DIM:HEAD_DIM + 1]).astype(BF16)


def _mixer_a(qat, ka, vat, bias_ext):
    B, _, S = qat.shape
    nb = BAND_BLOCKS - 1
    kspec = lambda m: pl.BlockSpec((None, QB, WIDTH_A), lambda b, j: (b, jnp.maximum(j + m - nb, 0), 0))
    vspec = lambda m: pl.BlockSpec((None, WIDTH_A, QB), lambda b, j: (b, 0, jnp.maximum(j + m - nb, 0)))
    return pl.pallas_call(
        _mixer_a_kernel,
        grid=(B, S // QB),
        in_specs=[pl.BlockSpec((None, WIDTH_A, QB), lambda b, j: (b, 0, j)),
                  kspec(0), kspec(1), kspec(2), vspec(0), vspec(1), vspec(2),
                  pl.BlockSpec(bias_ext.shape, lambda b, j: (0, 0))],
        out_specs=pl.BlockSpec((None, WIDTH_A, QB), lambda b, j: (b, 0, j)),
        out_shape=jax.ShapeDtypeStruct((B, WIDTH_A, S), BF16),
        scratch_shapes=[pltpu.VMEM((N_HEADS_A, BAND_BLOCKS * QB, QB), F32),
                        pltpu.VMEM((2, BAND_BLOCKS, QB, QB), F32)],
        compiler_params=pltpu.CompilerParams(
            dimension_semantics=("arbitrary", "arbitrary"), vmem_limit_bytes=VMEM_LIMIT),
        name="mixer_a",
    )(qat, ka, ka, ka, vat, vat, vat, bias_ext)


def _ordered_bits_to_f32(u):
    bits = jnp.where(u < 0, u ^ jnp.int32(-2 ** 31), ~u)
    return lax.bitcast_convert_type(bits, F32)


def _mixer_b_kernel(qit_ref, wit_ref, qbt_ref, ki_ref, kb_ref, vbt_ref, o_ref,
                    score_scr, s_scr, rhs_scr, acc_scr, m_scr, l_scr, alpha_scr, *, topk):
    jb = pl.program_id(1)
    n_tiles = jb + 1
    diag0 = pl.multiple_of(jb * KT, KT)
    q_chunk = lax.broadcasted_iota(jnp.int32, (1, QB), 1) // CHUNK
    adm_diag = lax.broadcasted_iota(jnp.int32, (KT, QB), 0) < (q_chunk + 1) * CHUNK
    zero_rows = jnp.zeros((HEAD_DIM, QB), BF16)

    def tile_loop(body, init):
        def wrapped(t, carry):
            return body(pl.multiple_of(t * KT, KT), carry)
        return lax.fori_loop(0, n_tiles, wrapped, init)

    def col_count(hit):
        return hit.reshape(KT // 8, 8, QB).sum(axis=0)

    for h in range(N_IDX_HEADS):
        rhs_scr[h, 0:IDX_DIM, :] = qit_ref[h * IDX_DIM:(h + 1) * IDX_DIM, :]
        rhs_scr[h, IDX_DIM:, :] = zero_rows
    w = wit_ref[...]

    def score_tile(k0, carry):
        ki_t = ki_ref[pl.ds(k0, KT), :]
        acc = jnp.zeros((KT, QB), F32)
        for h in range(N_IDX_HEADS):
            acc = acc + w[h:h + 1, :] * jnp.maximum(_dot(ki_t, rhs_scr[h]), 0.0)
        score_scr[pl.ds(k0, KT), :] = acc
        return carry

    tile_loop(score_tile, 0)
    score_scr[pl.ds(diag0, KT), :] = jnp.where(adm_diag, score_scr[pl.ds(diag0, KT), :], -jnp.inf)

    def count_ge(cand):
        def body(k0, cnt):
            return cnt + col_count(jnp.where(score_scr[pl.ds(k0, KT), :] >= cand, 1.0, 0.0))
        return tile_loop(body, jnp.zeros((8, QB), F32)).sum(axis=0, keepdims=True)

    neg_inf_code = jnp.int32(0x007FFFFF)

    def bit_step(i, t_u):
        cand_u = t_u | lax.shift_left(jnp.int32(1), 31 - i)
        cnt = count_ge(_ordered_bits_to_f32(cand_u))
        below = (cand_u >= 0) & (cand_u <= neg_inf_code)
        return jnp.where((cnt >= topk) | below, cand_u, t_u)

    t_u = lax.fori_loop(0, 32, bit_step, jnp.zeros((1, QB), jnp.int32))
    thr = _ordered_bits_to_f32(t_u)

    def count_both(k0, carry):
        ge, gt = carry
        sc = score_scr[pl.ds(k0, KT), :]
        return (ge + col_count(jnp.where(sc >= thr, 1.0, 0.0)),
                gt + col_count(jnp.where(sc > thr, 1.0, 0.0)))

    ge8, gt8 = tile_loop(count_both, (jnp.zeros((8, QB), F32), jnp.zeros((8, QB), F32)))
    cnt_ge = ge8.sum(axis=0, keepdims=True)
    cnt_gt = gt8.sum(axis=0, keepdims=True)
    n_inadm = (KT - (q_chunk + 1) * CHUNK).astype(F32)
    n_sel = cnt_ge - jnp.where(thr == -jnp.inf, n_inadm, 0.0)
    has_ties = jnp.max(jnp.where(n_sel > topk, 1.0, 0.0)) > 0.0

    @pl.when(jnp.logical_not(has_ties))
    def _():
        def body(k0, carry):
            score_scr[pl.ds(k0, KT), :] = jnp.where(score_scr[pl.ds(k0, KT), :] >= thr, 0.0, -jnp.inf)
            return carry
        tile_loop(body, 0)

    @pl.when(has_ties)
    def _():
        tri = jnp.where(lax.broadcasted_iota(jnp.int32, (KT, KT), 1)
                        < lax.broadcasted_iota(jnp.int32, (KT, KT), 0), 1.0, 0.0).astype(BF16)

        def body(k0, need):
            sc = score_scr[pl.ds(k0, KT), :]
            eq_f = jnp.where(sc == thr, 1.0, 0.0)
            before = _dot(tri, eq_f.astype(BF16))
            take = jnp.where(sc > thr, 1.0, jnp.where(before < need, eq_f, 0.0))
            score_scr[pl.ds(k0, KT), :] = jnp.where(take > 0.0, 0.0, -jnp.inf)
            return need - eq_f.sum(axis=0, keepdims=True)
        tile_loop(body, topk - cnt_gt)

    score_scr[pl.ds(diag0, KT), :] = jnp.where(adm_diag, score_scr[pl.ds(diag0, KT), :], -jnp.inf)

    for h in range(N_HEADS_B):
        g = h // (N_HEADS_B // N_KV_B)
        q_h = qbt_ref[h * HEAD_DIM:(h + 1) * HEAD_DIM, :]
        rhs_scr[h, 0:HEAD_DIM, :] = q_h if g == 0 else zero_rows
        rhs_scr[h, HEAD_DIM:, :] = zero_rows if g == 0 else q_h
    m_scr[...] = jnp.full_like(m_scr, NEG_BIG)
    l_scr[...] = jnp.zeros_like(l_scr)
    acc_scr[...] = jnp.zeros_like(acc_scr)
    ones_rows = jnp.ones((ONES_ROWS, KT), BF16)

    def attn_tile(k0, carry):
        neg = score_scr[pl.ds(k0, KT), :]
        kb_t = kb_ref[pl.ds(k0, KT), :]
        v_ext = [jnp.concatenate([vbt_ref[g * HEAD_DIM:(g + 1) * HEAD_DIM, pl.ds(k0, KT)], ones_rows], axis=0)
                 for g in range(N_KV_B)]
        for h in range(N_HEADS_B):
            s = _dot(kb_t, rhs_scr[h]) + neg
            s_scr[h] = s
            m_old = m_scr[h:h + 1, :]
            m_new = jnp.maximum(m_old, s.max(axis=0, keepdims=True))
            m_scr[h:h + 1, :] = m_new
            alpha_scr[h:h + 1, :] = jnp.exp2(m_old - m_new)
        for h in range(N_HEADS_B):
            g = h // (N_HEADS_B // N_KV_B)
            p = jnp.exp2(s_scr[h] - m_scr[h:h + 1, :]).astype(BF16)
            pv = _dot(v_ext[g], p)
            alpha = alpha_scr[h:h + 1, :]
            r0 = h * HEAD_DIM
            acc_scr[r0:r0 + HEAD_DIM, :] = alpha * acc_scr[r0:r0 + HEAD_DIM, :] + pv[0:HEAD_DIM]
            l_scr[h:h + 1, :] = alpha * l_scr[h:h + 1, :] + pv[HEAD_DIM:HEAD_DIM + 1]
        return carry

    tile_loop(attn_tile, 0)

    for h in range(N_HEADS_B):
        r0 = h * HEAD_DIM
        o_ref[r0:r0 + HEAD_DIM, :] = (acc_scr[r0:r0 + HEAD_DIM, :] / l_scr[h:h + 1, :]).astype(BF16)


def _mixer_b(qit, wit, qbt, ki, kb, vbt):
    B, _, S = qbt.shape
    topk = min(TOPK_MAX, S // 4)
    qspec = lambda w: pl.BlockSpec((None, w, QB), lambda b, j: (b, 0, j))
    return pl.pallas_call(
        functools.partial(_mixer_b_kernel, topk=float(topk)),
        grid=(B, S // QB),
        in_specs=[qspec(WIDTH_IDX), qspec(WI_ROWS), qspec(WIDTH_B),
                  pl.BlockSpec((None, S, LANES), lambda b, j: (b, 0, 0)),
                  pl.BlockSpec((None, S, LANES), lambda b, j: (b, 0, 0)),
                  pl.BlockSpec((None, WIDTH_KV_B, S), lambda b, j: (b, 0, 0))],
        out_specs=qspec(WIDTH_B),
        out_shape=jax.ShapeDtypeStruct((B, WIDTH_B, S), BF16),
        scratch_shapes=[pltpu.VMEM((S, QB), F32),
                        pltpu.VMEM((N_HEADS_B, KT, QB), F32),
                        pltpu.VMEM((N_HEADS_B, 2 * HEAD_DIM, QB), BF16),
                        pltpu.VMEM((WIDTH_B, QB), F32),
                        pltpu.VMEM((N_HEADS_B, QB), F32),
                        pltpu.VMEM((N_HEADS_B, QB), F32),
                        pltpu.VMEM((N_HEADS_B, QB), F32)],
        compiler_params=pltpu.CompilerParams(
            dimension_semantics=("parallel", "arbitrary"), vmem_limit_bytes=VMEM_LIMIT),
        name="mixer_b",
    )(qit, wit, qbt, ki, kb, vbt)


def _merge_kernel(x_ref, oat_ref, obt_ref, ga_ref, gb_ref, wa_ref, wb_ref, wo_ref, o_ref):
    ya = _dot_tn(oat_ref[...], wa_ref[...])
    yb = _dot_tn(obt_ref[...], wb_ref[...])
    merged = jax.nn.sigmoid(ga_ref[...]) * ya + jax.nn.sigmoid(gb_ref[...]) * yb
    o_ref[...] = x_ref[...] + _dot(merged.astype(BF16), wo_ref[...])


def _merge(x3d, oat, obt, ga, gb, wa, wb, wo):
    B, S, _ = x3d.shape
    tm = PROJ_TM
    tmaj = pl.BlockSpec((None, tm, D_MODEL), lambda b, s: (b, s, 0))
    fmaj = lambda w: pl.BlockSpec((None, w, tm), lambda b, s: (b, 0, s))
    full = lambda a: pl.BlockSpec(a.shape, lambda b, s: (0, 0))
    return pl.pallas_call(
        _merge_kernel,
        grid=(B, S // tm),
        in_specs=[tmaj, fmaj(WIDTH_A), fmaj(WIDTH_B), tmaj, tmaj, full(wa), full(wb), full(wo)],
        out_specs=tmaj,
        out_shape=jax.ShapeDtypeStruct((B, S, D_MODEL), F32),
        compiler_params=pltpu.CompilerParams(
            dimension_semantics=("parallel", "parallel"), vmem_limit_bytes=VMEM_LIMIT),
        name="merge",
    )(x3d, oat, obt, ga, gb, wa, wb, wo)


def _rope_tables(seq):
    inv_freq = jnp.power(jnp.float32(ROPE_THETA), -jnp.arange(0, ROT_DIM, 2, dtype=F32) / ROT_DIM)
    ang = jnp.arange(seq, dtype=F32)[:, None] * inv_freq[None, :]
    cos, sin = jnp.cos(ang), jnp.sin(ang)
    ones = jnp.ones((seq, HEAD_DIM - ROT_DIM), F32)
    zeros = jnp.zeros((seq, HEAD_DIM - ROT_DIM), F32)
    zh = jnp.zeros((seq, ROT_HALF), F32)
    c = jnp.concatenate([cos, cos, ones], axis=1)
    sa = jnp.concatenate([-sin, zh, zeros], axis=1)
    sb = jnp.concatenate([zh, sin, zeros], axis=1)
    rep = LANES // HEAD_DIM
    return cos.T, sin.T, jnp.tile(c, (1, rep)), jnp.tile(sa, (1, rep)), jnp.tile(sb, (1, rep))


def _band_bias_ext(rel_bias):
    n_keys = BAND_BLOCKS * QB
    n_h = rel_bias.shape[0]
    lo = jnp.broadcast_to(rel_bias[:, :1], (n_h, QB - 1 - REL_CLIP))
    hi = jnp.broadcast_to(rel_bias[:, -1:], (n_h, BIAS_SPAN - (QB - 1 - REL_CLIP) - (2 * REL_CLIP + 1)))
    ext = jnp.concatenate([lo, rel_bias, hi], axis=1)
    return jnp.roll(ext, -(n_keys - 1), axis=1).astype(F32)


def kernel(x, n1_g, ffn1_w_in, ffn1_w_out, n2_g, w_in, rel_bias, w_branch_a, w_branch_b, w_out,
           n3_g, ffn2_w_in, ffn2_w_out, nf_g):
    B, S, D = x.shape
    depth = n1_g.shape[0]
    cos_t, sin_t, c_tab, sa_tab, sb_tab = _rope_tables(S)
    nf = nf_g.reshape(1, D)
    offs = np.cumsum([0, WIDTH_A, WIDTH_A, WIDTH_A, WIDTH_B, WIDTH_KV_B, WIDTH_KV_B,
                      WIDTH_IDX, IDX_DIM, N_IDX_HEADS, D_MODEL, D_MODEL])
    seg = lambda w, i: w[:, offs[i]:offs[i + 1]]

    for l in range(depth):
        w = w_in[l]
        wt = jnp.concatenate(
            [seg(w, 0), seg(w, 2), seg(w, 3), seg(w, 5), seg(w, 6),
             jnp.pad(seg(w, 8), ((0, 0), (0, WI_ROWS - N_IDX_HEADS)))], axis=1).T.astype(BF16)
        wk = jnp.concatenate(
            [seg(w, 1), seg(w, 4), jnp.pad(seg(w, 7), ((0, 0), (0, LANES - IDX_DIM)))], axis=1).astype(BF16)
        wg = jnp.concatenate([seg(w, 9), seg(w, 10)], axis=1).astype(BF16)

        x2d = _ffn(x.reshape(B * S, D), n1_g[l].reshape(1, D), ffn1_w_in[l].astype(BF16),
                   ffn1_w_out[l].astype(BF16), nf, False)
        x = x2d.reshape(B, S, D)
        (qat, vat, qbt, vbt, qit, wit, ka, kb, ki, ga, gb) = _inproj(
            x, n2_g[l].reshape(1, D), wt, wk, wg, cos_t, sin_t, c_tab, sa_tab, sb_tab)
        oat = _mixer_a(qat, ka, vat, _band_bias_ext(rel_bias[l]))
        obt = _mixer_b(qit, wit, qbt, ki, kb, vbt)
        x = _merge(x, oat, obt, ga, gb, w_branch_a[l].astype(BF16), w_branch_b[l].astype(BF16),
                   w_out[l].astype(BF16))
        last = l == depth - 1
        x2d = _ffn(x.reshape(B * S, D), n3_g[l].reshape(1, D), ffn2_w_in[l].astype(BF16),
                   ffn2_w_out[l].astype(BF16), nf, last)
        x = x2d.reshape(B, S, D)
    return x
```

```python
import functools

import jax
import jax.numpy as jnp
import numpy as np
from jax import lax
from jax.experimental import pallas as pl
from jax.experimental.pallas import tpu as pltpu

F32 = jnp.float32
BF16 = jnp.bfloat16

D_MODEL = 1024
D_FF = 2816
HEAD_DIM = 64
CHUNK = 64
N_PREV_CHUNKS = 8
N_HEADS_A = 8
REL_CLIP = 128
N_HEADS_B = 8
N_KV_B = 2
N_IDX_HEADS = 8
IDX_DIM = 64
TOPK_MAX = 256
ROPE_THETA = 500000.0
ROT_DIM = HEAD_DIM // 4
ROT_HALF = ROT_DIM // 2
EPS = 1e-6
WIDTH_A = N_HEADS_A * HEAD_DIM
WIDTH_B = N_HEADS_B * HEAD_DIM
WIDTH_KV_B = N_KV_B * HEAD_DIM
WIDTH_IDX = N_IDX_HEADS * IDX_DIM

LANES = 128
LOG2E = 1.4426950408889634
QK_SCALE = HEAD_DIM ** -0.5 * LOG2E
IDX_SCALE = IDX_DIM ** -0.5

VMEM_LIMIT = 56 * 1024 * 1024

FFN_TM = 512
FFN_TF = D_FF // 2
PROJ_TM = 512
QB = 256
KT = 256
BAND_BLOCKS = N_PREV_CHUNKS * CHUNK // QB + 1
BIAS_SPAN = (BAND_BLOCKS + 1) * QB
WI_ROWS = 16
ONES_ROWS = 16
NEG_BIG = -1e30


def _dot(a, b):
    return jnp.dot(a, b, preferred_element_type=F32)


def _dot_nt(a, b):
    return lax.dot_general(a, b, (((1,), (1,)), ((), ())), preferred_element_type=F32)


def _dot_tn(a, b):
    return lax.dot_general(a, b, (((0,), (0,)), ((), ())), preferred_element_type=F32)


def _rmsnorm(x, g):
    ms = jnp.mean(x * x, axis=-1, keepdims=True)
    return x * lax.rsqrt(ms + EPS) * g


def _ffn_kernel(x_ref, g_ref, wg_ref, wu_ref, wo_ref, gf_ref, o_ref, h_scr, acc_scr, *, final_norm):
    j = pl.program_id(1)

    @pl.when(j == 0)
    def _():
        h_scr[...] = _rmsnorm(x_ref[...], g_ref[...]).astype(BF16)
        acc_scr[...] = jnp.zeros_like(acc_scr)

    h = h_scr[...]
    gate = _dot(h, wg_ref[...])
    up = _dot(h, wu_ref[...])
    a = (gate * jax.nn.sigmoid(gate) * up).astype(BF16)
    acc_scr[...] += _dot(a, wo_ref[...])

    @pl.when(j == pl.num_programs(1) - 1)
    def _():
        y = x_ref[...] + 0.5 * acc_scr[...]
        if final_norm:
            y = _rmsnorm(y, gf_ref[...])
        o_ref[...] = y


def _ffn(x2d, g, w_in_bf, w_out_bf, gf, final_norm):
    T = x2d.shape[0]
    nf = D_FF // FFN_TF
    return pl.pallas_call(
        functools.partial(_ffn_kernel, final_norm=final_norm),
        grid=(T // FFN_TM, nf),
        in_specs=[
            pl.BlockSpec((FFN_TM, D_MODEL), lambda i, j: (i, 0)),
            pl.BlockSpec((1, D_MODEL), lambda i, j: (0, 0)),
            pl.BlockSpec((D_MODEL, FFN_TF), lambda i, j: (0, j)),
            pl.BlockSpec((D_MODEL, FFN_TF), lambda i, j: (0, j + nf)),
            pl.BlockSpec((FFN_TF, D_MODEL), lambda i, j: (j, 0)),
            pl.BlockSpec((1, D_MODEL), lambda i, j: (0, 0)),
        ],
        out_specs=pl.BlockSpec((FFN_TM, D_MODEL), lambda i, j: (i, 0)),
        out_shape=jax.ShapeDtypeStruct((T, D_MODEL), F32),
        scratch_shapes=[pltpu.VMEM((FFN_TM, D_MODEL), BF16), pltpu.VMEM((FFN_TM, D_MODEL), F32)],
        compiler_params=pltpu.CompilerParams(
            dimension_semantics=("parallel", "arbitrary"), vmem_limit_bytes=VMEM_LIMIT),
        name="ffn_final" if final_norm else "ffn",
    )(x2d, g, w_in_bf, w_in_bf, w_out_bf, gf)


def _rope_rows(x, cos, sin, n_heads):
    pieces = []
    for h in range(n_heads):
        r0 = h * HEAD_DIM
        t1 = x[r0:r0 + ROT_HALF]
        t2 = x[r0 + ROT_HALF:r0 + ROT_DIM]
        pieces.append(t1 * cos - t2 * sin)
        pieces.append(t2 * cos + t1 * sin)
        pieces.append(x[r0 + ROT_DIM:r0 + HEAD_DIM])
    return jnp.concatenate(pieces, axis=0)


def _rope_lanes(x, c, sa, sb):
    return x * c + pltpu.roll(x, LANES - ROT_HALF, 1) * sa + pltpu.roll(x, ROT_HALF, 1) * sb


def _inproj_kernel(x_ref, g_ref, wt_ref, wk_ref, wg_ref, cos_ref, sin_ref, c_ref, sa_ref, sb_ref,
                   qat_ref, vat_ref, qbt_ref, vbt_ref, qit_ref, wit_ref,
                   ka_ref, kb_ref, ki_ref, ga_ref, gb_ref):
    h = _rmsnorm(x_ref[...], g_ref[...]).astype(BF16)
    cos = cos_ref[...]
    sin = sin_ref[...]

    r = 0
    t = _dot_nt(wt_ref[r:r + WIDTH_A, :], h)
    qat_ref[...] = (t * QK_SCALE).astype(BF16)
    r += WIDTH_A
    vat_ref[...] = _dot_nt(wt_ref[r:r + WIDTH_A, :], h).astype(BF16)
    r += WIDTH_A
    t = _dot_nt(wt_ref[r:r + WIDTH_B, :], h)
    qbt_ref[...] = (_rope_rows(t, cos, sin, N_HEADS_B) * QK_SCALE).astype(BF16)
    r += WIDTH_B
    vbt_ref[...] = _dot_nt(wt_ref[r:r + WIDTH_KV_B, :], h).astype(BF16)
    r += WIDTH_KV_B
    t = _dot_nt(wt_ref[r:r + WIDTH_IDX, :], h)
    qit_ref[...] = (_rope_rows(t, cos, sin, N_IDX_HEADS) * IDX_SCALE).astype(BF16)
    r += WIDTH_IDX
    wit_ref[...] = _dot_nt(wt_ref[r:r + WI_ROWS, :], h) * (N_IDX_HEADS ** -0.5)

    ka_ref[...] = _dot(h, wk_ref[:, 0:WIDTH_A]).astype(BF16)
    c, sa, sb = c_ref[...], sa_ref[...], sb_ref[...]
    t = _dot(h, wk_ref[:, WIDTH_A:WIDTH_A + LANES])
    kb_ref[...] = _rope_lanes(t, c, sa, sb).astype(BF16)
    t = _dot(h, wk_ref[:, WIDTH_A + LANES:WIDTH_A + 2 * LANES])
    ki_ref[...] = _rope_lanes(t, c, sa, sb).astype(BF16)

    ga_ref[...] = _dot(h, wg_ref[:, 0:D_MODEL])
    gb_ref[...] = _dot(h, wg_ref[:, D_MODEL:2 * D_MODEL])


def _inproj(x3d, g, wt, wk, wg, cos_t, sin_t, c_tab, sa_tab, sb_tab):
    B, S, _ = x3d.shape
    tm = PROJ_TM
    full = lambda shape: pl.BlockSpec(shape, lambda b, s: (0,) * len(shape))
    tmaj = lambda w: pl.BlockSpec((None, tm, w), lambda b, s: (b, s, 0))
    fmaj = lambda w: pl.BlockSpec((None, w, tm), lambda b, s: (b, 0, s))
    tshape = lambda w, dt: jax.ShapeDtypeStruct((B, S, w), dt)
    fshape = lambda w, dt: jax.ShapeDtypeStruct((B, w, S), dt)
    return pl.pallas_call(
        _inproj_kernel,
        grid=(B, S // tm),
        in_specs=[
            tmaj(D_MODEL), full((1, D_MODEL)), full(wt.shape), full(wk.shape), full(wg.shape),
            pl.BlockSpec((ROT_HALF, tm), lambda b, s: (0, s)),
            pl.BlockSpec((ROT_HALF, tm), lambda b, s: (0, s)),
            pl.BlockSpec((tm, LANES), lambda b, s: (s, 0)),
            pl.BlockSpec((tm, LANES), lambda b, s: (s, 0)),
            pl.BlockSpec((tm, LANES), lambda b, s: (s, 0)),
        ],
        out_specs=[fmaj(WIDTH_A), fmaj(WIDTH_A), fmaj(WIDTH_B), fmaj(WIDTH_KV_B), fmaj(WIDTH_IDX),
                   fmaj(WI_ROWS), tmaj(WIDTH_A), tmaj(LANES), tmaj(LANES), tmaj(D_MODEL), tmaj(D_MODEL)],
        out_shape=[fshape(WIDTH_A, BF16), fshape(WIDTH_A, BF16), fshape(WIDTH_B, BF16),
                   fshape(WIDTH_KV_B, BF16), fshape(WIDTH_IDX, BF16), fshape(WI_ROWS, F32),
                   tshape(WIDTH_A, BF16), tshape(LANES, BF16), tshape(LANES, BF16),
                   tshape(D_MODEL, F32), tshape(D_MODEL, F32)],
        compiler_params=pltpu.CompilerParams(
            dimension_semantics=("parallel", "parallel"), vmem_limit_bytes=VMEM_LIMIT),
        name="inproj",
    )(x3d, g, wt, wk, wg, cos_t, sin_t, c_tab, sa_tab, sb_tab)


def _padded_rhs(qt_h, slot):
    z = jnp.zeros_like(qt_h)
    return jnp.concatenate([qt_h, z] if slot == 0 else [z, qt_h], axis=0)


def _mixer_a_kernel(qt_ref, k0_ref, k1_ref, k2_ref, v0_ref, v1_ref, v2_ref, ext_ref, o_ref,
                    bias_scr, s_scr):
    jb = pl.program_id(1)
    k_refs = (k0_ref, k1_ref, k2_ref)
    v_refs = (v0_ref, v1_ref, v2_ref)

    @pl.when((pl.program_id(0) == 0) & (jb == 0))
    def _():
        qc = lax.broadcasted_iota(jnp.int32, (QB, QB), 1) // CHUNK
        for m in range(BAND_BLOCKS):
            kc = lax.broadcasted_iota(jnp.int32, (QB, QB), 0) // CHUNK + m * (QB // CHUNK)
            valid = (kc >= qc) & (kc <= qc + N_PREV_CHUNKS)
            for h in range(N_HEADS_A):
                rows = jnp.broadcast_to(ext_ref[h:h + 1, :], (QB, BIAS_SPAN))
                skew = pltpu.roll(rows, m * QB, 1, stride=1, stride_axis=0)
                bias_scr[h, m * QB:(m + 1) * QB, :] = jnp.where(valid, skew[:, 0:QB] * LOG2E, -jnp.inf)

    ones_rows = jnp.ones((ONES_ROWS, QB), BF16)
    for h in range(N_HEADS_A):
        r0 = h * HEAD_DIM
        rhs = _padded_rhs(qt_ref[r0:r0 + HEAD_DIM, :], h % 2)
        lane0 = (h // 2) * LANES
        slot = h % 2
        mx = None
        for m in range(BAND_BLOCKS):
            sm = _dot(k_refs[m][:, lane0:lane0 + LANES], rhs) + bias_scr[h, m * QB:(m + 1) * QB, :]
            if m < BAND_BLOCKS - 1:
                sm = jnp.where(jb + m >= BAND_BLOCKS - 1, sm, -jnp.inf)
            s_scr[slot, m] = sm
            bm = sm.max(axis=0, keepdims=True)
            mx = bm if mx is None else jnp.maximum(mx, bm)
        acc = jnp.zeros((HEAD_DIM + ONES_ROWS, QB), F32)
        for m in range(BAND_BLOCKS):
            p = jnp.exp2(s_scr[slot, m] - mx).astype(BF16)
            v_ext = jnp.concatenate([v_refs[m][r0:r0 + HEAD_DIM, :], ones_rows], axis=0)
            acc = acc + _dot(v_ext, p)
        o_ref[r0:r0 + HEAD_DIM, :] = (acc[0:HEAD_DIM] / acc[HEAD_DIM:HEAD_DIM + 1]).astype(BF16)


def _mixer_a(qat, ka, vat, bias_ext):
    B, _, S = qat.shape
    nb = BAND_BLOCKS - 1
    kspec = lambda m: pl.BlockSpec((None, QB, WIDTH_A), lambda b, j: (b, jnp.maximum(j + m - nb, 0), 0))
    vspec = lambda m: pl.BlockSpec((None, WIDTH_A, QB), lambda b, j: (b, 0, jnp.maximum(j + m - nb, 0)))
    return pl.pallas_call(
        _mixer_a_kernel,
        grid=(B, S // QB),
        in_specs=[pl.BlockSpec((None, WIDTH_A, QB), lambda b, j: (b, 0, j)),
                  kspec(0), kspec(1), kspec(2), vspec(0), vspec(1), vspec(2),
                  pl.BlockSpec(bias_ext.shape, lambda b, j: (0, 0))],
        out_specs=pl.BlockSpec((None, WIDTH_A, QB), lambda b, j: (b, 0, j)),
        out_shape=jax.ShapeDtypeStruct((B, WIDTH_A, S), BF16),
        scratch_shapes=[pltpu.VMEM((N_HEADS_A, BAND_BLOCKS * QB, QB), F32),
                        pltpu.VMEM((2, BAND_BLOCKS, QB, QB), F32)],
        compiler_params=pltpu.CompilerParams(
            dimension_semantics=("arbitrary", "arbitrary"), vmem_limit_bytes=VMEM_LIMIT),
        name="mixer_a",
    )(qat, ka, ka, ka, vat, vat, vat, bias_ext)


def _f32_order_key(x):
    bits = x if x.dtype == jnp.int32 else lax.bitcast_convert_type(x, jnp.int32)
    return bits ^ ((bits >> 31) & jnp.int32(0x7FFFFFFF))


def _mixer_b_kernel(qit_ref, wit_ref, qbt_ref, ki_ref, kb_ref, vbt_ref, o_ref,
                    score_scr, hi_scr, lo_scr, s_scr, rhs_scr, acc_scr, m_scr, l_scr, alpha_scr, *, topk):
    jb = pl.program_id(1)
    n_tiles = jb + 1
    diag0 = pl.multiple_of(jb * KT, KT)
    q_chunk = lax.broadcasted_iota(jnp.int32, (1, QB), 1) // CHUNK
    adm_diag = lax.broadcasted_iota(jnp.int32, (KT, QB), 0) < (q_chunk + 1) * CHUNK
    zero_rows = jnp.zeros((HEAD_DIM, QB), BF16)

    def tile_loop(body, init):
        def wrapped(t, carry):
            return body(pl.multiple_of(t * KT, KT), carry)
        return lax.fori_loop(0, n_tiles, wrapped, init)

    def col_count(hit):
        return hit.reshape(KT // 8, 8, QB).sum(axis=0)

    for h in range(N_IDX_HEADS):
        rhs_scr[h, 0:IDX_DIM, :] = qit_ref[h * IDX_DIM:(h + 1) * IDX_DIM, :]
        rhs_scr[h, IDX_DIM:, :] = zero_rows
    w = wit_ref[...]

    def score_tile(k0, carry):
        ki_t = ki_ref[pl.ds(k0, KT), :]
        acc = jnp.zeros((KT, QB), F32)
        for h in range(N_IDX_HEADS):
            acc = acc + w[h:h + 1, :] * jnp.maximum(_dot(ki_t, rhs_scr[h]), 0.0)
        score_scr[pl.ds(k0, KT), :] = acc
        return carry

    tile_loop(score_tile, 0)
    score_scr[pl.ds(diag0, KT), :] = jnp.where(adm_diag, score_scr[pl.ds(diag0, KT), :], -jnp.inf)

    i16_min = jnp.int32(-2 ** 15)

    def split_tile(k0, carry):
        key = _f32_order_key(score_scr[pl.ds(k0, KT), :])
        hi_scr[pl.ds(k0, KT), :] = (key >> 16).astype(jnp.int16)
        lo_scr[pl.ds(k0, KT), :] = ((key & 0xFFFF) + i16_min).astype(jnp.int16)
        return carry

    tile_loop(split_tile, 0)
    pad0 = pl.multiple_of(n_tiles * KT, KT)
    hi_scr[pl.ds(pad0, KT), :] = jnp.full((KT, QB), -2 ** 15, jnp.int16)
    lo_scr[pl.ds(pad0, KT), :] = jnp.full((KT, QB), -2 ** 15, jnp.int16)
    n_pairs = (n_tiles + 1) // 2

    def pair_loop(body, init):
        def wrapped(t, carry):
            return body(pl.multiple_of(t * 2 * KT, 2 * KT), carry)
        return lax.fori_loop(0, n_pairs, wrapped, init)

    def count16(hit):
        parts = [hit[r:r + 16] for r in range(0, 2 * KT, 16)]
        while len(parts) > 1:
            parts = [a + b for a, b in zip(parts[0::2], parts[1::2])]
        return parts[0]

    def kth_largest16(ref, kk):
        def count_ge(cand):
            def body(k0, cnt):
                hit = jnp.where(ref[pl.ds(k0, 2 * KT), :] >= cand, jnp.int16(1), jnp.int16(0))
                return cnt + count16(hit)
            cnt = pair_loop(body, jnp.zeros((16, QB), jnp.int16))
            return cnt.astype(jnp.int32).sum(axis=0, keepdims=True)

        def bit_step(i, t_u):
            cand_u = t_u | lax.shift_left(jnp.int32(1), 15 - i)
            cnt = count_ge((cand_u + i16_min).astype(jnp.int16))
            return jnp.where(cnt >= kk, cand_u, t_u)

        return lax.fori_loop(0, 16, bit_step, jnp.zeros((1, QB), jnp.int32)) + i16_min

    k_int = jnp.int32(int(topk))
    t_hi = kth_largest16(hi_scr, k_int)
    t_hi16 = t_hi.astype(jnp.int16)

    def narrow(k0, cnt):
        hi = hi_scr[pl.ds(k0, 2 * KT), :]
        lo_scr[pl.ds(k0, 2 * KT), :] = jnp.where(hi == t_hi16, lo_scr[pl.ds(k0, 2 * KT), :],
                                                 jnp.int16(-2 ** 15))
        return cnt + count16(jnp.where(hi > t_hi16, jnp.int16(1), jnp.int16(0)))

    above = pair_loop(narrow, jnp.zeros((16, QB), jnp.int16)).astype(jnp.int32).sum(axis=0, keepdims=True)
    t_lo = kth_largest16(lo_scr, k_int - above)
    thr_key = lax.shift_left(t_hi, 16) | (t_lo - i16_min)
    thr = lax.bitcast_convert_type(_f32_order_key(thr_key), F32)

    def count_both(k0, carry):
        ge, gt = carry
        sc = score_scr[pl.ds(k0, KT), :]
        return (ge + col_count(jnp.where(sc >= thr, 1.0, 0.0)),
                gt + col_count(jnp.where(sc > thr, 1.0, 0.0)))

    ge8, gt8 = tile_loop(count_both, (jnp.zeros((8, QB), F32), jnp.zeros((8, QB), F32)))
    cnt_ge = ge8.sum(axis=0, keepdims=True)
    cnt_gt = gt8.sum(axis=0, keepdims=True)
    n_inadm = (KT - (q_chunk + 1) * CHUNK).astype(F32)
    n_sel = cnt_ge - jnp.where(thr == -jnp.inf, n_inadm, 0.0)
    has_ties = jnp.max(jnp.where(n_sel > topk, 1.0, 0.0)) > 0.0

    @pl.when(jnp.logical_not(has_ties))
    def _():
        def body(k0, carry):
            score_scr[pl.ds(k0, KT), :] = jnp.where(score_scr[pl.ds(k0, KT), :] >= thr, 0.0, -jnp.inf)
            return carry
        tile_loop(body, 0)

    @pl.when(has_ties)
    def _():
        tri = jnp.where(lax.broadcasted_iota(jnp.int32, (KT, KT), 1)
                        < lax.broadcasted_iota(jnp.int32, (KT, KT), 0), 1.0, 0.0).astype(BF16)

        def body(k0, need):
            sc = score_scr[pl.ds(k0, KT), :]
            eq_f = jnp.where(sc == thr, 1.0, 0.0)
            before = _dot(tri, eq_f.astype(BF16))
            take = jnp.where(sc > thr, 1.0, jnp.where(before < need, eq_f, 0.0))
            score_scr[pl.ds(k0, KT), :] = jnp.where(take > 0.0, 0.0, -jnp.inf)
            return need - eq_f.sum(axis=0, keepdims=True)
        tile_loop(body, topk - cnt_gt)

    score_scr[pl.ds(diag0, KT), :] = jnp.where(adm_diag, score_scr[pl.ds(diag0, KT), :], -jnp.inf)

    for h in range(N_HEADS_B):
        g = h // (N_HEADS_B // N_KV_B)
        q_h = qbt_ref[h * HEAD_DIM:(h + 1) * HEAD_DIM, :]
        rhs_scr[h, 0:HEAD_DIM, :] = q_h if g == 0 else zero_rows
        rhs_scr[h, HEAD_DIM:, :] = zero_rows if g == 0 else q_h
    m_scr[...] = jnp.full_like(m_scr, NEG_BIG)
    l_scr[...] = jnp.zeros_like(l_scr)
    acc_scr[...] = jnp.zeros_like(acc_scr)
    ones_rows = jnp.ones((ONES_ROWS, KT), BF16)

    def attn_tile(k0, carry):
        neg = score_scr[pl.ds(k0, KT), :]
        kb_t = kb_ref[pl.ds(k0, KT), :]
        v_ext = [jnp.concatenate([vbt_ref[g * HEAD_DIM:(g + 1) * HEAD_DIM, pl.ds(k0, KT)], ones_rows], axis=0)
                 for g in range(N_KV_B)]
        for h in range(N_HEADS_B):
            s = _dot(kb_t, rhs_scr[h]) + neg
            s_scr[h] = s
            m_old = m_scr[h:h + 1, :]
            m_new = jnp.maximum(m_old, s.max(axis=0, keepdims=True))
            m_scr[h:h + 1, :] = m_new
            alpha_scr[h:h + 1, :] = jnp.exp2(m_old - m_new)
        for h in range(N_HEADS_B):
            g = h // (N_HEADS_B // N_KV_B)
            p = jnp.exp2(s_scr[h] - m_scr[h:h + 1, :]).astype(BF16)
            pv = _dot(v_ext[g], p)
            alpha = alpha_scr[h:h + 1, :]
            r0 = h * HEAD_DIM
            acc_scr[r0:r0 + HEAD_DIM, :] = alpha * acc_scr[r0:r0 + HEAD_DIM, :] + pv[0:HEAD_DIM]
            l_scr[h:h + 1, :] = alpha * l_scr[h:h + 1, :] + pv[HEAD_DIM:HEAD_DIM + 1]
        return carry

    tile_loop(attn_tile, 0)

    for h in range(N_HEADS_B):
        r0 = h * HEAD_DIM
        o_ref[r0:r0 + HEAD_DIM, :] = (acc_scr[r0:r0 + HEAD_DIM, :] / l_scr[h:h + 1, :]).astype(BF16)


def _mixer_b(qit, wit, qbt, ki, kb, vbt):
    B, _, S = qbt.shape
    topk = min(TOPK_MAX, S // 4)
    qspec = lambda w: pl.BlockSpec((None, w, QB), lambda b, j: (b, 0, j))
    return pl.pallas_call(
        functools.partial(_mixer_b_kernel, topk=float(topk)),
        grid=(B, S // QB),
        in_specs=[qspec(WIDTH_IDX), qspec(WI_ROWS), qspec(WIDTH_B),
                  pl.BlockSpec((None, S, LANES), lambda b, j: (b, 0, 0)),
                  pl.BlockSpec((None, S, LANES), lambda b, j: (b, 0, 0)),
                  pl.BlockSpec((None, WIDTH_KV_B, S), lambda b, j: (b, 0, 0))],
        out_specs=qspec(WIDTH_B),
        out_shape=jax.ShapeDtypeStruct((B, WIDTH_B, S), BF16),
        scratch_shapes=[pltpu.VMEM((S, QB), F32),
                        pltpu.VMEM((S + KT, QB), jnp.int16),
                        pltpu.VMEM((S + KT, QB), jnp.int16),
                        pltpu.VMEM((N_HEADS_B, KT, QB), F32),
                        pltpu.VMEM((N_HEADS_B, 2 * HEAD_DIM, QB), BF16),
                        pltpu.VMEM((WIDTH_B, QB), F32),
                        pltpu.VMEM((N_HEADS_B, QB), F32),
                        pltpu.VMEM((N_HEADS_B, QB), F32),
                        pltpu.VMEM((N_HEADS_B, QB), F32)],
        compiler_params=pltpu.CompilerParams(
            dimension_semantics=("parallel", "arbitrary"), vmem_limit_bytes=VMEM_LIMIT),
        name="mixer_b",
    )(qit, wit, qbt, ki, kb, vbt)


def _merge_kernel(x_ref, oat_ref, obt_ref, ga_ref, gb_ref, wa_ref, wb_ref, wo_ref, o_ref):
    ya = _dot_tn(oat_ref[...], wa_ref[...])
    yb = _dot_tn(obt_ref[...], wb_ref[...])
    merged = jax.nn.sigmoid(ga_ref[...]) * ya + jax.nn.sigmoid(gb_ref[...]) * yb
    o_ref[...] = x_ref[...] + _dot(merged.astype(BF16), wo_ref[...])


def _merge(x3d, oat, obt, ga, gb, wa, wb, wo):
    B, S, _ = x3d.shape
    tm = PROJ_TM
    tmaj = pl.BlockSpec((None, tm, D_MODEL), lambda b, s: (b, s, 0))
    fmaj = lambda w: pl.BlockSpec((None, w, tm), lambda b, s: (b, 0, s))
    full = lambda a: pl.BlockSpec(a.shape, lambda b, s: (0, 0))
    return pl.pallas_call(
        _merge_kernel,
        grid=(B, S // tm),
        in_specs=[tmaj, fmaj(WIDTH_A), fmaj(WIDTH_B), tmaj, tmaj, full(wa), full(wb), full(wo)],
        out_specs=tmaj,
        out_shape=jax.ShapeDtypeStruct((B, S, D_MODEL), F32),
        compiler_params=pltpu.CompilerParams(
            dimension_semantics=("parallel", "parallel"), vmem_limit_bytes=VMEM_LIMIT),
        name="merge",
    )(x3d, oat, obt, ga, gb, wa, wb, wo)


def _rope_tables(seq):
    inv_freq = jnp.power(jnp.float32(ROPE_THETA), -jnp.arange(0, ROT_DIM, 2, dtype=F32) / ROT_DIM)
    ang = jnp.arange(seq, dtype=F32)[:, None] * inv_freq[None, :]
    cos, sin = jnp.cos(ang), jnp.sin(ang)
    ones = jnp.ones((seq, HEAD_DIM - ROT_DIM), F32)
    zeros = jnp.zeros((seq, HEAD_DIM - ROT_DIM), F32)
    zh = jnp.zeros((seq, ROT_HALF), F32)
    c = jnp.concatenate([cos, cos, ones], axis=1)
    sa = jnp.concatenate([-sin, zh, zeros], axis=1)
    sb = jnp.concatenate([zh, sin, zeros], axis=1)
    rep = LANES // HEAD_DIM
    return cos.T, sin.T, jnp.tile(c, (1, rep)), jnp.tile(sa, (1, rep)), jnp.tile(sb, (1, rep))


def _band_bias_ext(rel_bias):
    n_keys = BAND_BLOCKS * QB
    n_h = rel_bias.shape[0]
    lo = jnp.broadcast_to(rel_bias[:, :1], (n_h, QB - 1 - REL_CLIP))
    hi = jnp.broadcast_to(rel_bias[:, -1:], (n_h, BIAS_SPAN - (QB - 1 - REL_CLIP) - (2 * REL_CLIP + 1)))
    ext = jnp.concatenate([lo, rel_bias, hi], axis=1)
    return jnp.roll(ext, -(n_keys - 1), axis=1).astype(F32)


def kernel(x, n1_g, ffn1_w_in, ffn1_w_out, n2_g, w_in, rel_bias, w_branch_a, w_branch_b, w_out,
           n3_g, ffn2_w_in, ffn2_w_out, nf_g):
    B, S, D = x.shape
    depth = n1_g.shape[0]
    cos_t, sin_t, c_tab, sa_tab, sb_tab = _rope_tables(S)
    nf = nf_g.reshape(1, D)
    offs = np.cumsum([0, WIDTH_A, WIDTH_A, WIDTH_A, WIDTH_B, WIDTH_KV_B, WIDTH_KV_B,
                      WIDTH_IDX, IDX_DIM, N_IDX_HEADS, D_MODEL, D_MODEL])
    seg = lambda w, i: w[:, offs[i]:offs[i + 1]]

    for l in range(depth):
        w = w_in[l]
        wt = jnp.concatenate(
            [seg(w, 0), seg(w, 2), seg(w, 3), seg(w, 5), seg(w, 6),
             jnp.pad(seg(w, 8), ((0, 0), (0, WI_ROWS - N_IDX_HEADS)))], axis=1).T.astype(BF16)
        wk = jnp.concatenate(
            [seg(w, 1), seg(w, 4), jnp.pad(seg(w, 7), ((0, 0), (0, LANES - IDX_DIM)))], axis=1).astype(BF16)
        wg = jnp.concatenate([seg(w, 9), seg(w, 10)], axis=1).astype(BF16)

        x2d = _ffn(x.reshape(B * S, D), n1_g[l].reshape(1, D), ffn1_w_in[l].astype(BF16),
                   ffn1_w_out[l].astype(BF16), nf, False)
        x = x2d.reshape(B, S, D)
        (qat, vat, qbt, vbt, qit, wit, ka, kb, ki, ga, gb) = _inproj(
            x, n2_g[l].reshape(1, D), wt, wk, wg, cos_t, sin_t, c_tab, sa_tab, sb_tab)
        oat = _mixer_a(qat, ka, vat, _band_bias_ext(rel_bias[l]))
        obt = _mixer_b(qit, wit, qbt, ki, kb, vbt)
        x = _merge(x, oat, obt, ga, gb, w_branch_a[l].astype(BF16), w_branch_b[l].astype(BF16),
                   w_out[l].astype(BF16))
        last = l == depth - 1
        x2d = _ffn(x.reshape(B * S, D), n3_g[l].reshape(1, D), ffn2_w_in[l].astype(BF16),
                   ffn2_w_out[l].astype(BF16), nf, last)
        x = x2d.reshape(B, S, D)
    return x
```

```python
import functools

import jax
import jax.numpy as jnp
import numpy as np
from jax import lax
from jax.experimental import pallas as pl
from jax.experimental.pallas import tpu as pltpu

F32 = jnp.float32
BF16 = jnp.bfloat16

D_MODEL = 1024
D_FF = 2816
HEAD_DIM = 64
CHUNK = 64
N_PREV_CHUNKS = 8
N_HEADS_A = 8
REL_CLIP = 128
N_HEADS_B = 8
N_KV_B = 2
N_IDX_HEADS = 8
IDX_DIM = 64
TOPK_MAX = 256
ROPE_THETA = 500000.0
ROT_DIM = HEAD_DIM // 4
ROT_HALF = ROT_DIM // 2
EPS = 1e-6
WIDTH_A = N_HEADS_A * HEAD_DIM
WIDTH_B = N_HEADS_B * HEAD_DIM
WIDTH_KV_B = N_KV_B * HEAD_DIM
WIDTH_IDX = N_IDX_HEADS * IDX_DIM

LANES = 128
LOG2E = 1.4426950408889634
QK_SCALE = HEAD_DIM ** -0.5 * LOG2E
IDX_SCALE = IDX_DIM ** -0.5

VMEM_LIMIT = 56 * 1024 * 1024

FFN_TM = 512
FFN_TF = D_FF // 2
PROJ_TM = 512
QB = 256
KT = 256
BAND_BLOCKS = N_PREV_CHUNKS * CHUNK // QB + 1
BIAS_SPAN = (BAND_BLOCKS + 1) * QB
WI_ROWS = 16
ONES_ROWS = 16
NEG_BIG = -1e30
FOLD_CHAINS = 8


def _dot(a, b):
    return jnp.dot(a, b, preferred_element_type=F32)


def _dot_nt(a, b):
    return lax.dot_general(a, b, (((1,), (1,)), ((), ())), preferred_element_type=F32)


def _dot_tn(a, b):
    return lax.dot_general(a, b, (((0,), (0,)), ((), ())), preferred_element_type=F32)


def _rmsnorm(x, g):
    ms = jnp.mean(x * x, axis=-1, keepdims=True)
    return x * lax.rsqrt(ms + EPS) * g


def _ffn_kernel(x_ref, g_ref, wg_ref, wu_ref, wo_ref, gf_ref, o_ref, h_scr, acc_scr, *, final_norm):
    j = pl.program_id(1)

    @pl.when(j == 0)
    def _():
        h_scr[...] = _rmsnorm(x_ref[...], g_ref[...]).astype(BF16)
        acc_scr[...] = jnp.zeros_like(acc_scr)

    h = h_scr[...]
    gate = _dot(h, wg_ref[...])
    up = _dot(h, wu_ref[...])
    a = (gate * jax.nn.sigmoid(gate) * up).astype(BF16)
    acc_scr[...] += _dot(a, wo_ref[...])

    @pl.when(j == pl.num_programs(1) - 1)
    def _():
        y = x_ref[...] + 0.5 * acc_scr[...]
        if final_norm:
            y = _rmsnorm(y, gf_ref[...])
        o_ref[...] = y


def _ffn(x2d, g, w_in_bf, w_out_bf, gf, final_norm):
    T = x2d.shape[0]
    nf = D_FF // FFN_TF
    return pl.pallas_call(
        functools.partial(_ffn_kernel, final_norm=final_norm),
        grid=(T // FFN_TM, nf),
        in_specs=[
            pl.BlockSpec((FFN_TM, D_MODEL), lambda i, j: (i, 0)),
            pl.BlockSpec((1, D_MODEL), lambda i, j: (0, 0)),
            pl.BlockSpec((D_MODEL, FFN_TF), lambda i, j: (0, j)),
            pl.BlockSpec((D_MODEL, FFN_TF), lambda i, j: (0, j + nf)),
            pl.BlockSpec((FFN_TF, D_MODEL), lambda i, j: (j, 0)),
            pl.BlockSpec((1, D_MODEL), lambda i, j: (0, 0)),
        ],
        out_specs=pl.BlockSpec((FFN_TM, D_MODEL), lambda i, j: (i, 0)),
        out_shape=jax.ShapeDtypeStruct((T, D_MODEL), F32),
        scratch_shapes=[pltpu.VMEM((FFN_TM, D_MODEL), BF16), pltpu.VMEM((FFN_TM, D_MODEL), F32)],
        compiler_params=pltpu.CompilerParams(
            dimension_semantics=("parallel", "arbitrary"), vmem_limit_bytes=VMEM_LIMIT),
        name="ffn_final" if final_norm else "ffn",
    )(x2d, g, w_in_bf, w_in_bf, w_out_bf, gf)


def _rope_rows(x, cos, sin, n_heads):
    pieces = []
    for h in range(n_heads):
        r0 = h * HEAD_DIM
        t1 = x[r0:r0 + ROT_HALF]
        t2 = x[r0 + ROT_HALF:r0 + ROT_DIM]
        pieces.append(t1 * cos - t2 * sin)
        pieces.append(t2 * cos + t1 * sin)
        pieces.append(x[r0 + ROT_DIM:r0 + HEAD_DIM])
    return jnp.concatenate(pieces, axis=0)


def _rope_lanes(x, c, sa, sb):
    return x * c + pltpu.roll(x, LANES - ROT_HALF, 1) * sa + pltpu.roll(x, ROT_HALF, 1) * sb


def _inproj_kernel(x_ref, g_ref, wt_ref, wk_ref, wg_ref, cos_ref, sin_ref, c_ref, sa_ref, sb_ref,
                   qat_ref, vat_ref, qbt_ref, vbt_ref, qit_ref, wit_ref,
                   ka_ref, kb_ref, ki_ref, ga_ref, gb_ref):
    h = _rmsnorm(x_ref[...], g_ref[...]).astype(BF16)
    cos = cos_ref[...]
    sin = sin_ref[...]

    r = 0
    t = _dot_nt(wt_ref[r:r + WIDTH_A, :], h)
    qat_ref[...] = (t * QK_SCALE).astype(BF16)
    r += WIDTH_A
    vat_ref[...] = _dot_nt(wt_ref[r:r + WIDTH_A, :], h).astype(BF16)
    r += WIDTH_A
    t = _dot_nt(wt_ref[r:r + WIDTH_B, :], h)
    qbt_ref[...] = (_rope_rows(t, cos, sin, N_HEADS_B) * QK_SCALE).astype(BF16)
    r += WIDTH_B
    vbt_ref[...] = _dot_nt(wt_ref[r:r + WIDTH_KV_B, :], h).astype(BF16)
    r += WIDTH_KV_B
    t = _dot_nt(wt_ref[r:r + WIDTH_IDX, :], h)
    qit_ref[...] = (_rope_rows(t, cos, sin, N_IDX_HEADS) * IDX_SCALE).astype(BF16)
    r += WIDTH_IDX
    wit_ref[...] = _dot_nt(wt_ref[r:r + WI_ROWS, :], h) * (N_IDX_HEADS ** -0.5)

    ka_ref[...] = _dot(h, wk_ref[:, 0:WIDTH_A]).astype(BF16)
    c, sa, sb = c_ref[...], sa_ref[...], sb_ref[...]
    t = _dot(h, wk_ref[:, WIDTH_A:WIDTH_A + LANES])
    kb_ref[...] = _rope_lanes(t, c, sa, sb).astype(BF16)
    t = _dot(h, wk_ref[:, WIDTH_A + LANES:WIDTH_A + 2 * LANES])
    ki_ref[...] = _rope_lanes(t, c, sa, sb).astype(BF16)

    ga_ref[...] = _dot(h, wg_ref[:, 0:D_MODEL])
    gb_ref[...] = _dot(h, wg_ref[:, D_MODEL:2 * D_MODEL])


def _inproj(x3d, g, wt, wk, wg, cos_t, sin_t, c_tab, sa_tab, sb_tab):
    B, S, _ = x3d.shape
    tm = PROJ_TM
    full = lambda shape: pl.BlockSpec(shape, lambda b, s: (0,) * len(shape))
    tmaj = lambda w: pl.BlockSpec((None, tm, w), lambda b, s: (b, s, 0))
    fmaj = lambda w: pl.BlockSpec((None, w, tm), lambda b, s: (b, 0, s))
    tshape = lambda w, dt: jax.ShapeDtypeStruct((B, S, w), dt)
    fshape = lambda w, dt: jax.ShapeDtypeStruct((B, w, S), dt)
    return pl.pallas_call(
        _inproj_kernel,
        grid=(B, S // tm),
        in_specs=[
            tmaj(D_MODEL), full((1, D_MODEL)), full(wt.shape), full(wk.shape), full(wg.shape),
            pl.BlockSpec((ROT_HALF, tm), lambda b, s: (0, s)),
            pl.BlockSpec((ROT_HALF, tm), lambda b, s: (0, s)),
            pl.BlockSpec((tm, LANES), lambda b, s: (s, 0)),
            pl.BlockSpec((tm, LANES), lambda b, s: (s, 0)),
            pl.BlockSpec((tm, LANES), lambda b, s: (s, 0)),
        ],
        out_specs=[fmaj(WIDTH_A), fmaj(WIDTH_A), fmaj(WIDTH_B), fmaj(WIDTH_KV_B), fmaj(WIDTH_IDX),
                   fmaj(WI_ROWS), tmaj(WIDTH_A), tmaj(LANES), tmaj(LANES), tmaj(D_MODEL), tmaj(D_MODEL)],
        out_shape=[fshape(WIDTH_A, BF16), fshape(WIDTH_A, BF16), fshape(WIDTH_B, BF16),
                   fshape(WIDTH_KV_B, BF16), fshape(WIDTH_IDX, BF16), fshape(WI_ROWS, F32),
                   tshape(WIDTH_A, BF16), tshape(LANES, BF16), tshape(LANES, BF16),
                   tshape(D_MODEL, F32), tshape(D_MODEL, F32)],
        compiler_params=pltpu.CompilerParams(
            dimension_semantics=("parallel", "parallel"), vmem_limit_bytes=VMEM_LIMIT),
        name="inproj",
    )(x3d, g, wt, wk, wg, cos_t, sin_t, c_tab, sa_tab, sb_tab)


def _padded_rhs(qt_h, slot):
    z = jnp.zeros_like(qt_h)
    return jnp.concatenate([qt_h, z] if slot == 0 else [z, qt_h], axis=0)


def _mixer_a_kernel(qt_ref, k0_ref, k1_ref, k2_ref, v0_ref, v1_ref, v2_ref, ext_ref, o_ref,
                    bias_scr, s_scr):
    jb = pl.program_id(1)
    k_refs = (k0_ref, k1_ref, k2_ref)
    v_refs = (v0_ref, v1_ref, v2_ref)

    @pl.when((pl.program_id(0) == 0) & (jb == 0))
    def _():
        qc = lax.broadcasted_iota(jnp.int32, (QB, QB), 1) // CHUNK
        for m in range(BAND_BLOCKS):
            kc = lax.broadcasted_iota(jnp.int32, (QB, QB), 0) // CHUNK + m * (QB // CHUNK)
            valid = (kc >= qc) & (kc <= qc + N_PREV_CHUNKS)
            for h in range(N_HEADS_A):
                rows = jnp.broadcast_to(ext_ref[h:h + 1, :], (QB, BIAS_SPAN))
                skew = pltpu.roll(rows, m * QB, 1, stride=1, stride_axis=0)
                bias_scr[h, m * QB:(m + 1) * QB, :] = jnp.where(valid, skew[:, 0:QB] * LOG2E, -jnp.inf)

    ones_rows = jnp.ones((ONES_ROWS, QB), BF16)
    for h in range(N_HEADS_A):
        r0 = h * HEAD_DIM
        rhs = _padded_rhs(qt_ref[r0:r0 + HEAD_DIM, :], h % 2)
        lane0 = (h // 2) * LANES
        slot = h % 2
        mx = None
        for m in range(BAND_BLOCKS):
            sm = _dot(k_refs[m][:, lane0:lane0 + LANES], rhs) + bias_scr[h, m * QB:(m + 1) * QB, :]
            if m < BAND_BLOCKS - 1:
                sm = jnp.where(jb + m >= BAND_BLOCKS - 1, sm, -jnp.inf)
            s_scr[slot, m] = sm
            bm = sm.max(axis=0, keepdims=True)
            mx = bm if mx is None else jnp.maximum(mx, bm)
        acc = jnp.zeros((HEAD_DIM + ONES_ROWS, QB), F32)
        for m in range(BAND_BLOCKS):
            p = jnp.exp2((s_scr[slot, m] - mx).astype(BF16))
            v_ext = jnp.concatenate([v_refs[m][r0:r0 + HEAD_DIM, :], ones_rows], axis=0)
            acc = acc + _dot(v_ext, p)
        o_ref[r0:r0 + HEAD_DIM, :] = (acc[0:HEAD_DIM] / acc[HEAD_DIM:HEAD_DIM + 1]).astype(BF16)


def _mixer_a(qat, ka, vat, bias_ext):
    B, _, S = qat.shape
    nb = BAND_BLOCKS - 1
    kspec = lambda m: pl.BlockSpec((None, QB, WIDTH_A), lambda b, j: (b, jnp.maximum(j + m - nb, 0), 0))
    vspec = lambda m: pl.BlockSpec((None, WIDTH_A, QB), lambda b, j: (b, 0, jnp.maximum(j + m - nb, 0)))
    return pl.pallas_call(
        _mixer_a_kernel,
        grid=(B, S // QB),
        in_specs=[pl.BlockSpec((None, WIDTH_A, QB), lambda b, j: (b, 0, j)),
                  kspec(0), kspec(1), kspec(2), vspec(0), vspec(1), vspec(2),
                  pl.BlockSpec(bias_ext.shape, lambda b, j: (0, 0))],
        out_specs=pl.BlockSpec((None, WIDTH_A, QB), lambda b, j: (b, 0, j)),
        out_shape=jax.ShapeDtypeStruct((B, WIDTH_A, S), BF16),
        scratch_shapes=[pltpu.VMEM((N_HEADS_A, BAND_BLOCKS * QB, QB), F32),
                        pltpu.VMEM((2, BAND_BLOCKS, QB, QB), F32)],
        compiler_params=pltpu.CompilerParams(
            dimension_semantics=("arbitrary", "arbitrary"), vmem_limit_bytes=VMEM_LIMIT),
        name="mixer_a",
    )(qat, ka, ka, ka, vat, vat, vat, bias_ext)


def _ordered_code_to_f32(u):
    bits = jnp.where(u < 0, u ^ jnp.int32(-2 ** 31), ~u)
    return lax.bitcast_convert_type(bits, F32)


def _mixer_b_kernel(qit_ref, wit_ref, qbt_ref, ki_ref, kb_ref, vbt_ref, o_ref,
                    score_scr, sb_scr, s_scr, s2_scr, tmax_scr, tmax2_scr, rhs_scr, acc_scr, m_scr, l_scr,
                    *, topk):
    jb = pl.program_id(1)
    n_tiles = jb + 1
    diag0 = pl.multiple_of(jb * KT, KT)
    q_chunk = lax.broadcasted_iota(jnp.int32, (1, QB), 1) // CHUNK
    adm_diag = lax.broadcasted_iota(jnp.int32, (KT, QB), 0) < (q_chunk + 1) * CHUNK
    zero_rows = jnp.zeros((HEAD_DIM, QB), BF16)

    def tile_loop(body, init):
        def wrapped(t, carry):
            return body(pl.multiple_of(t * KT, KT), carry)
        return lax.fori_loop(0, n_tiles, wrapped, init)

    def col_count(hit):
        return hit.reshape(KT // 8, 8, QB).sum(axis=0)

    for h in range(N_IDX_HEADS):
        rhs_scr[h, 0:IDX_DIM, :] = qit_ref[h * IDX_DIM:(h + 1) * IDX_DIM, :]
        rhs_scr[h, IDX_DIM:, :] = zero_rows
    w = wit_ref[...]

    def score_tile(k0, carry):
        ki_t = ki_ref[pl.ds(k0, KT), :]
        acc = jnp.zeros((KT, QB), F32)
        for h in range(N_IDX_HEADS):
            acc = acc + w[h:h + 1, :] * jnp.maximum(_dot(ki_t, rhs_scr[h]), 0.0)
        score_scr[pl.ds(k0, KT), :] = acc
        return carry

    tile_loop(score_tile, 0)
    score_scr[pl.ds(diag0, KT), :] = jnp.where(adm_diag, score_scr[pl.ds(diag0, KT), :], -jnp.inf)

    pad0 = pl.multiple_of(n_tiles * KT, KT)
    score_scr[pl.ds(pad0, KT), :] = jnp.full((KT, QB), -jnp.inf, F32)
    n_pairs = (n_tiles + 1) // 2

    def pair_loop(body, init):
        def wrapped(t, carry):
            return body(pl.multiple_of(t * 2 * KT, 2 * KT), carry)
        return lax.fori_loop(0, n_pairs, wrapped, init)

    def round_pair(k0, carry):
        sb_scr[pl.ds(k0, 2 * KT), :] = score_scr[pl.ds(k0, 2 * KT), :].astype(BF16)
        return carry

    pair_loop(round_pair, 0)

    def fold_rows(hit, rows):
        parts = [hit[r:r + rows] for r in range(0, 2 * KT, rows)]
        chains = parts[:FOLD_CHAINS]
        for i, p in enumerate(parts[FOLD_CHAINS:]):
            chains[i % FOLD_CHAINS] = chains[i % FOLD_CHAINS] + p
        while len(chains) > 1:
            chains = [a + b for a, b in zip(chains[0::2], chains[1::2])]
        return chains[0]

    def count_ge_bf16(cand):
        cand = cand.astype(BF16)

        def body(k0, cnt):
            hit = jnp.where(sb_scr[pl.ds(k0, 2 * KT), :] >= cand, jnp.int16(1), jnp.int16(0))
            return cnt + fold_rows(hit, 16)
        cnt = pair_loop(body, jnp.zeros((16, QB), jnp.int16))
        return cnt.astype(jnp.int32).sum(axis=0, keepdims=True)

    def count_ge_f32(cand):
        def body(k0, cnt):
            hit = jnp.where(score_scr[pl.ds(k0, 2 * KT), :] >= cand, 1.0, 0.0)
            return cnt + fold_rows(hit, 8)
        cnt = pair_loop(body, jnp.zeros((8, QB), F32))
        return cnt.sum(axis=0, keepdims=True).astype(jnp.int32)

    k_int = jnp.int32(int(topk))
    neg_inf_code = jnp.int32(0x007FFFFF)

    def accept(cnt, code):
        return (cnt >= k_int) | ((code >= 0) & (code <= neg_inf_code))

    def coarse_step(i, t_u):
        cand_u = t_u | lax.shift_left(jnp.int32(1), 31 - i)
        cnt = count_ge_bf16(_ordered_code_to_f32(cand_u))
        return jnp.where(accept(cnt, cand_u), cand_u, t_u)

    t_coarse = lax.fori_loop(0, 16, coarse_step, jnp.zeros((1, QB), jnp.int32))
    base = t_coarse - jnp.int32(1 << 16)

    def fine_step(i, off):
        cand_off = off | lax.shift_left(jnp.int32(1), 16 - i)
        cand_u = base + cand_off
        cnt = count_ge_f32(_ordered_code_to_f32(cand_u))
        return jnp.where(accept(cnt, cand_u), cand_off, off)

    off = lax.fori_loop(0, 17, fine_step, jnp.zeros((1, QB), jnp.int32))
    thr = _ordered_code_to_f32(base + off)

    def count_both(k0, carry):
        ge, gt = carry
        sc = score_scr[pl.ds(k0, KT), :]
        return (ge + col_count(jnp.where(sc >= thr, 1.0, 0.0)),
                gt + col_count(jnp.where(sc > thr, 1.0, 0.0)))

    ge8, gt8 = tile_loop(count_both, (jnp.zeros((8, QB), F32), jnp.zeros((8, QB), F32)))
    cnt_ge = ge8.sum(axis=0, keepdims=True)
    cnt_gt = gt8.sum(axis=0, keepdims=True)
    n_inadm = (KT - (q_chunk + 1) * CHUNK).astype(F32)
    n_sel = cnt_ge - jnp.where(thr == -jnp.inf, n_inadm, 0.0)
    has_ties = jnp.max(jnp.where(n_sel > topk, 1.0, 0.0)) > 0.0

    @pl.when(jnp.logical_not(has_ties))
    def _():
        def body(k0, carry):
            score_scr[pl.ds(k0, KT), :] = jnp.where(score_scr[pl.ds(k0, KT), :] >= thr, 0.0, -jnp.inf)
            return carry
        tile_loop(body, 0)

    @pl.when(has_ties)
    def _():
        tri = jnp.where(lax.broadcasted_iota(jnp.int32, (KT, KT), 1)
                        < lax.broadcasted_iota(jnp.int32, (KT, KT), 0), 1.0, 0.0).astype(BF16)

        def body(k0, need):
            sc = score_scr[pl.ds(k0, KT), :]
            eq_f = jnp.where(sc == thr, 1.0, 0.0)
            before = _dot(tri, eq_f.astype(BF16))
            take = jnp.where(sc > thr, 1.0, jnp.where(before < need, eq_f, 0.0))
            score_scr[pl.ds(k0, KT), :] = jnp.where(take > 0.0, 0.0, -jnp.inf)
            return need - eq_f.sum(axis=0, keepdims=True)
        tile_loop(body, topk - cnt_gt)

    score_scr[pl.ds(diag0, KT), :] = jnp.where(adm_diag, score_scr[pl.ds(diag0, KT), :], -jnp.inf)

    for h in range(N_HEADS_B):
        g = h // (N_HEADS_B // N_KV_B)
        q_h = qbt_ref[h * HEAD_DIM:(h + 1) * HEAD_DIM, :]
        rhs_scr[h, 0:HEAD_DIM, :] = q_h if g == 0 else zero_rows
        rhs_scr[h, HEAD_DIM:, :] = zero_rows if g == 0 else q_h
    m_scr[...] = jnp.full_like(m_scr, NEG_BIG)
    l_scr[...] = jnp.zeros_like(l_scr)
    acc_scr[...] = jnp.zeros_like(acc_scr)
    ones_rows = jnp.ones((ONES_ROWS, KT), BF16)

    def logits_tile(t, s_buf, tmax_buf):
        k0 = pl.multiple_of(jnp.minimum(t, 2 * n_pairs - 1) * KT, KT)
        neg = score_scr[pl.ds(k0, KT), :]
        kb_t = kb_ref[pl.ds(k0, KT), :]
        for h in range(N_HEADS_B):
            s = _dot(kb_t, rhs_scr[h]) + neg
            s_buf[h] = s
            tmax_buf[h:h + 1, :] = s.max(axis=0, keepdims=True)

    def softmax_pv_tile(t, s_buf, tmax_buf):
        k0 = pl.multiple_of(t * KT, KT)
        v_ext = [jnp.concatenate([vbt_ref[g * HEAD_DIM:(g + 1) * HEAD_DIM, pl.ds(k0, KT)], ones_rows], axis=0)
                 for g in range(N_KV_B)]
        for h in range(N_HEADS_B):
            g = h // (N_HEADS_B // N_KV_B)
            m_old = m_scr[h:h + 1, :]
            m_new = jnp.maximum(m_old, tmax_buf[h:h + 1, :])
            m_scr[h:h + 1, :] = m_new
            alpha = jnp.exp2(m_old - m_new)
            p = jnp.exp2((s_buf[h] - m_new).astype(BF16))
            pv = _dot(v_ext[g], p)
            r0 = h * HEAD_DIM
            acc_scr[r0:r0 + HEAD_DIM, :] = alpha * acc_scr[r0:r0 + HEAD_DIM, :] + pv[0:HEAD_DIM]
            l_scr[h:h + 1, :] = alpha * l_scr[h:h + 1, :] + pv[HEAD_DIM:HEAD_DIM + 1]

    logits_tile(0, s_scr, tmax_scr)

    def attn_pair(i, carry):
        t = 2 * i
        logits_tile(t + 1, s2_scr, tmax2_scr)
        softmax_pv_tile(t, s_scr, tmax_scr)
        logits_tile(t + 2, s_scr, tmax_scr)
        softmax_pv_tile(t + 1, s2_scr, tmax2_scr)
        return carry

    lax.fori_loop(0, n_pairs, attn_pair, 0)

    for h in range(N_HEADS_B):
        r0 = h * HEAD_DIM
        o_ref[r0:r0 + HEAD_DIM, :] = (acc_scr[r0:r0 + HEAD_DIM, :] / l_scr[h:h + 1, :]).astype(BF16)


def _mixer_b(qit, wit, qbt, ki, kb, vbt):
    B, _, S = qbt.shape
    topk = min(TOPK_MAX, S // 4)
    qspec = lambda w: pl.BlockSpec((None, w, QB), lambda b, j: (b, 0, j))
    return pl.pallas_call(
        functools.partial(_mixer_b_kernel, topk=float(topk)),
        grid=(B, S // QB),
        in_specs=[qspec(WIDTH_IDX), qspec(WI_ROWS), qspec(WIDTH_B),
                  pl.BlockSpec((None, S, LANES), lambda b, j: (b, 0, 0)),
                  pl.BlockSpec((None, S, LANES), lambda b, j: (b, 0, 0)),
                  pl.BlockSpec((None, WIDTH_KV_B, S), lambda b, j: (b, 0, 0))],
        out_specs=qspec(WIDTH_B),
        out_shape=jax.ShapeDtypeStruct((B, WIDTH_B, S), BF16),
        scratch_shapes=[pltpu.VMEM((S + KT, QB), F32),
                        pltpu.VMEM((S + KT, QB), BF16),
                        pltpu.VMEM((N_HEADS_B, KT, QB), F32),
                        pltpu.VMEM((N_HEADS_B, KT, QB), F32),
                        pltpu.VMEM((N_HEADS_B, QB), F32),
                        pltpu.VMEM((N_HEADS_B, QB), F32),
                        pltpu.VMEM((N_HEADS_B, 2 * HEAD_DIM, QB), BF16),
                        pltpu.VMEM((WIDTH_B, QB), F32),
                        pltpu.VMEM((N_HEADS_B, QB), F32),
                        pltpu.VMEM((N_HEADS_B, QB), F32)],
        compiler_params=pltpu.CompilerParams(
            dimension_semantics=("parallel", "arbitrary"), vmem_limit_bytes=VMEM_LIMIT),
        name="mixer_b",
    )(qit, wit, qbt, ki, kb, vbt)


def _merge_kernel(x_ref, oat_ref, obt_ref, ga_ref, gb_ref, wa_ref, wb_ref, wo_ref, o_ref):
    ya = _dot_tn(oat_ref[...], wa_ref[...])
    yb = _dot_tn(obt_ref[...], wb_ref[...])
    merged = jax.nn.sigmoid(ga_ref[...]) * ya + jax.nn.sigmoid(gb_ref[...]) * yb
    o_ref[...] = x_ref[...] + _dot(merged.astype(BF16), wo_ref[...])


def _merge(x3d, oat, obt, ga, gb, wa, wb, wo):
    B, S, _ = x3d.shape
    tm = PROJ_TM
    tmaj = pl.BlockSpec((None, tm, D_MODEL), lambda b, s: (b, s, 0))
    fmaj = lambda w: pl.BlockSpec((None, w, tm), lambda b, s: (b, 0, s))
    full = lambda a: pl.BlockSpec(a.shape, lambda b, s: (0, 0))
    return pl.pallas_call(
        _merge_kernel,
        grid=(B, S // tm),
        in_specs=[tmaj, fmaj(WIDTH_A), fmaj(WIDTH_B), tmaj, tmaj, full(wa), full(wb), full(wo)],
        out_specs=tmaj,
        out_shape=jax.ShapeDtypeStruct((B, S, D_MODEL), F32),
        compiler_params=pltpu.CompilerParams(
            dimension_semantics=("parallel", "parallel"), vmem_limit_bytes=VMEM_LIMIT),
        name="merge",
    )(x3d, oat, obt, ga, gb, wa, wb, wo)


def _rope_tables(seq):
    inv_freq = jnp.power(jnp.float32(ROPE_THETA), -jnp.arange(0, ROT_DIM, 2, dtype=F32) / ROT_DIM)
    ang = jnp.arange(seq, dtype=F32)[:, None] * inv_freq[None, :]
    cos, sin = jnp.cos(ang), jnp.sin(ang)
    ones = jnp.ones((seq, HEAD_DIM - ROT_DIM), F32)
    zeros = jnp.zeros((seq, HEAD_DIM - ROT_DIM), F32)
    zh = jnp.zeros((seq, ROT_HALF), F32)
    c = jnp.concatenate([cos, cos, ones], axis=1)
    sa = jnp.concatenate([-sin, zh, zeros], axis=1)
    sb = jnp.concatenate([zh, sin, zeros], axis=1)
    rep = LANES // HEAD_DIM
    return cos.T, sin.T, jnp.tile(c, (1, rep)), jnp.tile(sa, (1, rep)), jnp.tile(sb, (1, rep))


def _band_bias_ext(rel_bias):
    n_keys = BAND_BLOCKS * QB
    n_h = rel_bias.shape[0]
    lo = jnp.broadcast_to(rel_bias[:, :1], (n_h, QB - 1 - REL_CLIP))
    hi = jnp.broadcast_to(rel_bias[:, -1:], (n_h, BIAS_SPAN - (QB - 1 - REL_CLIP) - (2 * REL_CLIP + 1)))
    ext = jnp.concatenate([lo, rel_bias, hi], axis=1)
    return jnp.roll(ext, -(n_keys - 1), axis=1).astype(F32)


def kernel(x, n1_g, ffn1_w_in, ffn1_w_out, n2_g, w_in, rel_bias, w_branch_a, w_branch_b, w_out,
           n3_g, ffn2_w_in, ffn2_w_out, nf_g):
    B, S, D = x.shape
    depth = n1_g.shape[0]
    cos_t, sin_t, c_tab, sa_tab, sb_tab = _rope_tables(S)
    nf = nf_g.reshape(1, D)
    offs = np.cumsum([0, WIDTH_A, WIDTH_A, WIDTH_A, WIDTH_B, WIDTH_KV_B, WIDTH_KV_B,
                      WIDTH_IDX, IDX_DIM, N_IDX_HEADS, D_MODEL, D_MODEL])
    seg = lambda w, i: w[:, offs[i]:offs[i + 1]]

    for l in range(depth):
        w = w_in[l]
        wt = jnp.concatenate(
            [seg(w, 0), seg(w, 2), seg(w, 3), seg(w, 5), seg(w, 6),
             jnp.pad(seg(w, 8), ((0, 0), (0, WI_ROWS - N_IDX_HEADS)))], axis=1).T.astype(BF16)
        wk = jnp.concatenate(
            [seg(w, 1), seg(w, 4), jnp.pad(seg(w, 7), ((0, 0), (0, LANES - IDX_DIM)))], axis=1).astype(BF16)
        wg = jnp.concatenate([seg(w, 9), seg(w, 10)], axis=1).astype(BF16)

        x2d = _ffn(x.reshape(B * S, D), n1_g[l].reshape(1, D), ffn1_w_in[l].astype(BF16),
                   ffn1_w_out[l].astype(BF16), nf, False)
        x = x2d.reshape(B, S, D)
        (qat, vat, qbt, vbt, qit, wit, ka, kb, ki, ga, gb) = _inproj(
            x, n2_g[l].reshape(1, D), wt, wk, wg, cos_t, sin_t, c_tab, sa_tab, sb_tab)
        oat = _mixer_a(qat, ka, vat, _band_bias_ext(rel_bias[l]))
        obt = _mixer_b(qit, wit, qbt, ki, kb, vbt)
        x = _merge(x, oat, obt, ga, gb, w_branch_a[l].astype(BF16), w_branch_b[l].astype(BF16),
                   w_out[l].astype(BF16))
        last = l == depth - 1
        x2d = _ffn(x.reshape(B * S, D), n3_g[l].reshape(1, D), ffn2_w_in[l].astype(BF16),
                   ffn2_w_out[l].astype(BF16), nf, last)
        x = x2d.reshape(B, S, D)
    return x
```

```python
import functools

import jax
import jax.numpy as jnp
import numpy as np
from jax import lax
from jax.experimental import pallas as pl
from jax.experimental.pallas import tpu as pltpu

F32 = jnp.float32
BF16 = jnp.bfloat16

D_MODEL = 1024
D_FF = 2816
HEAD_DIM = 64
CHUNK = 64
N_PREV_CHUNKS = 8
N_HEADS_A = 8
REL_CLIP = 128
N_HEADS_B = 8
N_KV_B = 2
N_IDX_HEADS = 8
IDX_DIM = 64
TOPK_MAX = 256
ROPE_THETA = 500000.0
ROT_DIM = HEAD_DIM // 4
ROT_HALF = ROT_DIM // 2
EPS = 1e-6
WIDTH_A = N_HEADS_A * HEAD_DIM
WIDTH_B = N_HEADS_B * HEAD_DIM
WIDTH_KV_B = N_KV_B * HEAD_DIM
WIDTH_IDX = N_IDX_HEADS * IDX_DIM

LANES = 128
LOG2E = 1.4426950408889634
QK_SCALE = HEAD_DIM ** -0.5 * LOG2E
IDX_SCALE = IDX_DIM ** -0.5

VMEM_LIMIT = 56 * 1024 * 1024

FFN_TM = 512
FFN_TF = D_FF // 2
PROJ_TM = 512
QB = 256
KT = 256
BAND_BLOCKS = N_PREV_CHUNKS * CHUNK // QB + 1
BIAS_SPAN = (BAND_BLOCKS + 1) * QB
WI_ROWS = 16
ONES_ROWS = 16
NEG_BIG = -1e30
FOLD_CHAINS = 8


def _dot(a, b):
    return jnp.dot(a, b, preferred_element_type=F32)


def _dot_nt(a, b):
    return lax.dot_general(a, b, (((1,), (1,)), ((), ())), preferred_element_type=F32)


def _dot_tn(a, b):
    return lax.dot_general(a, b, (((0,), (0,)), ((), ())), preferred_element_type=F32)


def _rmsnorm(x, g):
    ms = jnp.mean(x * x, axis=-1, keepdims=True)
    return x * lax.rsqrt(ms + EPS) * g


def _ffn_kernel(x_ref, g_ref, wg_ref, wu_ref, wo_ref, gf_ref, o_ref, h_scr, acc_scr, *, final_norm):
    j = pl.program_id(1)

    @pl.when(j == 0)
    def _():
        h_scr[...] = _rmsnorm(x_ref[...], g_ref[...]).astype(BF16)
        acc_scr[...] = jnp.zeros_like(acc_scr)

    h = h_scr[...]
    gate = _dot(h, wg_ref[...])
    up = _dot(h, wu_ref[...])
    a = (gate * jax.nn.sigmoid(gate) * up).astype(BF16)
    acc_scr[...] += _dot(a, wo_ref[...])

    @pl.when(j == pl.num_programs(1) - 1)
    def _():
        y = x_ref[...] + 0.5 * acc_scr[...]
        if final_norm:
            y = _rmsnorm(y, gf_ref[...])
        o_ref[...] = y


def _ffn(x2d, g, w_in_bf, w_out_bf, gf, final_norm):
    T = x2d.shape[0]
    nf = D_FF // FFN_TF
    return pl.pallas_call(
        functools.partial(_ffn_kernel, final_norm=final_norm),
        grid=(T // FFN_TM, nf),
        in_specs=[
            pl.BlockSpec((FFN_TM, D_MODEL), lambda i, j: (i, 0)),
            pl.BlockSpec((1, D_MODEL), lambda i, j: (0, 0)),
            pl.BlockSpec((D_MODEL, FFN_TF), lambda i, j: (0, j)),
            pl.BlockSpec((D_MODEL, FFN_TF), lambda i, j: (0, j + nf)),
            pl.BlockSpec((FFN_TF, D_MODEL), lambda i, j: (j, 0)),
            pl.BlockSpec((1, D_MODEL), lambda i, j: (0, 0)),
        ],
        out_specs=pl.BlockSpec((FFN_TM, D_MODEL), lambda i, j: (i, 0)),
        out_shape=jax.ShapeDtypeStruct((T, D_MODEL), F32),
        scratch_shapes=[pltpu.VMEM((FFN_TM, D_MODEL), BF16), pltpu.VMEM((FFN_TM, D_MODEL), F32)],
        compiler_params=pltpu.CompilerParams(
            dimension_semantics=("parallel", "arbitrary"), vmem_limit_bytes=VMEM_LIMIT),
        name="ffn_final" if final_norm else "ffn",
    )(x2d, g, w_in_bf, w_in_bf, w_out_bf, gf)


def _rope_rows(x, cos, sin, n_heads):
    pieces = []
    for h in range(n_heads):
        r0 = h * HEAD_DIM
        t1 = x[r0:r0 + ROT_HALF]
        t2 = x[r0 + ROT_HALF:r0 + ROT_DIM]
        pieces.append(t1 * cos - t2 * sin)
        pieces.append(t2 * cos + t1 * sin)
        pieces.append(x[r0 + ROT_DIM:r0 + HEAD_DIM])
    return jnp.concatenate(pieces, axis=0)


def _rope_lanes(x, c, sa, sb):
    return x * c + pltpu.roll(x, LANES - ROT_HALF, 1) * sa + pltpu.roll(x, ROT_HALF, 1) * sb


def _inproj_kernel(x_ref, g_ref, wt_ref, wk_ref, wg_ref, cos_ref, sin_ref, c_ref, sa_ref, sb_ref,
                   qat_ref, vat_ref, qbt_ref, vbt_ref, qit_ref, wit_ref,
                   ka_ref, kb_ref, ki_ref, ga_ref, gb_ref):
    h = _rmsnorm(x_ref[...], g_ref[...]).astype(BF16)
    cos = cos_ref[...]
    sin = sin_ref[...]

    r = 0
    t = _dot_nt(wt_ref[r:r + WIDTH_A, :], h)
    qat_ref[...] = (t * QK_SCALE).astype(BF16)
    r += WIDTH_A
    vat_ref[...] = _dot_nt(wt_ref[r:r + WIDTH_A, :], h).astype(BF16)
    r += WIDTH_A
    t = _dot_nt(wt_ref[r:r + WIDTH_B, :], h)
    qbt_ref[...] = (_rope_rows(t, cos, sin, N_HEADS_B) * QK_SCALE).astype(BF16)
    r += WIDTH_B
    vbt_ref[...] = _dot_nt(wt_ref[r:r + WIDTH_KV_B, :], h).astype(BF16)
    r += WIDTH_KV_B
    t = _dot_nt(wt_ref[r:r + WIDTH_IDX, :], h)
    qit_ref[...] = (_rope_rows(t, cos, sin, N_IDX_HEADS) * IDX_SCALE).astype(BF16)
    r += WIDTH_IDX
    wit_ref[...] = _dot_nt(wt_ref[r:r + WI_ROWS, :], h) * (N_IDX_HEADS ** -0.5)

    ka_ref[...] = _dot(h, wk_ref[:, 0:WIDTH_A]).astype(BF16)
    c, sa, sb = c_ref[...], sa_ref[...], sb_ref[...]
    t = _dot(h, wk_ref[:, WIDTH_A:WIDTH_A + LANES])
    kb_ref[...] = _rope_lanes(t, c, sa, sb).astype(BF16)
    t = _dot(h, wk_ref[:, WIDTH_A + LANES:WIDTH_A + 2 * LANES])
    ki_ref[...] = _rope_lanes(t, c, sa, sb).astype(BF16)

    ga_ref[...] = _dot(h, wg_ref[:, 0:D_MODEL])
    gb_ref[...] = _dot(h, wg_ref[:, D_MODEL:2 * D_MODEL])


def _inproj(x3d, g, wt, wk, wg, cos_t, sin_t, c_tab, sa_tab, sb_tab):
    B, S, _ = x3d.shape
    tm = PROJ_TM
    full = lambda shape: pl.BlockSpec(shape, lambda b, s: (0,) * len(shape))
    tmaj = lambda w: pl.BlockSpec((None, tm, w), lambda b, s: (b, s, 0))
    fmaj = lambda w: pl.BlockSpec((None, w, tm), lambda b, s: (b, 0, s))
    tshape = lambda w, dt: jax.ShapeDtypeStruct((B, S, w), dt)
    fshape = lambda w, dt: jax.ShapeDtypeStruct((B, w, S), dt)
    return pl.pallas_call(
        _inproj_kernel,
        grid=(B, S // tm),
        in_specs=[
            tmaj(D_MODEL), full((1, D_MODEL)), full(wt.shape), full(wk.shape), full(wg.shape),
            pl.BlockSpec((ROT_HALF, tm), lambda b, s: (0, s)),
            pl.BlockSpec((ROT_HALF, tm), lambda b, s: (0, s)),
            pl.BlockSpec((tm, LANES), lambda b, s: (s, 0)),
            pl.BlockSpec((tm, LANES), lambda b, s: (s, 0)),
            pl.BlockSpec((tm, LANES), lambda b, s: (s, 0)),
        ],
        out_specs=[fmaj(WIDTH_A), fmaj(WIDTH_A), fmaj(WIDTH_B), fmaj(WIDTH_KV_B), fmaj(WIDTH_IDX),
                   fmaj(WI_ROWS), tmaj(WIDTH_A), tmaj(LANES), tmaj(LANES), tmaj(D_MODEL), tmaj(D_MODEL)],
        out_shape=[fshape(WIDTH_A, BF16), fshape(WIDTH_A, BF16), fshape(WIDTH_B, BF16),
                   fshape(WIDTH_KV_B, BF16), fshape(WIDTH_IDX, BF16), fshape(WI_ROWS, F32),
                   tshape(WIDTH_A, BF16), tshape(LANES, BF16), tshape(LANES, BF16),
                   tshape(D_MODEL, F32), tshape(D_MODEL, F32)],
        compiler_params=pltpu.CompilerParams(
            dimension_semantics=("parallel", "parallel"), vmem_limit_bytes=VMEM_LIMIT),
        name="inproj",
    )(x3d, g, wt, wk, wg, cos_t, sin_t, c_tab, sa_tab, sb_tab)


def _padded_rhs(qt_h, slot):
    z = jnp.zeros_like(qt_h)
    return jnp.concatenate([qt_h, z] if slot == 0 else [z, qt_h], axis=0)


def _mixer_a_kernel(qt_ref, k0_ref, k1_ref, k2_ref, v0_ref, v1_ref, v2_ref, ext_ref, o_ref,
                    bias_scr, s_scr):
    jb = pl.program_id(1)
    k_refs = (k0_ref, k1_ref, k2_ref)
    v_refs = (v0_ref, v1_ref, v2_ref)

    @pl.when((pl.program_id(0) == 0) & (jb == 0))
    def _():
        qc = lax.broadcasted_iota(jnp.int32, (QB, QB), 1) // CHUNK
        for m in range(BAND_BLOCKS):
            kc = lax.broadcasted_iota(jnp.int32, (QB, QB), 0) // CHUNK + m * (QB // CHUNK)
            valid = (kc >= qc) & (kc <= qc + N_PREV_CHUNKS)
            for h in range(N_HEADS_A):
                rows = jnp.broadcast_to(ext_ref[h:h + 1, :], (QB, BIAS_SPAN))
                skew = pltpu.roll(rows, m * QB, 1, stride=1, stride_axis=0)
                bias_scr[h, m * QB:(m + 1) * QB, :] = jnp.where(valid, skew[:, 0:QB] * LOG2E, -jnp.inf)

    ones_rows = jnp.ones((ONES_ROWS, QB), BF16)

    def logits_block(h, m, rhs):
        lane0 = (h // 2) * LANES
        sm = _dot(k_refs[m][:, lane0:lane0 + LANES], rhs) + bias_scr[h, m * QB:(m + 1) * QB, :]
        if m < BAND_BLOCKS - 1:
            sm = jnp.where(jb + m >= BAND_BLOCKS - 1, sm, -jnp.inf)
        s_scr[h % 2, m] = sm
        return sm.max(axis=0, keepdims=True)

    def pv_block(h, m, mx):
        r0 = h * HEAD_DIM
        p = jnp.exp2((s_scr[h % 2, m] - mx).astype(BF16))
        v_ext = jnp.concatenate([v_refs[m][r0:r0 + HEAD_DIM, :], ones_rows], axis=0)
        return _dot(v_ext, p)

    def head_rhs(h):
        return _padded_rhs(qt_ref[h * HEAD_DIM:(h + 1) * HEAD_DIM, :], h % 2)

    rhs = head_rhs(0)
    mx = functools.reduce(jnp.maximum, [logits_block(0, m, rhs) for m in range(BAND_BLOCKS)])
    for h in range(N_HEADS_A):
        r0 = h * HEAD_DIM
        acc = jnp.zeros((HEAD_DIM + ONES_ROWS, QB), F32)
        next_max = []
        if h + 1 < N_HEADS_A:
            rhs = head_rhs(h + 1)
        for m in range(BAND_BLOCKS):
            if h + 1 < N_HEADS_A:
                next_max.append(logits_block(h + 1, m, rhs))
            acc = acc + pv_block(h, m, mx)
        o_ref[r0:r0 + HEAD_DIM, :] = (acc[0:HEAD_DIM] / acc[HEAD_DIM:HEAD_DIM + 1]).astype(BF16)
        if next_max:
            mx = functools.reduce(jnp.maximum, next_max)


def _mixer_a(qat, ka, vat, bias_ext):
    B, _, S = qat.shape
    nb = BAND_BLOCKS - 1
    kspec = lambda m: pl.BlockSpec((None, QB, WIDTH_A), lambda b, j: (b, jnp.maximum(j + m - nb, 0), 0))
    vspec = lambda m: pl.BlockSpec((None, WIDTH_A, QB), lambda b, j: (b, 0, jnp.maximum(j + m - nb, 0)))
    return pl.pallas_call(
        _mixer_a_kernel,
        grid=(B, S // QB),
        in_specs=[pl.BlockSpec((None, WIDTH_A, QB), lambda b, j: (b, 0, j)),
                  kspec(0), kspec(1), kspec(2), vspec(0), vspec(1), vspec(2),
                  pl.BlockSpec(bias_ext.shape, lambda b, j: (0, 0))],
        out_specs=pl.BlockSpec((None, WIDTH_A, QB), lambda b, j: (b, 0, j)),
        out_shape=jax.ShapeDtypeStruct((B, WIDTH_A, S), BF16),
        scratch_shapes=[pltpu.VMEM((N_HEADS_A, BAND_BLOCKS * QB, QB), F32),
                        pltpu.VMEM((2, BAND_BLOCKS, QB, QB), F32)],
        compiler_params=pltpu.CompilerParams(
            dimension_semantics=("arbitrary", "arbitrary"), vmem_limit_bytes=VMEM_LIMIT),
        name="mixer_a",
    )(qat, ka, ka, ka, vat, vat, vat, bias_ext)


def _ordered_code_to_f32(u):
    bits = jnp.where(u < 0, u ^ jnp.int32(-2 ** 31), ~u)
    return lax.bitcast_convert_type(bits, F32)


def _mixer_b_kernel(qit_ref, wit_ref, qbt_ref, ki_ref, kb_ref, vbt_ref, o_ref,
                    score_scr, sb_scr, s_scr, s2_scr, tmax_scr, tmax2_scr, rhs_scr, acc_scr, m_scr, l_scr,
                    *, topk):
    jb = pl.program_id(1)
    n_tiles = jb + 1
    diag0 = pl.multiple_of(jb * KT, KT)
    q_chunk = lax.broadcasted_iota(jnp.int32, (1, QB), 1) // CHUNK
    adm_diag = lax.broadcasted_iota(jnp.int32, (KT, QB), 0) < (q_chunk + 1) * CHUNK
    zero_rows = jnp.zeros((HEAD_DIM, QB), BF16)

    def tile_loop(body, init):
        def wrapped(t, carry):
            return body(pl.multiple_of(t * KT, KT), carry)
        return lax.fori_loop(0, n_tiles, wrapped, init)

    def col_count(hit):
        return hit.reshape(KT // 8, 8, QB).sum(axis=0)

    for h in range(N_IDX_HEADS):
        rhs_scr[h, 0:IDX_DIM, :] = qit_ref[h * IDX_DIM:(h + 1) * IDX_DIM, :]
        rhs_scr[h, IDX_DIM:, :] = zero_rows
    w = wit_ref[...]

    def score_tile(k0, carry):
        ki_t = ki_ref[pl.ds(k0, KT), :]
        acc = jnp.zeros((KT, QB), F32)
        for h in range(N_IDX_HEADS):
            acc = acc + w[h:h + 1, :] * jnp.maximum(_dot(ki_t, rhs_scr[h]), 0.0)
        score_scr[pl.ds(k0, KT), :] = acc
        return carry

    tile_loop(score_tile, 0)
    score_scr[pl.ds(diag0, KT), :] = jnp.where(adm_diag, score_scr[pl.ds(diag0, KT), :], -jnp.inf)

    pad0 = pl.multiple_of(n_tiles * KT, KT)
    score_scr[pl.ds(pad0, KT), :] = jnp.full((KT, QB), -jnp.inf, F32)
    n_pairs = (n_tiles + 1) // 2

    def pair_loop(body, init):
        def wrapped(t, carry):
            return body(pl.multiple_of(t * 2 * KT, 2 * KT), carry)
        return lax.fori_loop(0, n_pairs, wrapped, init)

    def round_pair(k0, carry):
        sb_scr[pl.ds(k0, 2 * KT), :] = score_scr[pl.ds(k0, 2 * KT), :].astype(BF16)
        return carry

    pair_loop(round_pair, 0)

    def fold_rows(hit, rows):
        parts = [hit[r:r + rows] for r in range(0, 2 * KT, rows)]
        chains = parts[:FOLD_CHAINS]
        for i, p in enumerate(parts[FOLD_CHAINS:]):
            chains[i % FOLD_CHAINS] = chains[i % FOLD_CHAINS] + p
        while len(chains) > 1:
            chains = [a + b for a, b in zip(chains[0::2], chains[1::2])]
        return chains[0]

    def count_ge_bf16(cand):
        cand = cand.astype(BF16)

        def body(k0, cnt):
            hit = jnp.where(sb_scr[pl.ds(k0, 2 * KT), :] >= cand, jnp.int16(1), jnp.int16(0))
            return cnt + fold_rows(hit, 16)
        cnt = pair_loop(body, jnp.zeros((16, QB), jnp.int16))
        return cnt.astype(jnp.int32).sum(axis=0, keepdims=True)

    def count_ge_f32(cand):
        def body(k0, cnt):
            hit = jnp.where(score_scr[pl.ds(k0, 2 * KT), :] >= cand, 1.0, 0.0)
            return cnt + fold_rows(hit, 8)
        cnt = pair_loop(body, jnp.zeros((8, QB), F32))
        return cnt.sum(axis=0, keepdims=True).astype(jnp.int32)

    k_int = jnp.int32(int(topk))
    neg_inf_code = jnp.int32(0x007FFFFF)

    def accept(cnt, code):
        return (cnt >= k_int) | ((code >= 0) & (code <= neg_inf_code))

    def coarse_step(i, t_u):
        cand_u = t_u | lax.shift_left(jnp.int32(1), 31 - i)
        cnt = count_ge_bf16(_ordered_code_to_f32(cand_u))
        return jnp.where(accept(cnt, cand_u), cand_u, t_u)

    t_coarse = lax.fori_loop(0, 16, coarse_step, jnp.zeros((1, QB), jnp.int32))
    base = t_coarse - jnp.int32(1 << 16)

    def fine_step(i, off):
        cand_off = off | lax.shift_left(jnp.int32(1), 16 - i)
        cand_u = base + cand_off
        cnt = count_ge_f32(_ordered_code_to_f32(cand_u))
        return jnp.where(accept(cnt, cand_u), cand_off, off)

    off = lax.fori_loop(0, 17, fine_step, jnp.zeros((1, QB), jnp.int32))
    thr = _ordered_code_to_f32(base + off)

    def count_both(k0, carry):
        ge, gt = carry
        sc = score_scr[pl.ds(k0, KT), :]
        return (ge + col_count(jnp.where(sc >= thr, 1.0, 0.0)),
                gt + col_count(jnp.where(sc > thr, 1.0, 0.0)))

    ge8, gt8 = tile_loop(count_both, (jnp.zeros((8, QB), F32), jnp.zeros((8, QB), F32)))
    cnt_ge = ge8.sum(axis=0, keepdims=True)
    cnt_gt = gt8.sum(axis=0, keepdims=True)
    n_inadm = (KT - (q_chunk + 1) * CHUNK).astype(F32)
    n_sel = cnt_ge - jnp.where(thr == -jnp.inf, n_inadm, 0.0)
    has_ties = jnp.max(jnp.where(n_sel > topk, 1.0, 0.0)) > 0.0

    @pl.when(jnp.logical_not(has_ties))
    def _():
        def body(k0, carry):
            score_scr[pl.ds(k0, KT), :] = jnp.where(score_scr[pl.ds(k0, KT), :] >= thr, 0.0, -jnp.inf)
            return carry
        tile_loop(body, 0)

    @pl.when(has_ties)
    def _():
        tri = jnp.where(lax.broadcasted_iota(jnp.int32, (KT, KT), 1)
                        < lax.broadcasted_iota(jnp.int32, (KT, KT), 0), 1.0, 0.0).astype(BF16)

        def body(k0, need):
            sc = score_scr[pl.ds(k0, KT), :]
            eq_f = jnp.where(sc == thr, 1.0, 0.0)
            before = _dot(tri, eq_f.astype(BF16))
            take = jnp.where(sc > thr, 1.0, jnp.where(before < need, eq_f, 0.0))
            score_scr[pl.ds(k0, KT), :] = jnp.where(take > 0.0, 0.0, -jnp.inf)
            return need - eq_f.sum(axis=0, keepdims=True)
        tile_loop(body, topk - cnt_gt)

    score_scr[pl.ds(diag0, KT), :] = jnp.where(adm_diag, score_scr[pl.ds(diag0, KT), :], -jnp.inf)

    for h in range(N_HEADS_B):
        g = h // (N_HEADS_B // N_KV_B)
        q_h = qbt_ref[h * HEAD_DIM:(h + 1) * HEAD_DIM, :]
        rhs_scr[h, 0:HEAD_DIM, :] = q_h if g == 0 else zero_rows
        rhs_scr[h, HEAD_DIM:, :] = zero_rows if g == 0 else q_h
    m_scr[...] = jnp.full_like(m_scr, NEG_BIG)
    l_scr[...] = jnp.zeros_like(l_scr)
    acc_scr[...] = jnp.zeros_like(acc_scr)
    ones_rows = jnp.ones((ONES_ROWS, KT), BF16)

    def logits_head(h, neg, kb_t, s_buf, tmax_buf):
        s = _dot(kb_t, rhs_scr[h]) + neg
        s_buf[h] = s
        tmax_buf[h:h + 1, :] = s.max(axis=0, keepdims=True)

    def softmax_pv_head(h, v_ext, s_buf, tmax_buf):
        g = h // (N_HEADS_B // N_KV_B)
        m_old = m_scr[h:h + 1, :]
        m_new = jnp.maximum(m_old, tmax_buf[h:h + 1, :])
        m_scr[h:h + 1, :] = m_new
        alpha = jnp.exp2(m_old - m_new)
        p = jnp.exp2((s_buf[h] - m_new).astype(BF16))
        pv = _dot(v_ext[g], p)
        r0 = h * HEAD_DIM
        acc_scr[r0:r0 + HEAD_DIM, :] = alpha * acc_scr[r0:r0 + HEAD_DIM, :] + pv[0:HEAD_DIM]
        l_scr[h:h + 1, :] = alpha * l_scr[h:h + 1, :] + pv[HEAD_DIM:HEAD_DIM + 1]

    def stage(t_next, next_bufs, t_cur, cur_bufs):
        k_next = pl.multiple_of(jnp.minimum(t_next, 2 * n_pairs - 1) * KT, KT)
        neg = score_scr[pl.ds(k_next, KT), :]
        kb_t = kb_ref[pl.ds(k_next, KT), :]
        if t_cur is not None:
            k_cur = pl.multiple_of(t_cur * KT, KT)
            v_ext = [jnp.concatenate([vbt_ref[g * HEAD_DIM:(g + 1) * HEAD_DIM, pl.ds(k_cur, KT)], ones_rows],
                                     axis=0) for g in range(N_KV_B)]
        for h in range(N_HEADS_B):
            logits_head(h, neg, kb_t, *next_bufs)
            if t_cur is not None:
                softmax_pv_head(h, v_ext, *cur_bufs)

    bufs_a, bufs_b = (s_scr, tmax_scr), (s2_scr, tmax2_scr)
    stage(0, bufs_a, None, None)

    def attn_pair(i, carry):
        t = 2 * i
        stage(t + 1, bufs_b, t, bufs_a)
        stage(t + 2, bufs_a, t + 1, bufs_b)
        return carry

    lax.fori_loop(0, n_pairs, attn_pair, 0)

    for h in range(N_HEADS_B):
        r0 = h * HEAD_DIM
        o_ref[r0:r0 + HEAD_DIM, :] = (acc_scr[r0:r0 + HEAD_DIM, :] / l_scr[h:h + 1, :]).astype(BF16)


def _mixer_b(qit, wit, qbt, ki, kb, vbt):
    B, _, S = qbt.shape
    topk = min(TOPK_MAX, S // 4)
    qspec = lambda w: pl.BlockSpec((None, w, QB), lambda b, j: (b, 0, j))
    return pl.pallas_call(
        functools.partial(_mixer_b_kernel, topk=float(topk)),
        grid=(B, S // QB),
        in_specs=[qspec(WIDTH_IDX), qspec(WI_ROWS), qspec(WIDTH_B),
                  pl.BlockSpec((None, S, LANES), lambda b, j: (b, 0, 0)),
                  pl.BlockSpec((None, S, LANES), lambda b, j: (b, 0, 0)),
                  pl.BlockSpec((None, WIDTH_KV_B, S), lambda b, j: (b, 0, 0))],
        out_specs=qspec(WIDTH_B),
        out_shape=jax.ShapeDtypeStruct((B, WIDTH_B, S), BF16),
        scratch_shapes=[pltpu.VMEM((S + KT, QB), F32),
                        pltpu.VMEM((S + KT, QB), BF16),
                        pltpu.VMEM((N_HEADS_B, KT, QB), F32),
                        pltpu.VMEM((N_HEADS_B, KT, QB), F32),
                        pltpu.VMEM((N_HEADS_B, QB), F32),
                        pltpu.VMEM((N_HEADS_B, QB), F32),
                        pltpu.VMEM((N_HEADS_B, 2 * HEAD_DIM, QB), BF16),
                        pltpu.VMEM((WIDTH_B, QB), F32),
                        pltpu.VMEM((N_HEADS_B, QB), F32),
                        pltpu.VMEM((N_HEADS_B, QB), F32)],
        compiler_params=pltpu.CompilerParams(
            dimension_semantics=("parallel", "arbitrary"), vmem_limit_bytes=VMEM_LIMIT),
        name="mixer_b",
    )(qit, wit, qbt, ki, kb, vbt)


def _merge_kernel(x_ref, oat_ref, obt_ref, ga_ref, gb_ref, wa_ref, wb_ref, wo_ref, o_ref):
    ya = _dot_tn(oat_ref[...], wa_ref[...])
    yb = _dot_tn(obt_ref[...], wb_ref[...])
    merged = jax.nn.sigmoid(ga_ref[...]) * ya + jax.nn.sigmoid(gb_ref[...]) * yb
    o_ref[...] = x_ref[...] + _dot(merged.astype(BF16), wo_ref[...])


def _merge(x3d, oat, obt, ga, gb, wa, wb, wo):
    B, S, _ = x3d.shape
    tm = PROJ_TM
    tmaj = pl.BlockSpec((None, tm, D_MODEL), lambda b, s: (b, s, 0))
    fmaj = lambda w: pl.BlockSpec((None, w, tm), lambda b, s: (b, 0, s))
    full = lambda a: pl.BlockSpec(a.shape, lambda b, s: (0, 0))
    return pl.pallas_call(
        _merge_kernel,
        grid=(B, S // tm),
        in_specs=[tmaj, fmaj(WIDTH_A), fmaj(WIDTH_B), tmaj, tmaj, full(wa), full(wb), full(wo)],
        out_specs=tmaj,
        out_shape=jax.ShapeDtypeStruct((B, S, D_MODEL), F32),
        compiler_params=pltpu.CompilerParams(
            dimension_semantics=("parallel", "parallel"), vmem_limit_bytes=VMEM_LIMIT),
        name="merge",
    )(x3d, oat, obt, ga, gb, wa, wb, wo)


def _rope_tables(seq):
    inv_freq = jnp.power(jnp.float32(ROPE_THETA), -jnp.arange(0, ROT_DIM, 2, dtype=F32) / ROT_DIM)
    ang = jnp.arange(seq, dtype=F32)[:, None] * inv_freq[None, :]
    cos, sin = jnp.cos(ang), jnp.sin(ang)
    ones = jnp.ones((seq, HEAD_DIM - ROT_DIM), F32)
    zeros = jnp.zeros((seq, HEAD_DIM - ROT_DIM), F32)
    zh = jnp.zeros((seq, ROT_HALF), F32)
    c = jnp.concatenate([cos, cos, ones], axis=1)
    sa = jnp.concatenate([-sin, zh, zeros], axis=1)
    sb = jnp.concatenate([zh, sin, zeros], axis=1)
    rep = LANES // HEAD_DIM
    return cos.T, sin.T, jnp.tile(c, (1, rep)), jnp.tile(sa, (1, rep)), jnp.tile(sb, (1, rep))


def _band_bias_ext(rel_bias):
    n_keys = BAND_BLOCKS * QB
    n_h = rel_bias.shape[0]
    lo = jnp.broadcast_to(rel_bias[:, :1], (n_h, QB - 1 - REL_CLIP))
    hi = jnp.broadcast_to(rel_bias[:, -1:], (n_h, BIAS_SPAN - (QB - 1 - REL_CLIP) - (2 * REL_CLIP + 1)))
    ext = jnp.concatenate([lo, rel_bias, hi], axis=1)
    return jnp.roll(ext, -(n_keys - 1), axis=1).astype(F32)


def kernel(x, n1_g, ffn1_w_in, ffn1_w_out, n2_g, w_in, rel_bias, w_branch_a, w_branch_b, w_out,
           n3_g, ffn2_w_in, ffn2_w_out, nf_g):
    B, S, D = x.shape
    depth = n1_g.shape[0]
    cos_t, sin_t, c_tab, sa_tab, sb_tab = _rope_tables(S)
    nf = nf_g.reshape(1, D)
    offs = np.cumsum([0, WIDTH_A, WIDTH_A, WIDTH_A, WIDTH_B, WIDTH_KV_B, WIDTH_KV_B,
                      WIDTH_IDX, IDX_DIM, N_IDX_HEADS, D_MODEL, D_MODEL])
    seg = lambda w, i: w[:, offs[i]:offs[i + 1]]

    for l in range(depth):
        w = w_in[l]
        wt = jnp.concatenate(
            [seg(w, 0), seg(w, 2), seg(w, 3), seg(w, 5), seg(w, 6),
             jnp.pad(seg(w, 8), ((0, 0), (0, WI_ROWS - N_IDX_HEADS)))], axis=1).T.astype(BF16)
        wk = jnp.concatenate(
            [seg(w, 1), seg(w, 4), jnp.pad(seg(w, 7), ((0, 0), (0, LANES - IDX_DIM)))], axis=1).astype(BF16)
        wg = jnp.concatenate([seg(w, 9), seg(w, 10)], axis=1).astype(BF16)

        x2d = _ffn(x.reshape(B * S, D), n1_g[l].reshape(1, D), ffn1_w_in[l].astype(BF16),
                   ffn1_w_out[l].astype(BF16), nf, False)
        x = x2d.reshape(B, S, D)
        (qat, vat, qbt, vbt, qit, wit, ka, kb, ki, ga, gb) = _inproj(
            x, n2_g[l].reshape(1, D), wt, wk, wg, cos_t, sin_t, c_tab, sa_tab, sb_tab)
        oat = _mixer_a(qat, ka, vat, _band_bias_ext(rel_bias[l]))
        obt = _mixer_b(qit, wit, qbt, ki, kb, vbt)
        x = _merge(x, oat, obt, ga, gb, w_branch_a[l].astype(BF16), w_branch_b[l].astype(BF16),
                   w_out[l].astype(BF16))
        last = l == depth - 1
        x2d = _ffn(x.reshape(B * S, D), n3_g[l].reshape(1, D), ffn2_w_in[l].astype(BF16),
                   ffn2_w_out[l].astype(BF16), nf, last)
        x = x2d.reshape(B, S, D)
    return x
```

```python
import functools

import jax
import jax.numpy as jnp
import numpy as np
from jax import lax
from jax.experimental import pallas as pl
from jax.experimental.pallas import tpu as pltpu

F32 = jnp.float32
BF16 = jnp.bfloat16

D_MODEL = 1024
D_FF = 2816
HEAD_DIM = 64
CHUNK = 64
N_PREV_CHUNKS = 8
N_HEADS_A = 8
REL_CLIP = 128
N_HEADS_B = 8
N_KV_B = 2
N_IDX_HEADS = 8
IDX_DIM = 64
TOPK_MAX = 256
ROPE_THETA = 500000.0
ROT_DIM = HEAD_DIM // 4
ROT_HALF = ROT_DIM // 2
EPS = 1e-6
WIDTH_A = N_HEADS_A * HEAD_DIM
WIDTH_B = N_HEADS_B * HEAD_DIM
WIDTH_KV_B = N_KV_B * HEAD_DIM
WIDTH_IDX = N_IDX_HEADS * IDX_DIM

LANES = 128
LOG2E = 1.4426950408889634
QK_SCALE = HEAD_DIM ** -0.5 * LOG2E
IDX_SCALE = IDX_DIM ** -0.5

VMEM_LIMIT = 56 * 1024 * 1024

FFN_TM = 512
FFN_TF = D_FF // 2
PROJ_TM = 512
QB = 256
KT = 256
BAND_BLOCKS = N_PREV_CHUNKS * CHUNK // QB + 1
BIAS_SPAN = (BAND_BLOCKS + 1) * QB
WI_ROWS = 16
ONES_ROWS = 16
NEG_BIG = -1e30
FINE_STEPS = 17
FINE_STEPS_FIRST = 13
FOLD_CHAINS = 8


def _dot(a, b):
    return jnp.dot(a, b, preferred_element_type=F32)


def _dot_nt(a, b):
    return lax.dot_general(a, b, (((1,), (1,)), ((), ())), preferred_element_type=F32)


def _dot_tn(a, b):
    return lax.dot_general(a, b, (((0,), (0,)), ((), ())), preferred_element_type=F32)


def _rmsnorm(x, g):
    ms = jnp.mean(x * x, axis=-1, keepdims=True)
    return x * lax.rsqrt(ms + EPS) * g


def _ffn_kernel(x_ref, g_ref, wg_ref, wu_ref, wo_ref, gf_ref, o_ref, h_scr, acc_scr, *, final_norm):
    j = pl.program_id(1)

    @pl.when(j == 0)
    def _():
        h_scr[...] = _rmsnorm(x_ref[...], g_ref[...]).astype(BF16)
        acc_scr[...] = jnp.zeros_like(acc_scr)

    h = h_scr[...]
    gate = _dot(h, wg_ref[...])
    up = _dot(h, wu_ref[...])
    a = (gate * jax.nn.sigmoid(gate) * up).astype(BF16)
    acc_scr[...] += _dot(a, wo_ref[...])

    @pl.when(j == pl.num_programs(1) - 1)
    def _():
        y = x_ref[...] + 0.5 * acc_scr[...]
        if final_norm:
            y = _rmsnorm(y, gf_ref[...])
        o_ref[...] = y


def _ffn(x2d, g, w_in_bf, w_out_bf, gf, final_norm):
    T = x2d.shape[0]
    nf = D_FF // FFN_TF
    return pl.pallas_call(
        functools.partial(_ffn_kernel, final_norm=final_norm),
        grid=(T // FFN_TM, nf),
        in_specs=[
            pl.BlockSpec((FFN_TM, D_MODEL), lambda i, j: (i, 0)),
            pl.BlockSpec((1, D_MODEL), lambda i, j: (0, 0)),
            pl.BlockSpec((D_MODEL, FFN_TF), lambda i, j: (0, j)),
            pl.BlockSpec((D_MODEL, FFN_TF), lambda i, j: (0, j + nf)),
            pl.BlockSpec((FFN_TF, D_MODEL), lambda i, j: (j, 0)),
            pl.BlockSpec((1, D_MODEL), lambda i, j: (0, 0)),
        ],
        out_specs=pl.BlockSpec((FFN_TM, D_MODEL), lambda i, j: (i, 0)),
        out_shape=jax.ShapeDtypeStruct((T, D_MODEL), F32),
        scratch_shapes=[pltpu.VMEM((FFN_TM, D_MODEL), BF16), pltpu.VMEM((FFN_TM, D_MODEL), F32)],
        compiler_params=pltpu.CompilerParams(
            dimension_semantics=("parallel", "arbitrary"), vmem_limit_bytes=VMEM_LIMIT),
        name="ffn_final" if final_norm else "ffn",
    )(x2d, g, w_in_bf, w_in_bf, w_out_bf, gf)


def _rope_rows(x, cos, sin, n_heads):
    pieces = []
    for h in range(n_heads):
        r0 = h * HEAD_DIM
        t1 = x[r0:r0 + ROT_HALF]
        t2 = x[r0 + ROT_HALF:r0 + ROT_DIM]
        pieces.append(t1 * cos - t2 * sin)
        pieces.append(t2 * cos + t1 * sin)
        pieces.append(x[r0 + ROT_DIM:r0 + HEAD_DIM])
    return jnp.concatenate(pieces, axis=0)


def _rope_lanes(x, c, sa, sb):
    return x * c + pltpu.roll(x, LANES - ROT_HALF, 1) * sa + pltpu.roll(x, ROT_HALF, 1) * sb


def _inproj_kernel(x_ref, g_ref, wt_ref, wk_ref, wg_ref, cos_ref, sin_ref, c_ref, sa_ref, sb_ref,
                   qat_ref, vat_ref, qbt_ref, vbt_ref, qit_ref, wit_ref,
                   ka_ref, kb_ref, ki_ref, ga_ref, gb_ref):
    h = _rmsnorm(x_ref[...], g_ref[...]).astype(BF16)
    cos = cos_ref[...]
    sin = sin_ref[...]

    r = 0
    t = _dot_nt(wt_ref[r:r + WIDTH_A, :], h)
    qat_ref[...] = (t * QK_SCALE).astype(BF16)
    r += WIDTH_A
    vat_ref[...] = _dot_nt(wt_ref[r:r + WIDTH_A, :], h).astype(BF16)
    r += WIDTH_A
    t = _dot_nt(wt_ref[r:r + WIDTH_B, :], h)
    qbt_ref[...] = (_rope_rows(t, cos, sin, N_HEADS_B) * QK_SCALE).astype(BF16)
    r += WIDTH_B
    vbt_ref[...] = _dot_nt(wt_ref[r:r + WIDTH_KV_B, :], h).astype(BF16)
    r += WIDTH_KV_B
    t = _dot_nt(wt_ref[r:r + WIDTH_IDX, :], h)
    qit_ref[...] = (_rope_rows(t, cos, sin, N_IDX_HEADS) * IDX_SCALE).astype(BF16)
    r += WIDTH_IDX
    wit_ref[...] = _dot_nt(wt_ref[r:r + WI_ROWS, :], h) * (N_IDX_HEADS ** -0.5)

    ka_ref[...] = _dot(h, wk_ref[:, 0:WIDTH_A]).astype(BF16)
    c, sa, sb = c_ref[...], sa_ref[...], sb_ref[...]
    t = _dot(h, wk_ref[:, WIDTH_A:WIDTH_A + LANES])
    kb_ref[...] = _rope_lanes(t, c, sa, sb).astype(BF16)
    t = _dot(h, wk_ref[:, WIDTH_A + LANES:WIDTH_A + 2 * LANES])
    ki_ref[...] = _rope_lanes(t, c, sa, sb).astype(BF16)

    ga_ref[...] = _dot(h, wg_ref[:, 0:D_MODEL])
    gb_ref[...] = _dot(h, wg_ref[:, D_MODEL:2 * D_MODEL])


def _inproj(x3d, g, wt, wk, wg, cos_t, sin_t, c_tab, sa_tab, sb_tab):
    B, S, _ = x3d.shape
    tm = PROJ_TM
    full = lambda shape: pl.BlockSpec(shape, lambda b, s: (0,) * len(shape))
    tmaj = lambda w: pl.BlockSpec((None, tm, w), lambda b, s: (b, s, 0))
    fmaj = lambda w: pl.BlockSpec((None, w, tm), lambda b, s: (b, 0, s))
    tshape = lambda w, dt: jax.ShapeDtypeStruct((B, S, w), dt)
    fshape = lambda w, dt: jax.ShapeDtypeStruct((B, w, S), dt)
    return pl.pallas_call(
        _inproj_kernel,
        grid=(B, S // tm),
        in_specs=[
            tmaj(D_MODEL), full((1, D_MODEL)), full(wt.shape), full(wk.shape), full(wg.shape),
            pl.BlockSpec((ROT_HALF, tm), lambda b, s: (0, s)),
            pl.BlockSpec((ROT_HALF, tm), lambda b, s: (0, s)),
            pl.BlockSpec((tm, LANES), lambda b, s: (s, 0)),
            pl.BlockSpec((tm, LANES), lambda b, s: (s, 0)),
            pl.BlockSpec((tm, LANES), lambda b, s: (s, 0)),
        ],
        out_specs=[fmaj(WIDTH_A), fmaj(WIDTH_A), fmaj(WIDTH_B), fmaj(WIDTH_KV_B), fmaj(WIDTH_IDX),
                   fmaj(WI_ROWS), tmaj(WIDTH_A), tmaj(LANES), tmaj(LANES), tmaj(D_MODEL), tmaj(D_MODEL)],
        out_shape=[fshape(WIDTH_A, BF16), fshape(WIDTH_A, BF16), fshape(WIDTH_B, BF16),
                   fshape(WIDTH_KV_B, BF16), fshape(WIDTH_IDX, BF16), fshape(WI_ROWS, F32),
                   tshape(WIDTH_A, BF16), tshape(LANES, BF16), tshape(LANES, BF16),
                   tshape(D_MODEL, F32), tshape(D_MODEL, F32)],
        compiler_params=pltpu.CompilerParams(
            dimension_semantics=("parallel", "parallel"), vmem_limit_bytes=VMEM_LIMIT),
        name="inproj",
    )(x3d, g, wt, wk, wg, cos_t, sin_t, c_tab, sa_tab, sb_tab)


def _padded_rhs(qt_h, slot):
    z = jnp.zeros_like(qt_h)
    return jnp.concatenate([qt_h, z] if slot == 0 else [z, qt_h], axis=0)


def _mixer_a_kernel(qt_ref, k0_ref, k1_ref, k2_ref, v0_ref, v1_ref, v2_ref, ext_ref, o_ref,
                    bias_scr, s_scr):
    jb = pl.program_id(1)
    k_refs = (k0_ref, k1_ref, k2_ref)
    v_refs = (v0_ref, v1_ref, v2_ref)

    @pl.when((pl.program_id(0) == 0) & (jb == 0))
    def _():
        qc = lax.broadcasted_iota(jnp.int32, (QB, QB), 1) // CHUNK
        for m in range(BAND_BLOCKS):
            kc = lax.broadcasted_iota(jnp.int32, (QB, QB), 0) // CHUNK + m * (QB // CHUNK)
            valid = (kc >= qc) & (kc <= qc + N_PREV_CHUNKS)
            for h in range(N_HEADS_A):
                rows = jnp.broadcast_to(ext_ref[h:h + 1, :], (QB, BIAS_SPAN))
                skew = pltpu.roll(rows, m * QB, 1, stride=1, stride_axis=0)
                bias_scr[h, m * QB:(m + 1) * QB, :] = jnp.where(valid, skew[:, 0:QB] * LOG2E, -jnp.inf)

    ones_rows = jnp.ones((ONES_ROWS, QB), BF16)

    def logits_block(h, m, rhs):
        lane0 = (h // 2) * LANES
        sm = _dot(k_refs[m][:, lane0:lane0 + LANES], rhs) + bias_scr[h, m * QB:(m + 1) * QB, :]
        if m < BAND_BLOCKS - 1:
            sm = jnp.where(jb + m >= BAND_BLOCKS - 1, sm, -jnp.inf)
        s_scr[h % 2, m] = sm
        return sm.max(axis=0, keepdims=True)

    def pv_block(h, m, mx):
        r0 = h * HEAD_DIM
        p = jnp.exp2((s_scr[h % 2, m] - mx).astype(BF16))
        v_ext = jnp.concatenate([v_refs[m][r0:r0 + HEAD_DIM, :], ones_rows], axis=0)
        return _dot(v_ext, p)

    def head_rhs(h):
        return _padded_rhs(qt_ref[h * HEAD_DIM:(h + 1) * HEAD_DIM, :], h % 2)

    rhs = head_rhs(0)
    mx = functools.reduce(jnp.maximum, [logits_block(0, m, rhs) for m in range(BAND_BLOCKS)])
    for h in range(N_HEADS_A):
        r0 = h * HEAD_DIM
        acc = jnp.zeros((HEAD_DIM + ONES_ROWS, QB), F32)
        next_max = []
        if h + 1 < N_HEADS_A:
            rhs = head_rhs(h + 1)
        for m in range(BAND_BLOCKS):
            if h + 1 < N_HEADS_A:
                next_max.append(logits_block(h + 1, m, rhs))
            acc = acc + pv_block(h, m, mx)
        o_ref[r0:r0 + HEAD_DIM, :] = (acc[0:HEAD_DIM] / acc[HEAD_DIM:HEAD_DIM + 1]).astype(BF16)
        if next_max:
            mx = functools.reduce(jnp.maximum, next_max)


def _mixer_a(qat, ka, vat, bias_ext):
    B, _, S = qat.shape
    nb = BAND_BLOCKS - 1
    kspec = lambda m: pl.BlockSpec((None, QB, WIDTH_A), lambda b, j: (b, jnp.maximum(j + m - nb, 0), 0))
    vspec = lambda m: pl.BlockSpec((None, WIDTH_A, QB), lambda b, j: (b, 0, jnp.maximum(j + m - nb, 0)))
    return pl.pallas_call(
        _mixer_a_kernel,
        grid=(B, S // QB),
        in_specs=[pl.BlockSpec((None, WIDTH_A, QB), lambda b, j: (b, 0, j)),
                  kspec(0), kspec(1), kspec(2), vspec(0), vspec(1), vspec(2),
                  pl.BlockSpec(bias_ext.shape, lambda b, j: (0, 0))],
        out_specs=pl.BlockSpec((None, WIDTH_A, QB), lambda b, j: (b, 0, j)),
        out_shape=jax.ShapeDtypeStruct((B, WIDTH_A, S), BF16),
        scratch_shapes=[pltpu.VMEM((N_HEADS_A, BAND_BLOCKS * QB, QB), F32),
                        pltpu.VMEM((2, BAND_BLOCKS, QB, QB), F32)],
        compiler_params=pltpu.CompilerParams(
            dimension_semantics=("arbitrary", "arbitrary"), vmem_limit_bytes=VMEM_LIMIT),
        name="mixer_a",
    )(qat, ka, ka, ka, vat, vat, vat, bias_ext)


def _ordered_code_to_f32(u):
    bits = jnp.where(u < 0, u ^ jnp.int32(-2 ** 31), ~u)
    return lax.bitcast_convert_type(bits, F32)


def _mixer_b_kernel(qit_ref, wit_ref, qbt_ref, ki_ref, kb_ref, vbt_ref, o_ref,
                    score_scr, sb_scr, s_scr, s2_scr, tmax_scr, tmax2_scr, rhs_scr, acc_scr, m_scr, l_scr,
                    thr_scr, need_scr, ties_scr, *, topk):
    jb = pl.program_id(1)
    n_tiles = jb + 1
    diag0 = pl.multiple_of(jb * KT, KT)
    q_chunk = lax.broadcasted_iota(jnp.int32, (1, QB), 1) // CHUNK
    adm_diag = lax.broadcasted_iota(jnp.int32, (KT, QB), 0) < (q_chunk + 1) * CHUNK
    zero_rows = jnp.zeros((HEAD_DIM, QB), BF16)

    def tile_loop(body, init):
        def wrapped(t, carry):
            return body(pl.multiple_of(t * KT, KT), carry)
        return lax.fori_loop(0, n_tiles, wrapped, init)

    def col_count(hit):
        return hit.reshape(KT // 8, 8, QB).sum(axis=0)

    for h in range(N_IDX_HEADS):
        rhs_scr[h, 0:IDX_DIM, :] = qit_ref[h * IDX_DIM:(h + 1) * IDX_DIM, :]
        rhs_scr[h, IDX_DIM:, :] = zero_rows
    w = wit_ref[...]

    n_pairs = (n_tiles + 1) // 2
    pad0 = pl.multiple_of(n_tiles * KT, KT)

    def pair_loop(body, init):
        def wrapped(t, carry):
            return body(pl.multiple_of(t * 2 * KT, 2 * KT), carry)
        return lax.fori_loop(0, n_pairs, wrapped, init)

    def score_pair(k0, carry):
        ki_a = ki_ref[pl.ds(k0, KT), :]
        ki_b = ki_ref[pl.ds(k0 + KT, KT), :]
        acc_a = jnp.zeros((KT, QB), F32)
        acc_b = jnp.zeros((KT, QB), F32)
        for h in range(N_IDX_HEADS):
            acc_a = acc_a + w[h:h + 1, :] * jnp.maximum(_dot(ki_a, rhs_scr[h]), 0.0)
            acc_b = acc_b + w[h:h + 1, :] * jnp.maximum(_dot(ki_b, rhs_scr[h]), 0.0)
        score_scr[pl.ds(k0, KT), :] = acc_a
        score_scr[pl.ds(k0 + KT, KT), :] = acc_b
        return carry

    pair_loop(score_pair, 0)
    score_scr[pl.ds(diag0, KT), :] = jnp.where(adm_diag, score_scr[pl.ds(diag0, KT), :], -jnp.inf)
    score_scr[pl.ds(pad0, KT), :] = jnp.full((KT, QB), -jnp.inf, F32)

    def round_pair(k0, carry):
        sb_scr[pl.ds(k0, 2 * KT), :] = score_scr[pl.ds(k0, 2 * KT), :].astype(BF16)
        return carry

    pair_loop(round_pair, 0)

    def fold_rows(hit, rows):
        parts = [hit[r:r + rows] for r in range(0, 2 * KT, rows)]
        chains = parts[:FOLD_CHAINS]
        for i, p in enumerate(parts[FOLD_CHAINS:]):
            chains[i % FOLD_CHAINS] = chains[i % FOLD_CHAINS] + p
        while len(chains) > 1:
            chains = [a + b for a, b in zip(chains[0::2], chains[1::2])]
        return chains[0]

    def count_ge_bf16(cand):
        cand = cand.astype(BF16)

        def body(k0, cnt):
            hit = jnp.where(sb_scr[pl.ds(k0, 2 * KT), :] >= cand, jnp.int16(1), jnp.int16(0))
            return cnt + fold_rows(hit, 16)
        cnt = pair_loop(body, jnp.zeros((16, QB), jnp.int16))
        return cnt.astype(jnp.int32).sum(axis=0, keepdims=True)

    def count_ge_f32(cand):
        def body(k0, cnt):
            hit = jnp.where(score_scr[pl.ds(k0, 2 * KT), :] >= cand, 1.0, 0.0)
            return cnt + fold_rows(hit, 8)
        cnt = pair_loop(body, jnp.zeros((8, QB), F32))
        return cnt.sum(axis=0, keepdims=True).astype(jnp.int32)

    k_int = jnp.int32(int(topk))
    neg_inf_code = jnp.int32(0x007FFFFF)

    def accept(cnt, code):
        return (cnt >= k_int) | ((code >= 0) & (code <= neg_inf_code))

    def coarse_step(i, t_u):
        cand_u = t_u | lax.shift_left(jnp.int32(1), 31 - i)
        cnt = count_ge_bf16(_ordered_code_to_f32(cand_u))
        return jnp.where(accept(cnt, cand_u), cand_u, t_u)

    t_coarse = lax.fori_loop(0, 16, coarse_step, jnp.zeros((1, QB), jnp.int32))
    base = t_coarse - jnp.int32(1 << 16)

    def fine_step(i, carry):
        off, cnt_at = carry
        cand_off = off | lax.shift_left(jnp.int32(1), 16 - i)
        cand_u = base + cand_off
        cnt = count_ge_f32(_ordered_code_to_f32(cand_u))
        ok = accept(cnt, cand_u)
        return jnp.where(ok, cand_off, off), jnp.where(ok, cnt, cnt_at)

    off, cnt_at = lax.fori_loop(0, FINE_STEPS_FIRST, fine_step,
                                (jnp.zeros((1, QB), jnp.int32), jnp.full((1, QB), -1, jnp.int32)))
    thr_scr[...] = _ordered_code_to_f32(base + off)
    ties_scr[0] = jnp.int32(0)
    unsettled = jnp.max(jnp.where(cnt_at == k_int, 0.0, 1.0)) > 0.0

    @pl.when(unsettled)
    def _():
        off_full, _ = lax.fori_loop(FINE_STEPS_FIRST, FINE_STEPS, fine_step, (off, cnt_at))
        thr_full = _ordered_code_to_f32(base + off_full)
        thr_scr[...] = thr_full

        def count_both(k0, carry):
            ge, gt = carry
            sc = score_scr[pl.ds(k0, KT), :]
            return (ge + col_count(jnp.where(sc >= thr_full, 1.0, 0.0)),
                    gt + col_count(jnp.where(sc > thr_full, 1.0, 0.0)))

        ge8, gt8 = tile_loop(count_both, (jnp.zeros((8, QB), F32), jnp.zeros((8, QB), F32)))
        cnt_ge = ge8.sum(axis=0, keepdims=True)
        need_scr[...] = topk - gt8.sum(axis=0, keepdims=True)
        n_inadm = (KT - (q_chunk + 1) * CHUNK).astype(F32)
        n_sel = cnt_ge - jnp.where(thr_full == -jnp.inf, n_inadm, 0.0)
        ties_scr[0] = (jnp.max(jnp.where(n_sel > topk, 1.0, 0.0)) > 0.0).astype(jnp.int32)

    thr = thr_scr[...]
    has_ties = ties_scr[0] > 0

    @pl.when(jnp.logical_not(has_ties))
    def _():
        def body(k0, carry):
            score_scr[pl.ds(k0, KT), :] = jnp.where(score_scr[pl.ds(k0, KT), :] >= thr, 0.0, -jnp.inf)
            return carry
        tile_loop(body, 0)

    @pl.when(has_ties)
    def _():
        tri = jnp.where(lax.broadcasted_iota(jnp.int32, (KT, KT), 1)
                        < lax.broadcasted_iota(jnp.int32, (KT, KT), 0), 1.0, 0.0).astype(BF16)

        def body(k0, need):
            sc = score_scr[pl.ds(k0, KT), :]
            eq_f = jnp.where(sc == thr, 1.0, 0.0)
            before = _dot(tri, eq_f.astype(BF16))
            take = jnp.where(sc > thr, 1.0, jnp.where(before < need, eq_f, 0.0))
            score_scr[pl.ds(k0, KT), :] = jnp.where(take > 0.0, 0.0, -jnp.inf)
            return need - eq_f.sum(axis=0, keepdims=True)
        tile_loop(body, need_scr[...])

    score_scr[pl.ds(diag0, KT), :] = jnp.where(adm_diag, score_scr[pl.ds(diag0, KT), :], -jnp.inf)

    for h in range(N_HEADS_B):
        g = h // (N_HEADS_B // N_KV_B)
        q_h = qbt_ref[h * HEAD_DIM:(h + 1) * HEAD_DIM, :]
        rhs_scr[h, 0:HEAD_DIM, :] = q_h if g == 0 else zero_rows
        rhs_scr[h, HEAD_DIM:, :] = zero_rows if g == 0 else q_h
    m_scr[...] = jnp.full_like(m_scr, NEG_BIG)
    l_scr[...] = jnp.zeros_like(l_scr)
    acc_scr[...] = jnp.zeros_like(acc_scr)
    ones_rows = jnp.ones((ONES_ROWS, KT), BF16)

    def logits_head(h, neg, kb_t, s_buf, tmax_buf):
        s = _dot(kb_t, rhs_scr[h]) + neg
        s_buf[h] = s
        tmax_buf[h:h + 1, :] = s.max(axis=0, keepdims=True)

    def softmax_pv_head(h, v_ext, s_buf, tmax_buf):
        g = h // (N_HEADS_B // N_KV_B)
        m_old = m_scr[h:h + 1, :]
        m_new = jnp.maximum(m_old, tmax_buf[h:h + 1, :])
        m_scr[h:h + 1, :] = m_new
        alpha = jnp.exp2(m_old - m_new)
        p = jnp.exp2((s_buf[h] - m_new).astype(BF16))
        pv = _dot(v_ext[g], p)
        r0 = h * HEAD_DIM
        acc_scr[r0:r0 + HEAD_DIM, :] = alpha * acc_scr[r0:r0 + HEAD_DIM, :] + pv[0:HEAD_DIM]
        l_scr[h:h + 1, :] = alpha * l_scr[h:h + 1, :] + pv[HEAD_DIM:HEAD_DIM + 1]

    def stage(t_next, next_bufs, t_cur, cur_bufs):
        k_next = pl.multiple_of(jnp.minimum(t_next, 2 * n_pairs - 1) * KT, KT)
        neg = score_scr[pl.ds(k_next, KT), :]
        kb_t = kb_ref[pl.ds(k_next, KT), :]
        if t_cur is not None:
            k_cur = pl.multiple_of(t_cur * KT, KT)
            v_ext = [jnp.concatenate([vbt_ref[g * HEAD_DIM:(g + 1) * HEAD_DIM, pl.ds(k_cur, KT)], ones_rows],
                                     axis=0) for g in range(N_KV_B)]
        for h in range(N_HEADS_B):
            logits_head(h, neg, kb_t, *next_bufs)
            if t_cur is not None:
                softmax_pv_head(h, v_ext, *cur_bufs)

    bufs_a, bufs_b = (s_scr, tmax_scr), (s2_scr, tmax2_scr)
    stage(0, bufs_a, None, None)

    def attn_pair(i, carry):
        t = 2 * i
        stage(t + 1, bufs_b, t, bufs_a)
        stage(t + 2, bufs_a, t + 1, bufs_b)
        return carry

    lax.fori_loop(0, n_pairs, attn_pair, 0)

    for h in range(N_HEADS_B):
        r0 = h * HEAD_DIM
        o_ref[r0:r0 + HEAD_DIM, :] = (acc_scr[r0:r0 + HEAD_DIM, :] / l_scr[h:h + 1, :]).astype(BF16)


def _mixer_b(qit, wit, qbt, ki, kb, vbt):
    B, _, S = qbt.shape
    topk = min(TOPK_MAX, S // 4)
    qspec = lambda w: pl.BlockSpec((None, w, QB), lambda b, j: (b, 0, j))
    return pl.pallas_call(
        functools.partial(_mixer_b_kernel, topk=float(topk)),
        grid=(B, S // QB),
        in_specs=[qspec(WIDTH_IDX), qspec(WI_ROWS), qspec(WIDTH_B),
                  pl.BlockSpec((None, S, LANES), lambda b, j: (b, 0, 0)),
                  pl.BlockSpec((None, S, LANES), lambda b, j: (b, 0, 0)),
                  pl.BlockSpec((None, WIDTH_KV_B, S), lambda b, j: (b, 0, 0))],
        out_specs=qspec(WIDTH_B),
        out_shape=jax.ShapeDtypeStruct((B, WIDTH_B, S), BF16),
        scratch_shapes=[pltpu.VMEM((S + KT, QB), F32),
                        pltpu.VMEM((S + KT, QB), BF16),
                        pltpu.VMEM((N_HEADS_B, KT, QB), F32),
                        pltpu.VMEM((N_HEADS_B, KT, QB), F32),
                        pltpu.VMEM((N_HEADS_B, QB), F32),
                        pltpu.VMEM((N_HEADS_B, QB), F32),
                        pltpu.VMEM((N_HEADS_B, 2 * HEAD_DIM, QB), BF16),
                        pltpu.VMEM((WIDTH_B, QB), F32),
                        pltpu.VMEM((N_HEADS_B, QB), F32),
                        pltpu.VMEM((N_HEADS_B, QB), F32),
                        pltpu.VMEM((1, QB), F32),
                        pltpu.VMEM((1, QB), F32),
                        pltpu.SMEM((1,), jnp.int32)],
        compiler_params=pltpu.CompilerParams(
            dimension_semantics=("parallel", "arbitrary"), vmem_limit_bytes=VMEM_LIMIT),
        name="mixer_b",
    )(qit, wit, qbt, ki, kb, vbt)


def _merge_kernel(x_ref, oat_ref, obt_ref, ga_ref, gb_ref, wa_ref, wb_ref, wo_ref, o_ref):
    ya = _dot_tn(oat_ref[...], wa_ref[...])
    yb = _dot_tn(obt_ref[...], wb_ref[...])
    merged = jax.nn.sigmoid(ga_ref[...]) * ya + jax.nn.sigmoid(gb_ref[...]) * yb
    o_ref[...] = x_ref[...] + _dot(merged.astype(BF16), wo_ref[...])


def _merge(x3d, oat, obt, ga, gb, wa, wb, wo):
    B, S, _ = x3d.shape
    tm = PROJ_TM
    tmaj = pl.BlockSpec((None, tm, D_MODEL), lambda b, s: (b, s, 0))
    fmaj = lambda w: pl.BlockSpec((None, w, tm), lambda b, s: (b, 0, s))
    full = lambda a: pl.BlockSpec(a.shape, lambda b, s: (0, 0))
    return pl.pallas_call(
        _merge_kernel,
        grid=(B, S // tm),
        in_specs=[tmaj, fmaj(WIDTH_A), fmaj(WIDTH_B), tmaj, tmaj, full(wa), full(wb), full(wo)],
        out_specs=tmaj,
        out_shape=jax.ShapeDtypeStruct((B, S, D_MODEL), F32),
        compiler_params=pltpu.CompilerParams(
            dimension_semantics=("parallel", "parallel"), vmem_limit_bytes=VMEM_LIMIT),
        name="merge",
    )(x3d, oat, obt, ga, gb, wa, wb, wo)


def _rope_tables(seq):
    inv_freq = jnp.power(jnp.float32(ROPE_THETA), -jnp.arange(0, ROT_DIM, 2, dtype=F32) / ROT_DIM)
    ang = jnp.arange(seq, dtype=F32)[:, None] * inv_freq[None, :]
    cos, sin = jnp.cos(ang), jnp.sin(ang)
    ones = jnp.ones((seq, HEAD_DIM - ROT_DIM), F32)
    zeros = jnp.zeros((seq, HEAD_DIM - ROT_DIM), F32)
    zh = jnp.zeros((seq, ROT_HALF), F32)
    c = jnp.concatenate([cos, cos, ones], axis=1)
    sa = jnp.concatenate([-sin, zh, zeros], axis=1)
    sb = jnp.concatenate([zh, sin, zeros], axis=1)
    rep = LANES // HEAD_DIM
    return cos.T, sin.T, jnp.tile(c, (1, rep)), jnp.tile(sa, (1, rep)), jnp.tile(sb, (1, rep))


def _band_bias_ext(rel_bias):
    n_keys = BAND_BLOCKS * QB
    n_h = rel_bias.shape[0]
    lo = jnp.broadcast_to(rel_bias[:, :1], (n_h, QB - 1 - REL_CLIP))
    hi = jnp.broadcast_to(rel_bias[:, -1:], (n_h, BIAS_SPAN - (QB - 1 - REL_CLIP) - (2 * REL_CLIP + 1)))
    ext = jnp.concatenate([lo, rel_bias, hi], axis=1)
    return jnp.roll(ext, -(n_keys - 1), axis=1).astype(F32)


def kernel(x, n1_g, ffn1_w_in, ffn1_w_out, n2_g, w_in, rel_bias, w_branch_a, w_branch_b, w_out,
           n3_g, ffn2_w_in, ffn2_w_out, nf_g):
    B, S, D = x.shape
    depth = n1_g.shape[0]
    cos_t, sin_t, c_tab, sa_tab, sb_tab = _rope_tables(S)
    nf = nf_g.reshape(1, D)
    offs = np.cumsum([0, WIDTH_A, WIDTH_A, WIDTH_A, WIDTH_B, WIDTH_KV_B, WIDTH_KV_B,
                      WIDTH_IDX, IDX_DIM, N_IDX_HEADS, D_MODEL, D_MODEL])
    seg = lambda w, i: w[:, offs[i]:offs[i + 1]]

    for l in range(depth):
        w = w_in[l]
        wt = jnp.concatenate(
            [seg(w, 0), seg(w, 2), seg(w, 3), seg(w, 5), seg(w, 6),
             jnp.pad(seg(w, 8), ((0, 0), (0, WI_ROWS - N_IDX_HEADS)))], axis=1).T.astype(BF16)
        wk = jnp.concatenate(
            [seg(w, 1), seg(w, 4), jnp.pad(seg(w, 7), ((0, 0), (0, LANES - IDX_DIM)))], axis=1).astype(BF16)
        wg = jnp.concatenate([seg(w, 9), seg(w, 10)], axis=1).astype(BF16)

        x2d = _ffn(x.reshape(B * S, D), n1_g[l].reshape(1, D), ffn1_w_in[l].astype(BF16),
                   ffn1_w_out[l].astype(BF16), nf, False)
        x = x2d.reshape(B, S, D)
        (qat, vat, qbt, vbt, qit, wit, ka, kb, ki, ga, gb) = _inproj(
            x, n2_g[l].reshape(1, D), wt, wk, wg, cos_t, sin_t, c_tab, sa_tab, sb_tab)
        oat = _mixer_a(qat, ka, vat, _band_bias_ext(rel_bias[l]))
        obt = _mixer_b(qit, wit, qbt, ki, kb, vbt)
        x = _merge(x, oat, obt, ga, gb, w_branch_a[l].astype(BF16), w_branch_b[l].astype(BF16),
                   w_out[l].astype(BF16))
        last = l == depth - 1
        x2d = _ffn(x.reshape(B * S, D), n3_g[l].reshape(1, D), ffn2_w_in[l].astype(BF16),
                   ffn2_w_out[l].astype(BF16), nf, last)
        x = x2d.reshape(B, S, D)
    return x
```

```python
import functools

import jax
import jax.numpy as jnp
import numpy as np
from jax import lax
from jax.experimental import pallas as pl
from jax.experimental.pallas import tpu as pltpu

F32 = jnp.float32
BF16 = jnp.bfloat16

D_MODEL = 1024
D_FF = 2816
HEAD_DIM = 64
CHUNK = 64
N_PREV_CHUNKS = 8
N_HEADS_A = 8
REL_CLIP = 128
N_HEADS_B = 8
N_KV_B = 2
N_IDX_HEADS = 8
IDX_DIM = 64
TOPK_MAX = 256
ROPE_THETA = 500000.0
ROT_DIM = HEAD_DIM // 4
ROT_HALF = ROT_DIM // 2
EPS = 1e-6
WIDTH_A = N_HEADS_A * HEAD_DIM
WIDTH_B = N_HEADS_B * HEAD_DIM
WIDTH_KV_B = N_KV_B * HEAD_DIM
WIDTH_IDX = N_IDX_HEADS * IDX_DIM

LANES = 128
LOG2E = 1.4426950408889634
QK_SCALE = HEAD_DIM ** -0.5 * LOG2E
IDX_SCALE = IDX_DIM ** -0.5

VMEM_LIMIT = 56 * 1024 * 1024

FFN_TM = 512
FFN_TF = D_FF // 2
PROJ_TM = 512
QB = 256
KT = 256
BAND_BLOCKS = N_PREV_CHUNKS * CHUNK // QB + 1
BIAS_SPAN = (BAND_BLOCKS + 1) * QB
WI_ROWS = 16
ONES_ROWS = 16
NEG_BIG = -1e30
FINE_STEPS = 17
FINE_STEPS_FIRST = 11
FOLD_CHAINS = 8


def _dot(a, b):
    return jnp.dot(a, b, preferred_element_type=F32)


def _dot_nt(a, b):
    return lax.dot_general(a, b, (((1,), (1,)), ((), ())), preferred_element_type=F32)


def _dot_tn(a, b):
    return lax.dot_general(a, b, (((0,), (0,)), ((), ())), preferred_element_type=F32)


def _rmsnorm(x, g):
    ms = jnp.mean(x * x, axis=-1, keepdims=True)
    return x * lax.rsqrt(ms + EPS) * g


def _ffn_kernel(x_ref, g_ref, wg_ref, wu_ref, wo_ref, gf_ref, o_ref, h_scr, acc_scr, *, final_norm):
    j = pl.program_id(1)

    @pl.when(j == 0)
    def _():
        h_scr[...] = _rmsnorm(x_ref[...], g_ref[...]).astype(BF16)
        acc_scr[...] = jnp.zeros_like(acc_scr)

    h = h_scr[...]
    gate = _dot(h, wg_ref[...])
    up = _dot(h, wu_ref[...])
    a = (gate * jax.nn.sigmoid(gate) * up).astype(BF16)
    acc_scr[...] += _dot(a, wo_ref[...])

    @pl.when(j == pl.num_programs(1) - 1)
    def _():
        y = x_ref[...] + 0.5 * acc_scr[...]
        if final_norm:
            y = _rmsnorm(y, gf_ref[...])
        o_ref[...] = y


def _ffn(x2d, g, w_in_bf, w_out_bf, gf, final_norm):
    T = x2d.shape[0]
    nf = D_FF // FFN_TF
    return pl.pallas_call(
        functools.partial(_ffn_kernel, final_norm=final_norm),
        grid=(T // FFN_TM, nf),
        in_specs=[
            pl.BlockSpec((FFN_TM, D_MODEL), lambda i, j: (i, 0)),
            pl.BlockSpec((1, D_MODEL), lambda i, j: (0, 0)),
            pl.BlockSpec((D_MODEL, FFN_TF), lambda i, j: (0, j)),
            pl.BlockSpec((D_MODEL, FFN_TF), lambda i, j: (0, j + nf)),
            pl.BlockSpec((FFN_TF, D_MODEL), lambda i, j: (j, 0)),
            pl.BlockSpec((1, D_MODEL), lambda i, j: (0, 0)),
        ],
        out_specs=pl.BlockSpec((FFN_TM, D_MODEL), lambda i, j: (i, 0)),
        out_shape=jax.ShapeDtypeStruct((T, D_MODEL), F32),
        scratch_shapes=[pltpu.VMEM((FFN_TM, D_MODEL), BF16), pltpu.VMEM((FFN_TM, D_MODEL), F32)],
        compiler_params=pltpu.CompilerParams(
            dimension_semantics=("parallel", "arbitrary"), vmem_limit_bytes=VMEM_LIMIT),
        name="ffn_final" if final_norm else "ffn",
    )(x2d, g, w_in_bf, w_in_bf, w_out_bf, gf)


def _rope_rows(x, cos, sin, n_heads):
    pieces = []
    for h in range(n_heads):
        r0 = h * HEAD_DIM
        t1 = x[r0:r0 + ROT_HALF]
        t2 = x[r0 + ROT_HALF:r0 + ROT_DIM]
        pieces.append(t1 * cos - t2 * sin)
        pieces.append(t2 * cos + t1 * sin)
        pieces.append(x[r0 + ROT_DIM:r0 + HEAD_DIM])
    return jnp.concatenate(pieces, axis=0)


def _rope_lanes(x, c, sa, sb):
    return x * c + pltpu.roll(x, LANES - ROT_HALF, 1) * sa + pltpu.roll(x, ROT_HALF, 1) * sb


def _inproj_kernel(x_ref, g_ref, wt_ref, wk_ref, wg_ref, cos_ref, sin_ref, c_ref, sa_ref, sb_ref,
                   qat_ref, vat_ref, qbt_ref, vbt_ref, qit_ref, wit_ref,
                   ka_ref, kb_ref, ki_ref, ga_ref, gb_ref):
    h = _rmsnorm(x_ref[...], g_ref[...]).astype(BF16)
    cos = cos_ref[...]
    sin = sin_ref[...]

    r = 0
    t = _dot_nt(wt_ref[r:r + WIDTH_A, :], h)
    qat_ref[...] = (t * QK_SCALE).astype(BF16)
    r += WIDTH_A
    vat_ref[...] = _dot_nt(wt_ref[r:r + WIDTH_A, :], h).astype(BF16)
    r += WIDTH_A
    t = _dot_nt(wt_ref[r:r + WIDTH_B, :], h)
    qbt_ref[...] = (_rope_rows(t, cos, sin, N_HEADS_B) * QK_SCALE).astype(BF16)
    r += WIDTH_B
    vbt_ref[...] = _dot_nt(wt_ref[r:r + WIDTH_KV_B, :], h).astype(BF16)
    r += WIDTH_KV_B
    t = _dot_nt(wt_ref[r:r + WIDTH_IDX, :], h)
    qit_ref[...] = (_rope_rows(t, cos, sin, N_IDX_HEADS) * IDX_SCALE).astype(BF16)
    r += WIDTH_IDX
    wit_ref[...] = _dot_nt(wt_ref[r:r + WI_ROWS, :], h) * (N_IDX_HEADS ** -0.5)

    ka_ref[...] = _dot(h, wk_ref[:, 0:WIDTH_A]).astype(BF16)
    c, sa, sb = c_ref[...], sa_ref[...], sb_ref[...]
    t = _dot(h, wk_ref[:, WIDTH_A:WIDTH_A + LANES])
    kb_ref[...] = _rope_lanes(t, c, sa, sb).astype(BF16)
    t = _dot(h, wk_ref[:, WIDTH_A + LANES:WIDTH_A + 2 * LANES])
    ki_ref[...] = _rope_lanes(t, c, sa, sb).astype(BF16)

    ga_ref[...] = _dot(h, wg_ref[:, 0:D_MODEL])
    gb_ref[...] = _dot(h, wg_ref[:, D_MODEL:2 * D_MODEL])


def _inproj(x3d, g, wt, wk, wg, cos_t, sin_t, c_tab, sa_tab, sb_tab):
    B, S, _ = x3d.shape
    tm = PROJ_TM
    full = lambda shape: pl.BlockSpec(shape, lambda b, s: (0,) * len(shape))
    tmaj = lambda w: pl.BlockSpec((None, tm, w), lambda b, s: (b, s, 0))
    fmaj = lambda w: pl.BlockSpec((None, w, tm), lambda b, s: (b, 0, s))
    tshape = lambda w, dt: jax.ShapeDtypeStruct((B, S, w), dt)
    fshape = lambda w, dt: jax.ShapeDtypeStruct((B, w, S), dt)
    return pl.pallas_call(
        _inproj_kernel,
        grid=(B, S // tm),
        in_specs=[
            tmaj(D_MODEL), full((1, D_MODEL)), full(wt.shape), full(wk.shape), full(wg.shape),
            pl.BlockSpec((ROT_HALF, tm), lambda b, s: (0, s)),
            pl.BlockSpec((ROT_HALF, tm), lambda b, s: (0, s)),
            pl.BlockSpec((tm, LANES), lambda b, s: (s, 0)),
            pl.BlockSpec((tm, LANES), lambda b, s: (s, 0)),
            pl.BlockSpec((tm, LANES), lambda b, s: (s, 0)),
        ],
        out_specs=[fmaj(WIDTH_A), fmaj(WIDTH_A), fmaj(WIDTH_B), fmaj(WIDTH_KV_B), fmaj(WIDTH_IDX),
                   fmaj(WI_ROWS), tmaj(WIDTH_A), tmaj(LANES), tmaj(LANES), tmaj(D_MODEL), tmaj(D_MODEL)],
        out_shape=[fshape(WIDTH_A, BF16), fshape(WIDTH_A, BF16), fshape(WIDTH_B, BF16),
                   fshape(WIDTH_KV_B, BF16), fshape(WIDTH_IDX, BF16), fshape(WI_ROWS, F32),
                   tshape(WIDTH_A, BF16), tshape(LANES, BF16), tshape(LANES, BF16),
                   tshape(D_MODEL, F32), tshape(D_MODEL, F32)],
        compiler_params=pltpu.CompilerParams(
            dimension_semantics=("parallel", "parallel"), vmem_limit_bytes=VMEM_LIMIT),
        name="inproj",
    )(x3d, g, wt, wk, wg, cos_t, sin_t, c_tab, sa_tab, sb_tab)


def _padded_rhs(qt_h, slot):
    z = jnp.zeros_like(qt_h)
    return jnp.concatenate([qt_h, z] if slot == 0 else [z, qt_h], axis=0)


def _mixer_a_kernel(qt_ref, k0_ref, k1_ref, k2_ref, v0_ref, v1_ref, v2_ref, ext_ref, o_ref,
                    bias_scr, s_scr):
    jb = pl.program_id(1)
    k_refs = (k0_ref, k1_ref, k2_ref)
    v_refs = (v0_ref, v1_ref, v2_ref)

    @pl.when((pl.program_id(0) == 0) & (jb == 0))
    def _():
        qc = lax.broadcasted_iota(jnp.int32, (QB, QB), 1) // CHUNK
        for m in range(BAND_BLOCKS):
            kc = lax.broadcasted_iota(jnp.int32, (QB, QB), 0) // CHUNK + m * (QB // CHUNK)
            valid = (kc >= qc) & (kc <= qc + N_PREV_CHUNKS)
            for h in range(N_HEADS_A):
                rows = jnp.broadcast_to(ext_ref[h:h + 1, :], (QB, BIAS_SPAN))
                skew = pltpu.roll(rows, m * QB, 1, stride=1, stride_axis=0)
                bias_scr[h, m * QB:(m + 1) * QB, :] = jnp.where(valid, skew[:, 0:QB] * LOG2E, -jnp.inf)

    ones_rows = jnp.ones((ONES_ROWS, QB), BF16)

    def logits_block(h, m, rhs):
        lane0 = (h // 2) * LANES
        sm = _dot(k_refs[m][:, lane0:lane0 + LANES], rhs) + bias_scr[h, m * QB:(m + 1) * QB, :]
        if m < BAND_BLOCKS - 1:
            sm = jnp.where(jb + m >= BAND_BLOCKS - 1, sm, -jnp.inf)
        s_scr[h % 2, m] = sm
        return sm.max(axis=0, keepdims=True)

    def pv_block(h, m, mx):
        r0 = h * HEAD_DIM
        p = jnp.exp2((s_scr[h % 2, m] - mx).astype(BF16))
        v_ext = jnp.concatenate([v_refs[m][r0:r0 + HEAD_DIM, :], ones_rows], axis=0)
        return _dot(v_ext, p)

    def head_rhs(h):
        return _padded_rhs(qt_ref[h * HEAD_DIM:(h + 1) * HEAD_DIM, :], h % 2)

    rhs = head_rhs(0)
    mx = functools.reduce(jnp.maximum, [logits_block(0, m, rhs) for m in range(BAND_BLOCKS)])
    for h in range(N_HEADS_A):
        r0 = h * HEAD_DIM
        acc = jnp.zeros((HEAD_DIM + ONES_ROWS, QB), F32)
        next_max = []
        if h + 1 < N_HEADS_A:
            rhs = head_rhs(h + 1)
        for m in range(BAND_BLOCKS):
            if h + 1 < N_HEADS_A:
                next_max.append(logits_block(h + 1, m, rhs))
            acc = acc + pv_block(h, m, mx)
        o_ref[r0:r0 + HEAD_DIM, :] = (acc[0:HEAD_DIM] / acc[HEAD_DIM:HEAD_DIM + 1]).astype(BF16)
        if next_max:
            mx = functools.reduce(jnp.maximum, next_max)


def _mixer_a(qat, ka, vat, bias_ext):
    B, _, S = qat.shape
    assert S % QB == 0 and QB % CHUNK == 0 and (N_PREV_CHUNKS * CHUNK) % QB == 0
    assert QB - 1 >= REL_CLIP
    nb = BAND_BLOCKS - 1
    kspec = lambda m: pl.BlockSpec((None, QB, WIDTH_A), lambda b, j: (b, jnp.maximum(j + m - nb, 0), 0))
    vspec = lambda m: pl.BlockSpec((None, WIDTH_A, QB), lambda b, j: (b, 0, jnp.maximum(j + m - nb, 0)))
    return pl.pallas_call(
        _mixer_a_kernel,
        grid=(B, S // QB),
        in_specs=[pl.BlockSpec((None, WIDTH_A, QB), lambda b, j: (b, 0, j)),
                  kspec(0), kspec(1), kspec(2), vspec(0), vspec(1), vspec(2),
                  pl.BlockSpec(bias_ext.shape, lambda b, j: (0, 0))],
        out_specs=pl.BlockSpec((None, WIDTH_A, QB), lambda b, j: (b, 0, j)),
        out_shape=jax.ShapeDtypeStruct((B, WIDTH_A, S), BF16),
        scratch_shapes=[pltpu.VMEM((N_HEADS_A, BAND_BLOCKS * QB, QB), F32),
                        pltpu.VMEM((2, BAND_BLOCKS, QB, QB), F32)],
        compiler_params=pltpu.CompilerParams(
            dimension_semantics=("arbitrary", "arbitrary"), vmem_limit_bytes=VMEM_LIMIT),
        name="mixer_a",
    )(qat, ka, ka, ka, vat, vat, vat, bias_ext)


def _ordered_code_to_f32(u):
    bits = jnp.where(u < 0, u ^ jnp.int32(-2 ** 31), ~u)
    return lax.bitcast_convert_type(bits, F32)


def _mixer_b_kernel(qit_ref, wit_ref, qbt_ref, ki_ref, kb_ref, vbt_ref, o_ref,
                    score_scr, sb_scr, s_scr, s2_scr, tmax_scr, tmax2_scr, rhs_scr, acc_scr, m_scr, l_scr,
                    thr_scr, need_scr, ties_scr, *, topk):
    jb = pl.program_id(1)
    n_tiles = jb + 1
    diag0 = pl.multiple_of(jb * KT, KT)
    q_chunk = lax.broadcasted_iota(jnp.int32, (1, QB), 1) // CHUNK
    adm_diag = lax.broadcasted_iota(jnp.int32, (KT, QB), 0) < (q_chunk + 1) * CHUNK
    zero_rows = jnp.zeros((HEAD_DIM, QB), BF16)

    def tile_loop(body, init):
        def wrapped(t, carry):
            return body(pl.multiple_of(t * KT, KT), carry)
        return lax.fori_loop(0, n_tiles, wrapped, init)

    def col_count(hit):
        return hit.reshape(KT // 8, 8, QB).sum(axis=0)

    for h in range(N_IDX_HEADS):
        rhs_scr[h, 0:IDX_DIM, :] = qit_ref[h * IDX_DIM:(h + 1) * IDX_DIM, :]
        rhs_scr[h, IDX_DIM:, :] = zero_rows
    w = wit_ref[...]

    n_pairs = (n_tiles + 1) // 2
    pad0 = pl.multiple_of(n_tiles * KT, KT)

    def pair_loop(body, init):
        def wrapped(t, carry):
            return body(pl.multiple_of(t * 2 * KT, 2 * KT), carry)
        return lax.fori_loop(0, n_pairs, wrapped, init)

    def score_pair(k0, carry):
        ki_a = ki_ref[pl.ds(k0, KT), :]
        ki_b = ki_ref[pl.ds(k0 + KT, KT), :]
        acc_a = jnp.zeros((KT, QB), F32)
        acc_b = jnp.zeros((KT, QB), F32)
        for h in range(N_IDX_HEADS):
            acc_a = acc_a + w[h:h + 1, :] * jnp.maximum(_dot(ki_a, rhs_scr[h]), 0.0)
            acc_b = acc_b + w[h:h + 1, :] * jnp.maximum(_dot(ki_b, rhs_scr[h]), 0.0)
        score_scr[pl.ds(k0, KT), :] = acc_a
        score_scr[pl.ds(k0 + KT, KT), :] = acc_b
        return carry

    pair_loop(score_pair, 0)
    score_scr[pl.ds(diag0, KT), :] = jnp.where(adm_diag, score_scr[pl.ds(diag0, KT), :], -jnp.inf)
    score_scr[pl.ds(pad0, KT), :] = jnp.full((KT, QB), -jnp.inf, F32)

    def round_pair(k0, carry):
        sb_scr[pl.ds(k0, 2 * KT), :] = score_scr[pl.ds(k0, 2 * KT), :].astype(BF16)
        return carry

    pair_loop(round_pair, 0)

    def count_pair(ref, k0, cand, rows, one, zero):
        chains = [None] * FOLD_CHAINS
        view = ref.at[pl.ds(k0, 2 * KT), :]
        for n, r in enumerate(range(0, 2 * KT, rows)):
            hit = jnp.where(view[r:r + rows, :] >= cand, one, zero)
            c = n % FOLD_CHAINS
            chains[c] = hit if chains[c] is None else chains[c] + hit
        while len(chains) > 1:
            chains = [a + b for a, b in zip(chains[0::2], chains[1::2])]
        return chains[0]

    def count_ge_bf16(cand):
        cand = jnp.broadcast_to(cand.astype(BF16), (16, QB))

        def body(k0, cnt):
            return cnt + count_pair(sb_scr, k0, cand, 16, jnp.int16(1), jnp.int16(0))
        cnt = pair_loop(body, jnp.zeros((16, QB), jnp.int16))
        return cnt.astype(jnp.int32).sum(axis=0, keepdims=True)

    def count_ge_f32(cand):
        cand = jnp.broadcast_to(cand, (8, QB))

        def body(k0, cnt):
            return cnt + count_pair(score_scr, k0, cand, 8, 1.0, 0.0)
        cnt = pair_loop(body, jnp.zeros((8, QB), F32))
        return cnt.sum(axis=0, keepdims=True).astype(jnp.int32)

    k_int = jnp.int32(int(topk))
    neg_inf_code = jnp.int32(0x007FFFFF)

    def accept(cnt, code):
        return (cnt >= k_int) | ((code >= 0) & (code <= neg_inf_code))

    def coarse_step(i, t_u):
        cand_u = t_u | lax.shift_left(jnp.int32(1), 31 - i)
        cnt = count_ge_bf16(_ordered_code_to_f32(cand_u))
        return jnp.where(accept(cnt, cand_u), cand_u, t_u)

    t_coarse = lax.fori_loop(0, 16, coarse_step, jnp.zeros((1, QB), jnp.int32))
    base = t_coarse - jnp.int32(1 << 16)

    def fine_step(i, carry):
        off, cnt_at = carry
        cand_off = off | lax.shift_left(jnp.int32(1), 16 - i)
        cand_u = base + cand_off
        cnt = count_ge_f32(_ordered_code_to_f32(cand_u))
        ok = accept(cnt, cand_u)
        return jnp.where(ok, cand_off, off), jnp.where(ok, cnt, cnt_at)

    off, cnt_at = lax.fori_loop(0, FINE_STEPS_FIRST, fine_step,
                                (jnp.zeros((1, QB), jnp.int32), jnp.full((1, QB), -1, jnp.int32)))
    thr_scr[...] = _ordered_code_to_f32(base + off)
    ties_scr[0] = jnp.int32(0)
    unsettled = jnp.max(jnp.where(cnt_at == k_int, 0.0, 1.0)) > 0.0

    @pl.when(unsettled)
    def _():
        off_full, _ = lax.fori_loop(FINE_STEPS_FIRST, FINE_STEPS, fine_step, (off, cnt_at))
        thr_full = _ordered_code_to_f32(base + off_full)
        thr_scr[...] = thr_full

        def count_both(k0, carry):
            ge, gt = carry
            sc = score_scr[pl.ds(k0, KT), :]
            return (ge + col_count(jnp.where(sc >= thr_full, 1.0, 0.0)),
                    gt + col_count(jnp.where(sc > thr_full, 1.0, 0.0)))

        ge8, gt8 = tile_loop(count_both, (jnp.zeros((8, QB), F32), jnp.zeros((8, QB), F32)))
        cnt_ge = ge8.sum(axis=0, keepdims=True)
        need_scr[...] = topk - gt8.sum(axis=0, keepdims=True)
        n_inadm = (KT - (q_chunk + 1) * CHUNK).astype(F32)
        n_sel = cnt_ge - jnp.where(thr_full == -jnp.inf, n_inadm, 0.0)
        ties_scr[0] = (jnp.max(jnp.where(n_sel > topk, 1.0, 0.0)) > 0.0).astype(jnp.int32)

    thr = thr_scr[...]
    has_ties = ties_scr[0] > 0

    @pl.when(jnp.logical_not(has_ties))
    def _():
        def body(k0, carry):
            score_scr[pl.ds(k0, KT), :] = jnp.where(score_scr[pl.ds(k0, KT), :] >= thr, 0.0, -jnp.inf)
            return carry
        tile_loop(body, 0)

    @pl.when(has_ties)
    def _():
        tri = jnp.where(lax.broadcasted_iota(jnp.int32, (KT, KT), 1)
                        < lax.broadcasted_iota(jnp.int32, (KT, KT), 0), 1.0, 0.0).astype(BF16)

        def body(k0, need):
            sc = score_scr[pl.ds(k0, KT), :]
            eq_f = jnp.where(sc == thr, 1.0, 0.0)
            before = _dot(tri, eq_f.astype(BF16))
            take = jnp.where(sc > thr, 1.0, jnp.where(before < need, eq_f, 0.0))
            score_scr[pl.ds(k0, KT), :] = jnp.where(take > 0.0, 0.0, -jnp.inf)
            return need - eq_f.sum(axis=0, keepdims=True)
        tile_loop(body, need_scr[...])

    score_scr[pl.ds(diag0, KT), :] = jnp.where(adm_diag, score_scr[pl.ds(diag0, KT), :], -jnp.inf)

    for h in range(N_HEADS_B):
        g = h // (N_HEADS_B // N_KV_B)
        q_h = qbt_ref[h * HEAD_DIM:(h + 1) * HEAD_DIM, :]
        rhs_scr[h, 0:HEAD_DIM, :] = q_h if g == 0 else zero_rows
        rhs_scr[h, HEAD_DIM:, :] = zero_rows if g == 0 else q_h
    m_scr[...] = jnp.full_like(m_scr, NEG_BIG)
    l_scr[...] = jnp.zeros_like(l_scr)
    acc_scr[...] = jnp.zeros_like(acc_scr)
    ones_rows = jnp.ones((ONES_ROWS, KT), BF16)

    def logits_head(h, neg, kb_t, s_buf, tmax_buf):
        s = _dot(kb_t, rhs_scr[h]) + neg
        s_buf[h] = s
        tmax_buf[h:h + 1, :] = s.max(axis=0, keepdims=True)

    def softmax_pv_head(h, v_ext, s_buf, tmax_buf):
        g = h // (N_HEADS_B // N_KV_B)
        m_old = m_scr[h:h + 1, :]
        m_new = jnp.maximum(m_old, tmax_buf[h:h + 1, :])
        m_scr[h:h + 1, :] = m_new
        alpha = jnp.exp2(m_old - m_new)
        p = jnp.exp2((s_buf[h] - m_new).astype(BF16))
        pv = _dot(v_ext[g], p)
        r0 = h * HEAD_DIM
        acc_scr[r0:r0 + HEAD_DIM, :] = alpha * acc_scr[r0:r0 + HEAD_DIM, :] + pv[0:HEAD_DIM]
        l_scr[h:h + 1, :] = alpha * l_scr[h:h + 1, :] + pv[HEAD_DIM:HEAD_DIM + 1]

    def stage(t_next, next_bufs, t_cur, cur_bufs):
        k_next = pl.multiple_of(jnp.minimum(t_next, 2 * n_pairs - 1) * KT, KT)
        neg = score_scr[pl.ds(k_next, KT), :]
        kb_t = kb_ref[pl.ds(k_next, KT), :]
        if t_cur is not None:
            k_cur = pl.multiple_of(t_cur * KT, KT)
            v_ext = [jnp.concatenate([vbt_ref[g * HEAD_DIM:(g + 1) * HEAD_DIM, pl.ds(k_cur, KT)], ones_rows],
                                     axis=0) for g in range(N_KV_B)]
        for h in range(N_HEADS_B):
            logits_head(h, neg, kb_t, *next_bufs)
            if t_cur is not None:
                softmax_pv_head(h, v_ext, *cur_bufs)

    bufs_a, bufs_b = (s_scr, tmax_scr), (s2_scr, tmax2_scr)
    stage(0, bufs_a, None, None)

    def attn_pair(i, carry):
        t = 2 * i
        stage(t + 1, bufs_b, t, bufs_a)
        stage(t + 2, bufs_a, t + 1, bufs_b)
        return carry

    lax.fori_loop(0, n_pairs, attn_pair, 0)

    for h in range(N_HEADS_B):
        r0 = h * HEAD_DIM
        o_ref[r0:r0 + HEAD_DIM, :] = (acc_scr[r0:r0 + HEAD_DIM, :] / l_scr[h:h + 1, :]).astype(BF16)


def _mixer_b(qit, wit, qbt, ki, kb, vbt):
    B, _, S = qbt.shape
    topk = min(TOPK_MAX, S // 4)
    assert KT == QB and S % QB == 0 and QB % CHUNK == 0 and topk <= KT
    qspec = lambda w: pl.BlockSpec((None, w, QB), lambda b, j: (b, 0, j))
    return pl.pallas_call(
        functools.partial(_mixer_b_kernel, topk=float(topk)),
        grid=(B, S // QB),
        in_specs=[qspec(WIDTH_IDX), qspec(WI_ROWS), qspec(WIDTH_B),
                  pl.BlockSpec((None, S, LANES), lambda b, j: (b, 0, 0)),
                  pl.BlockSpec((None, S, LANES), lambda b, j: (b, 0, 0)),
                  pl.BlockSpec((None, WIDTH_KV_B, S), lambda b, j: (b, 0, 0))],
        out_specs=qspec(WIDTH_B),
        out_shape=jax.ShapeDtypeStruct((B, WIDTH_B, S), BF16),
        scratch_shapes=[pltpu.VMEM((S + KT, QB), F32),
                        pltpu.VMEM((S + KT, QB), BF16),
                        pltpu.VMEM((N_HEADS_B, KT, QB), F32),
                        pltpu.VMEM((N_HEADS_B, KT, QB), F32),
                        pltpu.VMEM((N_HEADS_B, QB), F32),
                        pltpu.VMEM((N_HEADS_B, QB), F32),
                        pltpu.VMEM((N_HEADS_B, 2 * HEAD_DIM, QB), BF16),
                        pltpu.VMEM((WIDTH_B, QB), F32),
                        pltpu.VMEM((N_HEADS_B, QB), F32),
                        pltpu.VMEM((N_HEADS_B, QB), F32),
                        pltpu.VMEM((1, QB), F32),
                        pltpu.VMEM((1, QB), F32),
                        pltpu.SMEM((1,), jnp.int32)],
        compiler_params=pltpu.CompilerParams(
            dimension_semantics=("parallel", "arbitrary"), vmem_limit_bytes=VMEM_LIMIT),
        name="mixer_b",
    )(qit, wit, qbt, ki, kb, vbt)


def _merge_kernel(x_ref, oat_ref, obt_ref, ga_ref, gb_ref, wa_ref, wb_ref, wo_ref, o_ref):
    ya = _dot_tn(oat_ref[...], wa_ref[...])
    yb = _dot_tn(obt_ref[...], wb_ref[...])
    merged = jax.nn.sigmoid(ga_ref[...]) * ya + jax.nn.sigmoid(gb_ref[...]) * yb
    o_ref[...] = x_ref[...] + _dot(merged.astype(BF16), wo_ref[...])


def _merge(x3d, oat, obt, ga, gb, wa, wb, wo):
    B, S, _ = x3d.shape
    tm = PROJ_TM
    tmaj = pl.BlockSpec((None, tm, D_MODEL), lambda b, s: (b, s, 0))
    fmaj = lambda w: pl.BlockSpec((None, w, tm), lambda b, s: (b, 0, s))
    full = lambda a: pl.BlockSpec(a.shape, lambda b, s: (0, 0))
    return pl.pallas_call(
        _merge_kernel,
        grid=(B, S // tm),
        in_specs=[tmaj, fmaj(WIDTH_A), fmaj(WIDTH_B), tmaj, tmaj, full(wa), full(wb), full(wo)],
        out_specs=tmaj,
        out_shape=jax.ShapeDtypeStruct((B, S, D_MODEL), F32),
        compiler_params=pltpu.CompilerParams(
            dimension_semantics=("parallel", "parallel"), vmem_limit_bytes=VMEM_LIMIT),
        name="merge",
    )(x3d, oat, obt, ga, gb, wa, wb, wo)


def _rope_tables(seq):
    inv_freq = jnp.power(jnp.float32(ROPE_THETA), -jnp.arange(0, ROT_DIM, 2, dtype=F32) / ROT_DIM)
    ang = jnp.arange(seq, dtype=F32)[:, None] * inv_freq[None, :]
    cos, sin = jnp.cos(ang), jnp.sin(ang)
    ones = jnp.ones((seq, HEAD_DIM - ROT_DIM), F32)
    zeros = jnp.zeros((seq, HEAD_DIM - ROT_DIM), F32)
    zh = jnp.zeros((seq, ROT_HALF), F32)
    c = jnp.concatenate([cos, cos, ones], axis=1)
    sa = jnp.concatenate([-sin, zh, zeros], axis=1)
    sb = jnp.concatenate([zh, sin, zeros], axis=1)
    rep = LANES // HEAD_DIM
    return cos.T, sin.T, jnp.tile(c, (1, rep)), jnp.tile(sa, (1, rep)), jnp.tile(sb, (1, rep))


def _band_bias_ext(rel_bias):
    n_keys = BAND_BLOCKS * QB
    n_h = rel_bias.shape[0]
    lo = jnp.broadcast_to(rel_bias[:, :1], (n_h, QB - 1 - REL_CLIP))
    hi = jnp.broadcast_to(rel_bias[:, -1:], (n_h, BIAS_SPAN - (QB - 1 - REL_CLIP) - (2 * REL_CLIP + 1)))
    ext = jnp.concatenate([lo, rel_bias, hi], axis=1)
    return jnp.roll(ext, -(n_keys - 1), axis=1).astype(F32)


def kernel(x, n1_g, ffn1_w_in, ffn1_w_out, n2_g, w_in, rel_bias, w_branch_a, w_branch_b, w_out,
           n3_g, ffn2_w_in, ffn2_w_out, nf_g):
    B, S, D = x.shape
    depth = n1_g.shape[0]
    cos_t, sin_t, c_tab, sa_tab, sb_tab = _rope_tables(S)
    nf = nf_g.reshape(1, D)
    offs = np.cumsum([0, WIDTH_A, WIDTH_A, WIDTH_A, WIDTH_B, WIDTH_KV_B, WIDTH_KV_B,
                      WIDTH_IDX, IDX_DIM, N_IDX_HEADS, D_MODEL, D_MODEL])
    seg = lambda w, i: w[:, offs[i]:offs[i + 1]]

    for l in range(depth):
        w = w_in[l]
        wt = jnp.concatenate(
            [seg(w, 0), seg(w, 2), seg(w, 3), seg(w, 5), seg(w, 6),
             jnp.pad(seg(w, 8), ((0, 0), (0, WI_ROWS - N_IDX_HEADS)))], axis=1).T.astype(BF16)
        wk = jnp.concatenate(
            [seg(w, 1), seg(w, 4), jnp.pad(seg(w, 7), ((0, 0), (0, LANES - IDX_DIM)))], axis=1).astype(BF16)
        wg = jnp.concatenate([seg(w, 9), seg(w, 10)], axis=1).astype(BF16)

        x2d = _ffn(x.reshape(B * S, D), n1_g[l].reshape(1, D), ffn1_w_in[l].astype(BF16),
                   ffn1_w_out[l].astype(BF16), nf, False)
        x = x2d.reshape(B, S, D)
        (qat, vat, qbt, vbt, qit, wit, ka, kb, ki, ga, gb) = _inproj(
            x, n2_g[l].reshape(1, D), wt, wk, wg, cos_t, sin_t, c_tab, sa_tab, sb_tab)
        oat = _mixer_a(qat, ka, vat, _band_bias_ext(rel_bias[l]))
        obt = _mixer_b(qit, wit, qbt, ki, kb, vbt)
        x = _merge(x, oat, obt, ga, gb, w_branch_a[l].astype(BF16), w_branch_b[l].astype(BF16),
                   w_out[l].astype(BF16))
        last = l == depth - 1
        x2d = _ffn(x.reshape(B * S, D), n3_g[l].reshape(1, D), ffn2_w_in[l].astype(BF16),
                   ffn2_w_out[l].astype(BF16), nf, last)
        x = x2d.reshape(B, S, D)
    return x
```

```python
import functools

import jax
import jax.numpy as jnp
import numpy as np
from jax import lax
from jax.experimental import pallas as pl
from jax.experimental.pallas import tpu as pltpu

F32 = jnp.float32
BF16 = jnp.bfloat16

D_MODEL = 1024
D_FF = 2816
HEAD_DIM = 64
CHUNK = 64
N_PREV_CHUNKS = 8
N_HEADS_A = 8
REL_CLIP = 128
N_HEADS_B = 8
N_KV_B = 2
N_IDX_HEADS = 8
IDX_DIM = 64
TOPK_MAX = 256
ROPE_THETA = 500000.0
ROT_DIM = HEAD_DIM // 4
ROT_HALF = ROT_DIM // 2
EPS = 1e-6
WIDTH_A = N_HEADS_A * HEAD_DIM
WIDTH_B = N_HEADS_B * HEAD_DIM
WIDTH_KV_B = N_KV_B * HEAD_DIM
WIDTH_IDX = N_IDX_HEADS * IDX_DIM

LANES = 128
LOG2E = 1.4426950408889634
QK_SCALE = HEAD_DIM ** -0.5 * LOG2E
IDX_SCALE = IDX_DIM ** -0.5

VMEM_LIMIT = 56 * 1024 * 1024

FFN_TM = 512
FFN_TF = D_FF // 2
PROJ_TM = 512
QB = 256
KT = 256
BAND_BLOCKS = N_PREV_CHUNKS * CHUNK // QB + 1
BIAS_SPAN = (BAND_BLOCKS + 1) * QB
WI_ROWS = 16
ONES_ROWS = 16
NEG_BIG = -1e30
FINE_STEPS = 17
FINE_STEPS_FIRST = 11
FOLD_CHAINS = 8


def _dot(a, b):
    return jnp.dot(a, b, preferred_element_type=F32)


def _dot_nt(a, b):
    return lax.dot_general(a, b, (((1,), (1,)), ((), ())), preferred_element_type=F32)


def _dot_tn(a, b):
    return lax.dot_general(a, b, (((0,), (0,)), ((), ())), preferred_element_type=F32)


def _rmsnorm(x, g):
    ms = jnp.mean(x * x, axis=-1, keepdims=True)
    return x * lax.rsqrt(ms + EPS) * g


def _ffn_kernel(x_ref, g_ref, wg_ref, wu_ref, wo_ref, gf_ref, o_ref, h_scr, acc_scr, *, final_norm):
    j = pl.program_id(1)

    @pl.when(j == 0)
    def _():
        h_scr[...] = _rmsnorm(x_ref[...], g_ref[...]).astype(BF16)
        acc_scr[...] = jnp.zeros_like(acc_scr)

    h = h_scr[...]
    gate = _dot(h, wg_ref[...])
    up = _dot(h, wu_ref[...])
    a = (gate * jax.nn.sigmoid(gate) * up).astype(BF16)
    acc_scr[...] += _dot(a, wo_ref[...])

    @pl.when(j == pl.num_programs(1) - 1)
    def _():
        y = x_ref[...] + 0.5 * acc_scr[...]
        if final_norm:
            y = _rmsnorm(y, gf_ref[...])
        o_ref[...] = y


def _ffn(x2d, g, w_in_bf, w_out_bf, gf, final_norm):
    T = x2d.shape[0]
    nf = D_FF // FFN_TF
    return pl.pallas_call(
        functools.partial(_ffn_kernel, final_norm=final_norm),
        grid=(T // FFN_TM, nf),
        in_specs=[
            pl.BlockSpec((FFN_TM, D_MODEL), lambda i, j: (i, 0)),
            pl.BlockSpec((1, D_MODEL), lambda i, j: (0, 0)),
            pl.BlockSpec((D_MODEL, FFN_TF), lambda i, j: (0, j)),
            pl.BlockSpec((D_MODEL, FFN_TF), lambda i, j: (0, j + nf)),
            pl.BlockSpec((FFN_TF, D_MODEL), lambda i, j: (j, 0)),
            pl.BlockSpec((1, D_MODEL), lambda i, j: (0, 0)),
        ],
        out_specs=pl.BlockSpec((FFN_TM, D_MODEL), lambda i, j: (i, 0)),
        out_shape=jax.ShapeDtypeStruct((T, D_MODEL), F32),
        scratch_shapes=[pltpu.VMEM((FFN_TM, D_MODEL), BF16), pltpu.VMEM((FFN_TM, D_MODEL), F32)],
        compiler_params=pltpu.CompilerParams(
            dimension_semantics=("parallel", "arbitrary"), vmem_limit_bytes=VMEM_LIMIT),
        name="ffn_final" if final_norm else "ffn",
    )(x2d, g, w_in_bf, w_in_bf, w_out_bf, gf)


def _rope_rows(x, cos, sin, n_heads):
    pieces = []
    for h in range(n_heads):
        r0 = h * HEAD_DIM
        t1 = x[r0:r0 + ROT_HALF]
        t2 = x[r0 + ROT_HALF:r0 + ROT_DIM]
        pieces.append(t1 * cos - t2 * sin)
        pieces.append(t2 * cos + t1 * sin)
        pieces.append(x[r0 + ROT_DIM:r0 + HEAD_DIM])
    return jnp.concatenate(pieces, axis=0)


def _rope_lanes(x, c, sa, sb):
    return x * c + pltpu.roll(x, LANES - ROT_HALF, 1) * sa + pltpu.roll(x, ROT_HALF, 1) * sb


def _inproj_kernel(x_ref, g_ref, wt_ref, wk_ref, wg_ref, cos_ref, sin_ref, c_ref, sa_ref, sb_ref,
                   qat_ref, vat_ref, qbt_ref, vbt_ref, qit_ref, wit_ref,
                   ka_ref, kb_ref, ki_ref, ga_ref, gb_ref):
    h = _rmsnorm(x_ref[...], g_ref[...]).astype(BF16)
    cos = cos_ref[...]
    sin = sin_ref[...]

    r = 0
    t = _dot_nt(wt_ref[r:r + WIDTH_A, :], h)
    qat_ref[...] = (t * QK_SCALE).astype(BF16)
    r += WIDTH_A
    vat_ref[...] = _dot_nt(wt_ref[r:r + WIDTH_A, :], h).astype(BF16)
    r += WIDTH_A
    t = _dot_nt(wt_ref[r:r + WIDTH_B, :], h)
    qbt_ref[...] = (_rope_rows(t, cos, sin, N_HEADS_B) * QK_SCALE).astype(BF16)
    r += WIDTH_B
    vbt_ref[...] = _dot_nt(wt_ref[r:r + WIDTH_KV_B, :], h).astype(BF16)
    r += WIDTH_KV_B
    t = _dot_nt(wt_ref[r:r + WIDTH_IDX, :], h)
    qit_ref[...] = (_rope_rows(t, cos, sin, N_IDX_HEADS) * IDX_SCALE).astype(BF16)
    r += WIDTH_IDX
    wit_ref[...] = _dot_nt(wt_ref[r:r + WI_ROWS, :], h) * (N_IDX_HEADS ** -0.5)

    ka_ref[...] = _dot(h, wk_ref[:, 0:WIDTH_A]).astype(BF16)
    c, sa, sb = c_ref[...], sa_ref[...], sb_ref[...]
    t = _dot(h, wk_ref[:, WIDTH_A:WIDTH_A + LANES])
    kb_ref[...] = _rope_lanes(t, c, sa, sb).astype(BF16)
    t = _dot(h, wk_ref[:, WIDTH_A + LANES:WIDTH_A + 2 * LANES])
    ki_ref[...] = _rope_lanes(t, c, sa, sb).astype(BF16)

    ga_ref[...] = _dot(h, wg_ref[:, 0:D_MODEL])
    gb_ref[...] = _dot(h, wg_ref[:, D_MODEL:2 * D_MODEL])


def _inproj(x3d, g, wt, wk, wg, cos_t, sin_t, c_tab, sa_tab, sb_tab):
    B, S, _ = x3d.shape
    tm = PROJ_TM
    full = lambda shape: pl.BlockSpec(shape, lambda b, s: (0,) * len(shape))
    tmaj = lambda w: pl.BlockSpec((None, tm, w), lambda b, s: (b, s, 0))
    fmaj = lambda w: pl.BlockSpec((None, w, tm), lambda b, s: (b, 0, s))
    tshape = lambda w, dt: jax.ShapeDtypeStruct((B, S, w), dt)
    fshape = lambda w, dt: jax.ShapeDtypeStruct((B, w, S), dt)
    return pl.pallas_call(
        _inproj_kernel,
        grid=(B, S // tm),
        in_specs=[
            tmaj(D_MODEL), full((1, D_MODEL)), full(wt.shape), full(wk.shape), full(wg.shape),
            pl.BlockSpec((ROT_HALF, tm), lambda b, s: (0, s)),
            pl.BlockSpec((ROT_HALF, tm), lambda b, s: (0, s)),
            pl.BlockSpec((tm, LANES), lambda b, s: (s, 0)),
            pl.BlockSpec((tm, LANES), lambda b, s: (s, 0)),
            pl.BlockSpec((tm, LANES), lambda b, s: (s, 0)),
        ],
        out_specs=[fmaj(WIDTH_A), fmaj(WIDTH_A), fmaj(WIDTH_B), fmaj(WIDTH_KV_B), fmaj(WIDTH_IDX),
                   fmaj(WI_ROWS), tmaj(WIDTH_A), tmaj(LANES), tmaj(LANES), tmaj(D_MODEL), tmaj(D_MODEL)],
        out_shape=[fshape(WIDTH_A, BF16), fshape(WIDTH_A, BF16), fshape(WIDTH_B, BF16),
                   fshape(WIDTH_KV_B, BF16), fshape(WIDTH_IDX, BF16), fshape(WI_ROWS, F32),
                   tshape(WIDTH_A, BF16), tshape(LANES, BF16), tshape(LANES, BF16),
                   tshape(D_MODEL, F32), tshape(D_MODEL, F32)],
        compiler_params=pltpu.CompilerParams(
            dimension_semantics=("parallel", "parallel"), vmem_limit_bytes=VMEM_LIMIT),
        name="inproj",
    )(x3d, g, wt, wk, wg, cos_t, sin_t, c_tab, sa_tab, sb_tab)


def _padded_rhs(qt_h, slot):
    z = jnp.zeros_like(qt_h)
    return jnp.concatenate([qt_h, z] if slot == 0 else [z, qt_h], axis=0)


def _mixer_a_kernel(qt_ref, k0_ref, k1_ref, k2_ref, v0_ref, v1_ref, v2_ref, ext_ref, o_ref,
                    bias_scr, s_scr):
    jb = pl.program_id(1)
    k_refs = (k0_ref, k1_ref, k2_ref)
    v_refs = (v0_ref, v1_ref, v2_ref)

    @pl.when((pl.program_id(0) == 0) & (jb == 0))
    def _():
        qc = lax.broadcasted_iota(jnp.int32, (QB, QB), 1) // CHUNK
        for m in range(BAND_BLOCKS):
            kc = lax.broadcasted_iota(jnp.int32, (QB, QB), 0) // CHUNK + m * (QB // CHUNK)
            valid = (kc >= qc) & (kc <= qc + N_PREV_CHUNKS)
            for h in range(N_HEADS_A):
                rows = jnp.broadcast_to(ext_ref[h:h + 1, :], (QB, BIAS_SPAN))
                skew = pltpu.roll(rows, m * QB, 1, stride=1, stride_axis=0)
                bias_scr[h, m * QB:(m + 1) * QB, :] = jnp.where(valid, skew[:, 0:QB] * LOG2E, -jnp.inf)

    ones_rows = jnp.ones((ONES_ROWS, QB), BF16)

    def logits_block(h, m, rhs):
        lane0 = (h // 2) * LANES
        s = _dot(k_refs[m][:, lane0:lane0 + LANES], rhs)
        chains = [None] * FOLD_CHAINS
        for n, r in enumerate(range(0, QB, 8)):
            sr = s[r:r + 8] + bias_scr[h, m * QB + r:m * QB + r + 8, :]
            if m < BAND_BLOCKS - 1:
                sr = jnp.where(jb + m >= BAND_BLOCKS - 1, sr, -jnp.inf)
            s_scr[h % 2, m, r:r + 8, :] = sr
            c = n % FOLD_CHAINS
            chains[c] = sr if chains[c] is None else jnp.maximum(chains[c], sr)
        while len(chains) > 1:
            chains = [jnp.maximum(a, b) for a, b in zip(chains[0::2], chains[1::2])]
        return chains[0].max(axis=0, keepdims=True)

    def pv_block(h, m, mx):
        r0 = h * HEAD_DIM
        mx_rows = jnp.broadcast_to(mx, (16, QB))
        p = jnp.concatenate([jnp.exp2((s_scr[h % 2, m, r:r + 16, :] - mx_rows).astype(BF16))
                             for r in range(0, QB, 16)], axis=0)
        v_ext = jnp.concatenate([v_refs[m][r0:r0 + HEAD_DIM, :], ones_rows], axis=0)
        return _dot(v_ext, p)

    def head_rhs(h):
        return _padded_rhs(qt_ref[h * HEAD_DIM:(h + 1) * HEAD_DIM, :], h % 2)

    rhs = head_rhs(0)
    mx = functools.reduce(jnp.maximum, [logits_block(0, m, rhs) for m in range(BAND_BLOCKS)])
    for h in range(N_HEADS_A):
        r0 = h * HEAD_DIM
        acc = jnp.zeros((HEAD_DIM + ONES_ROWS, QB), F32)
        next_max = []
        if h + 1 < N_HEADS_A:
            rhs = head_rhs(h + 1)
        for m in range(BAND_BLOCKS):
            if h + 1 < N_HEADS_A:
                next_max.append(logits_block(h + 1, m, rhs))
            acc = acc + pv_block(h, m, mx)
        o_ref[r0:r0 + HEAD_DIM, :] = (acc[0:HEAD_DIM] / acc[HEAD_DIM:HEAD_DIM + 1]).astype(BF16)
        if next_max:
            mx = functools.reduce(jnp.maximum, next_max)


def _mixer_a(qat, ka, vat, bias_ext):
    B, _, S = qat.shape
    assert S % QB == 0 and QB % CHUNK == 0 and (N_PREV_CHUNKS * CHUNK) % QB == 0
    assert QB - 1 >= REL_CLIP
    nb = BAND_BLOCKS - 1
    kspec = lambda m: pl.BlockSpec((None, QB, WIDTH_A), lambda b, j: (b, jnp.maximum(j + m - nb, 0), 0))
    vspec = lambda m: pl.BlockSpec((None, WIDTH_A, QB), lambda b, j: (b, 0, jnp.maximum(j + m - nb, 0)))
    return pl.pallas_call(
        _mixer_a_kernel,
        grid=(B, S // QB),
        in_specs=[pl.BlockSpec((None, WIDTH_A, QB), lambda b, j: (b, 0, j)),
                  kspec(0), kspec(1), kspec(2), vspec(0), vspec(1), vspec(2),
                  pl.BlockSpec(bias_ext.shape, lambda b, j: (0, 0))],
        out_specs=pl.BlockSpec((None, WIDTH_A, QB), lambda b, j: (b, 0, j)),
        out_shape=jax.ShapeDtypeStruct((B, WIDTH_A, S), BF16),
        scratch_shapes=[pltpu.VMEM((N_HEADS_A, BAND_BLOCKS * QB, QB), F32),
                        pltpu.VMEM((2, BAND_BLOCKS, QB, QB), F32)],
        compiler_params=pltpu.CompilerParams(
            dimension_semantics=("arbitrary", "arbitrary"), vmem_limit_bytes=VMEM_LIMIT),
        name="mixer_a",
    )(qat, ka, ka, ka, vat, vat, vat, bias_ext)


def _ordered_code_to_f32(u):
    bits = jnp.where(u < 0, u ^ jnp.int32(-2 ** 31), ~u)
    return lax.bitcast_convert_type(bits, F32)


def _mixer_b_kernel(qit_ref, wit_ref, qbt_ref, ki_ref, kb_ref, vbt_ref, o_ref,
                    score_scr, sb_scr, s_scr, s2_scr, tmax_scr, tmax2_scr, rhs_scr, acc_scr, m_scr, l_scr,
                    thr_scr, need_scr, ties_scr, *, topk):
    jb = pl.program_id(1)
    n_tiles = jb + 1
    diag0 = pl.multiple_of(jb * KT, KT)
    q_chunk = lax.broadcasted_iota(jnp.int32, (1, QB), 1) // CHUNK
    adm_diag = lax.broadcasted_iota(jnp.int32, (KT, QB), 0) < (q_chunk + 1) * CHUNK
    zero_rows = jnp.zeros((HEAD_DIM, QB), BF16)

    def tile_loop(body, init):
        def wrapped(t, carry):
            return body(pl.multiple_of(t * KT, KT), carry)
        return lax.fori_loop(0, n_tiles, wrapped, init)

    def col_count(hit):
        return hit.reshape(KT // 8, 8, QB).sum(axis=0)

    for h in range(N_IDX_HEADS):
        rhs_scr[h, 0:IDX_DIM, :] = qit_ref[h * IDX_DIM:(h + 1) * IDX_DIM, :]
        rhs_scr[h, IDX_DIM:, :] = zero_rows
    w = wit_ref[...]

    n_pairs = (n_tiles + 1) // 2
    pad0 = pl.multiple_of(n_tiles * KT, KT)

    def pair_loop(body, init):
        def wrapped(t, carry):
            return body(pl.multiple_of(t * 2 * KT, 2 * KT), carry)
        return lax.fori_loop(0, n_pairs, wrapped, init)

    def score_pair(k0, carry):
        for h in range(0, N_IDX_HEADS, 2):
            w0 = jnp.broadcast_to(w[h:h + 1, :], (8, QB))
            w1 = jnp.broadcast_to(w[h + 1:h + 2, :], (8, QB))
            for half in range(2):
                rows = pl.ds(k0 + half * KT, KT)
                ki_t = ki_ref[rows, :]
                s0 = _dot(ki_t, rhs_scr[h])
                s1 = _dot(ki_t, rhs_scr[h + 1])
                out = score_scr.at[rows, :]
                for r in range(0, KT, 8):
                    t = w0 * jnp.maximum(s0[r:r + 8], 0.0) + w1 * jnp.maximum(s1[r:r + 8], 0.0)
                    out[r:r + 8, :] = t if h == 0 else out[r:r + 8, :] + t
        return carry

    pair_loop(score_pair, 0)
    score_scr[pl.ds(diag0, KT), :] = jnp.where(adm_diag, score_scr[pl.ds(diag0, KT), :], -jnp.inf)
    score_scr[pl.ds(pad0, KT), :] = jnp.full((KT, QB), -jnp.inf, F32)

    def round_pair(k0, carry):
        sb_scr[pl.ds(k0, 2 * KT), :] = score_scr[pl.ds(k0, 2 * KT), :].astype(BF16)
        return carry

    pair_loop(round_pair, 0)

    def count_pair(ref, k0, cand, rows, one, zero):
        chains = [None] * FOLD_CHAINS
        view = ref.at[pl.ds(k0, 2 * KT), :]
        for n, r in enumerate(range(0, 2 * KT, rows)):
            hit = jnp.where(view[r:r + rows, :] >= cand, one, zero)
            c = n % FOLD_CHAINS
            chains[c] = hit if chains[c] is None else chains[c] + hit
        while len(chains) > 1:
            chains = [a + b for a, b in zip(chains[0::2], chains[1::2])]
        return chains[0]

    def count_ge_bf16(cand):
        cand = jnp.broadcast_to(cand.astype(BF16), (16, QB))

        def body(k0, cnt):
            return cnt + count_pair(sb_scr, k0, cand, 16, jnp.int16(1), jnp.int16(0))
        cnt = pair_loop(body, jnp.zeros((16, QB), jnp.int16))
        return cnt.astype(jnp.int32).sum(axis=0, keepdims=True)

    def count_ge_f32(cand):
        cand = jnp.broadcast_to(cand, (8, QB))

        def body(k0, cnt):
            return cnt + count_pair(score_scr, k0, cand, 8, 1.0, 0.0)
        cnt = pair_loop(body, jnp.zeros((8, QB), F32))
        return cnt.sum(axis=0, keepdims=True).astype(jnp.int32)

    k_int = jnp.int32(int(topk))
    neg_inf_code = jnp.int32(0x007FFFFF)

    def accept(cnt, code):
        return (cnt >= k_int) | ((code >= 0) & (code <= neg_inf_code))

    def coarse_step(i, t_u):
        cand_u = t_u | lax.shift_left(jnp.int32(1), 31 - i)
        cnt = count_ge_bf16(_ordered_code_to_f32(cand_u))
        return jnp.where(accept(cnt, cand_u), cand_u, t_u)

    t_coarse = lax.fori_loop(0, 16, coarse_step, jnp.zeros((1, QB), jnp.int32))
    base = t_coarse - jnp.int32(1 << 16)

    def fine_step(i, carry):
        off, cnt_at = carry
        cand_off = off | lax.shift_left(jnp.int32(1), 16 - i)
        cand_u = base + cand_off
        cnt = count_ge_f32(_ordered_code_to_f32(cand_u))
        ok = accept(cnt, cand_u)
        return jnp.where(ok, cand_off, off), jnp.where(ok, cnt, cnt_at)

    off, cnt_at = lax.fori_loop(0, FINE_STEPS_FIRST, fine_step,
                                (jnp.zeros((1, QB), jnp.int32), jnp.full((1, QB), -1, jnp.int32)))
    thr_scr[...] = _ordered_code_to_f32(base + off)
    ties_scr[0] = jnp.int32(0)
    unsettled = jnp.max(jnp.where(cnt_at == k_int, 0.0, 1.0)) > 0.0

    @pl.when(unsettled)
    def _():
        off_full, _ = lax.fori_loop(FINE_STEPS_FIRST, FINE_STEPS, fine_step, (off, cnt_at))
        thr_full = _ordered_code_to_f32(base + off_full)
        thr_scr[...] = thr_full

        def count_both(k0, carry):
            ge, gt = carry
            sc = score_scr[pl.ds(k0, KT), :]
            return (ge + col_count(jnp.where(sc >= thr_full, 1.0, 0.0)),
                    gt + col_count(jnp.where(sc > thr_full, 1.0, 0.0)))

        ge8, gt8 = tile_loop(count_both, (jnp.zeros((8, QB), F32), jnp.zeros((8, QB), F32)))
        cnt_ge = ge8.sum(axis=0, keepdims=True)
        need_scr[...] = topk - gt8.sum(axis=0, keepdims=True)
        n_inadm = (KT - (q_chunk + 1) * CHUNK).astype(F32)
        n_sel = cnt_ge - jnp.where(thr_full == -jnp.inf, n_inadm, 0.0)
        ties_scr[0] = (jnp.max(jnp.where(n_sel > topk, 1.0, 0.0)) > 0.0).astype(jnp.int32)

    thr = thr_scr[...]
    has_ties = ties_scr[0] > 0

    @pl.when(jnp.logical_not(has_ties))
    def _():
        def body(k0, carry):
            score_scr[pl.ds(k0, KT), :] = jnp.where(score_scr[pl.ds(k0, KT), :] >= thr, 0.0, -jnp.inf)
            return carry
        tile_loop(body, 0)

    @pl.when(has_ties)
    def _():
        tri = jnp.where(lax.broadcasted_iota(jnp.int32, (KT, KT), 1)
                        < lax.broadcasted_iota(jnp.int32, (KT, KT), 0), 1.0, 0.0).astype(BF16)

        def body(k0, need):
            sc = score_scr[pl.ds(k0, KT), :]
            eq_f = jnp.where(sc == thr, 1.0, 0.0)
            before = _dot(tri, eq_f.astype(BF16))
            take = jnp.where(sc > thr, 1.0, jnp.where(before < need, eq_f, 0.0))
            score_scr[pl.ds(k0, KT), :] = jnp.where(take > 0.0, 0.0, -jnp.inf)
            return need - eq_f.sum(axis=0, keepdims=True)
        tile_loop(body, need_scr[...])

    score_scr[pl.ds(diag0, KT), :] = jnp.where(adm_diag, score_scr[pl.ds(diag0, KT), :], -jnp.inf)

    for h in range(N_HEADS_B):
        g = h // (N_HEADS_B // N_KV_B)
        q_h = qbt_ref[h * HEAD_DIM:(h + 1) * HEAD_DIM, :]
        rhs_scr[h, 0:HEAD_DIM, :] = q_h if g == 0 else zero_rows
        rhs_scr[h, HEAD_DIM:, :] = zero_rows if g == 0 else q_h
    m_scr[...] = jnp.full_like(m_scr, NEG_BIG)
    l_scr[...] = jnp.zeros_like(l_scr)
    acc_scr[...] = jnp.zeros_like(acc_scr)
    ones_rows = jnp.ones((ONES_ROWS, KT), BF16)

    def logits_head(h, neg, kb_t, s_buf, tmax_buf):
        s = _dot(kb_t, rhs_scr[h])
        chains = [None] * FOLD_CHAINS
        for n, r in enumerate(range(0, KT, 8)):
            sr = s[r:r + 8] + neg[r:r + 8, :]
            s_buf[h, r:r + 8, :] = sr
            c = n % FOLD_CHAINS
            chains[c] = sr if chains[c] is None else jnp.maximum(chains[c], sr)
        while len(chains) > 1:
            chains = [jnp.maximum(a, b) for a, b in zip(chains[0::2], chains[1::2])]
        tmax_buf[h:h + 1, :] = chains[0].max(axis=0, keepdims=True)

    def softmax_pv_head(h, v_ext, s_buf, tmax_buf):
        g = h // (N_HEADS_B // N_KV_B)
        m_old = m_scr[h:h + 1, :]
        m_new = jnp.maximum(m_old, tmax_buf[h:h + 1, :])
        m_scr[h:h + 1, :] = m_new
        alpha = jnp.exp2(m_old - m_new)
        m_rows = jnp.broadcast_to(m_new, (16, QB))
        p = jnp.concatenate([jnp.exp2((s_buf[h, r:r + 16, :] - m_rows).astype(BF16))
                             for r in range(0, KT, 16)], axis=0)
        pv = _dot(v_ext[g], p)
        r0 = h * HEAD_DIM
        acc_scr[r0:r0 + HEAD_DIM, :] = alpha * acc_scr[r0:r0 + HEAD_DIM, :] + pv[0:HEAD_DIM]
        l_scr[h:h + 1, :] = alpha * l_scr[h:h + 1, :] + pv[HEAD_DIM:HEAD_DIM + 1]

    def stage(t_next, next_bufs, t_cur, cur_bufs):
        k_next = pl.multiple_of(jnp.minimum(t_next, 2 * n_pairs - 1) * KT, KT)
        neg = score_scr.at[pl.ds(k_next, KT), :]
        kb_t = kb_ref[pl.ds(k_next, KT), :]
        if t_cur is not None:
            k_cur = pl.multiple_of(t_cur * KT, KT)
            v_ext = [jnp.concatenate([vbt_ref[g * HEAD_DIM:(g + 1) * HEAD_DIM, pl.ds(k_cur, KT)], ones_rows],
                                     axis=0) for g in range(N_KV_B)]
        for h in range(N_HEADS_B):
            logits_head(h, neg, kb_t, *next_bufs)
            if t_cur is not None:
                softmax_pv_head(h, v_ext, *cur_bufs)

    bufs_a, bufs_b = (s_scr, tmax_scr), (s2_scr, tmax2_scr)
    stage(0, bufs_a, None, None)

    def attn_pair(i, carry):
        t = 2 * i
        stage(t + 1, bufs_b, t, bufs_a)
        stage(t + 2, bufs_a, t + 1, bufs_b)
        return carry

    lax.fori_loop(0, n_pairs, attn_pair, 0)

    for h in range(N_HEADS_B):
        r0 = h * HEAD_DIM
        o_ref[r0:r0 + HEAD_DIM, :] = (acc_scr[r0:r0 + HEAD_DIM, :] / l_scr[h:h + 1, :]).astype(BF16)


def _mixer_b(qit, wit, qbt, ki, kb, vbt):
    B, _, S = qbt.shape
    topk = min(TOPK_MAX, S // 4)
    assert KT == QB and S % QB == 0 and QB % CHUNK == 0 and topk <= KT
    qspec = lambda w: pl.BlockSpec((None, w, QB), lambda b, j: (b, 0, j))
    return pl.pallas_call(
        functools.partial(_mixer_b_kernel, topk=float(topk)),
        grid=(B, S // QB),
        in_specs=[qspec(WIDTH_IDX), qspec(WI_ROWS), qspec(WIDTH_B),
                  pl.BlockSpec((None, S, LANES), lambda b, j: (b, 0, 0)),
                  pl.BlockSpec((None, S, LANES), lambda b, j: (b, 0, 0)),
                  pl.BlockSpec((None, WIDTH_KV_B, S), lambda b, j: (b, 0, 0))],
        out_specs=qspec(WIDTH_B),
        out_shape=jax.ShapeDtypeStruct((B, WIDTH_B, S), BF16),
        scratch_shapes=[pltpu.VMEM((S + KT, QB), F32),
                        pltpu.VMEM((S + KT, QB), BF16),
                        pltpu.VMEM((N_HEADS_B, KT, QB), F32),
                        pltpu.VMEM((N_HEADS_B, KT, QB), F32),
                        pltpu.VMEM((N_HEADS_B, QB), F32),
                        pltpu.VMEM((N_HEADS_B, QB), F32),
                        pltpu.VMEM((N_HEADS_B, 2 * HEAD_DIM, QB), BF16),
                        pltpu.VMEM((WIDTH_B, QB), F32),
                        pltpu.VMEM((N_HEADS_B, QB), F32),
                        pltpu.VMEM((N_HEADS_B, QB), F32),
                        pltpu.VMEM((1, QB), F32),
                        pltpu.VMEM((1, QB), F32),
                        pltpu.SMEM((1,), jnp.int32)],
        compiler_params=pltpu.CompilerParams(
            dimension_semantics=("parallel", "arbitrary"), vmem_limit_bytes=VMEM_LIMIT),
        name="mixer_b",
    )(qit, wit, qbt, ki, kb, vbt)


def _merge_kernel(x_ref, oat_ref, obt_ref, ga_ref, gb_ref, wa_ref, wb_ref, wo_ref, o_ref):
    ya = _dot_tn(oat_ref[...], wa_ref[...])
    yb = _dot_tn(obt_ref[...], wb_ref[...])
    merged = jax.nn.sigmoid(ga_ref[...]) * ya + jax.nn.sigmoid(gb_ref[...]) * yb
    o_ref[...] = x_ref[...] + _dot(merged.astype(BF16), wo_ref[...])


def _merge(x3d, oat, obt, ga, gb, wa, wb, wo):
    B, S, _ = x3d.shape
    tm = PROJ_TM
    tmaj = pl.BlockSpec((None, tm, D_MODEL), lambda b, s: (b, s, 0))
    fmaj = lambda w: pl.BlockSpec((None, w, tm), lambda b, s: (b, 0, s))
    full = lambda a: pl.BlockSpec(a.shape, lambda b, s: (0, 0))
    return pl.pallas_call(
        _merge_kernel,
        grid=(B, S // tm),
        in_specs=[tmaj, fmaj(WIDTH_A), fmaj(WIDTH_B), tmaj, tmaj, full(wa), full(wb), full(wo)],
        out_specs=tmaj,
        out_shape=jax.ShapeDtypeStruct((B, S, D_MODEL), F32),
        compiler_params=pltpu.CompilerParams(
            dimension_semantics=("parallel", "parallel"), vmem_limit_bytes=VMEM_LIMIT),
        name="merge",
    )(x3d, oat, obt, ga, gb, wa, wb, wo)


def _rope_tables(seq):
    inv_freq = jnp.power(jnp.float32(ROPE_THETA), -jnp.arange(0, ROT_DIM, 2, dtype=F32) / ROT_DIM)
    ang = jnp.arange(seq, dtype=F32)[:, None] * inv_freq[None, :]
    cos, sin = jnp.cos(ang), jnp.sin(ang)
    ones = jnp.ones((seq, HEAD_DIM - ROT_DIM), F32)
    zeros = jnp.zeros((seq, HEAD_DIM - ROT_DIM), F32)
    zh = jnp.zeros((seq, ROT_HALF), F32)
    c = jnp.concatenate([cos, cos, ones], axis=1)
    sa = jnp.concatenate([-sin, zh, zeros], axis=1)
    sb = jnp.concatenate([zh, sin, zeros], axis=1)
    rep = LANES // HEAD_DIM
    return cos.T, sin.T, jnp.tile(c, (1, rep)), jnp.tile(sa, (1, rep)), jnp.tile(sb, (1, rep))


def _band_bias_ext(rel_bias):
    n_keys = BAND_BLOCKS * QB
    n_h = rel_bias.shape[0]
    lo = jnp.broadcast_to(rel_bias[:, :1], (n_h, QB - 1 - REL_CLIP))
    hi = jnp.broadcast_to(rel_bias[:, -1:], (n_h, BIAS_SPAN - (QB - 1 - REL_CLIP) - (2 * REL_CLIP + 1)))
    ext = jnp.concatenate([lo, rel_bias, hi], axis=1)
    return jnp.roll(ext, -(n_keys - 1), axis=1).astype(F32)


def kernel(x, n1_g, ffn1_w_in, ffn1_w_out, n2_g, w_in, rel_bias, w_branch_a, w_branch_b, w_out,
           n3_g, ffn2_w_in, ffn2_w_out, nf_g):
    B, S, D = x.shape
    depth = n1_g.shape[0]
    cos_t, sin_t, c_tab, sa_tab, sb_tab = _rope_tables(S)
    nf = nf_g.reshape(1, D)
    offs = np.cumsum([0, WIDTH_A, WIDTH_A, WIDTH_A, WIDTH_B, WIDTH_KV_B, WIDTH_KV_B,
                      WIDTH_IDX, IDX_DIM, N_IDX_HEADS, D_MODEL, D_MODEL])
    seg = lambda w, i: w[:, offs[i]:offs[i + 1]]

    for l in range(depth):
        w = w_in[l]
        wt = jnp.concatenate(
            [seg(w, 0), seg(w, 2), seg(w, 3), seg(w, 5), seg(w, 6),
             jnp.pad(seg(w, 8), ((0, 0), (0, WI_ROWS - N_IDX_HEADS)))], axis=1).T.astype(BF16)
        wk = jnp.concatenate(
            [seg(w, 1), seg(w, 4), jnp.pad(seg(w, 7), ((0, 0), (0, LANES - IDX_DIM)))], axis=1).astype(BF16)
        wg = jnp.concatenate([seg(w, 9), seg(w, 10)], axis=1).astype(BF16)

        x2d = _ffn(x.reshape(B * S, D), n1_g[l].reshape(1, D), ffn1_w_in[l].astype(BF16),
                   ffn1_w_out[l].astype(BF16), nf, False)
        x = x2d.reshape(B, S, D)
        (qat, vat, qbt, vbt, qit, wit, ka, kb, ki, ga, gb) = _inproj(
            x, n2_g[l].reshape(1, D), wt, wk, wg, cos_t, sin_t, c_tab, sa_tab, sb_tab)
        oat = _mixer_a(qat, ka, vat, _band_bias_ext(rel_bias[l]))
        obt = _mixer_b(qit, wit, qbt, ki, kb, vbt)
        x = _merge(x, oat, obt, ga, gb, w_branch_a[l].astype(BF16), w_branch_b[l].astype(BF16),
                   w_out[l].astype(BF16))
        last = l == depth - 1
        x2d = _ffn(x.reshape(B * S, D), n3_g[l].reshape(1, D), ffn2_w_in[l].astype(BF16),
                   ffn2_w_out[l].astype(BF16), nf, last)
        x = x2d.reshape(B, S, D)
    return x
```

```python
import functools

import jax
import jax.numpy as jnp
import numpy as np
from jax import lax
from jax.experimental import pallas as pl
from jax.experimental.pallas import tpu as pltpu

F32 = jnp.float32
BF16 = jnp.bfloat16

D_MODEL = 1024
D_FF = 2816
HEAD_DIM = 64
CHUNK = 64
N_PREV_CHUNKS = 8
N_HEADS_A = 8
REL_CLIP = 128
N_HEADS_B = 8
N_KV_B = 2
N_IDX_HEADS = 8
IDX_DIM = 64
TOPK_MAX = 256
ROPE_THETA = 500000.0
ROT_DIM = HEAD_DIM // 4
ROT_HALF = ROT_DIM // 2
EPS = 1e-6
WIDTH_A = N_HEADS_A * HEAD_DIM
WIDTH_B = N_HEADS_B * HEAD_DIM
WIDTH_KV_B = N_KV_B * HEAD_DIM
WIDTH_IDX = N_IDX_HEADS * IDX_DIM

LANES = 128
LOG2E = 1.4426950408889634
QK_SCALE = HEAD_DIM ** -0.5 * LOG2E
IDX_SCALE = IDX_DIM ** -0.5

VMEM_LIMIT = 56 * 1024 * 1024

FFN_TM = 1024
FFN_TF = D_FF // 2
PROJ_TM = 512
QB = 256
KT = 256
BAND_BLOCKS = N_PREV_CHUNKS * CHUNK // QB + 1
BIAS_SPAN = (BAND_BLOCKS + 1) * QB
WI_ROWS = 16
ONES_ROWS = 16
NEG_BIG = -1e30
FINE_STEPS = 17
FINE_STEPS_FIRST = 11
FOLD_CHAINS = 8


def _dot(a, b):
    return jnp.dot(a, b, preferred_element_type=F32)


def _dot_nt(a, b):
    return lax.dot_general(a, b, (((1,), (1,)), ((), ())), preferred_element_type=F32)


def _dot_tn(a, b):
    return lax.dot_general(a, b, (((0,), (0,)), ((), ())), preferred_element_type=F32)


def _rmsnorm(x, g):
    ms = jnp.mean(x * x, axis=-1, keepdims=True)
    return x * lax.rsqrt(ms + EPS) * g


def _ffn_kernel(x_ref, g_ref, wg_ref, wu_ref, wo_ref, gf_ref, o_ref, h_scr, acc_scr, *, final_norm):
    j = pl.program_id(1)

    @pl.when(j == 0)
    def _():
        h_scr[...] = _rmsnorm(x_ref[...], g_ref[...]).astype(BF16)
        acc_scr[...] = jnp.zeros_like(acc_scr)

    h = h_scr[...]
    gate = _dot(h, wg_ref[...])
    up = _dot(h, wu_ref[...])
    a = (gate * jax.nn.sigmoid(gate) * up).astype(BF16)
    acc_scr[...] += _dot(a, wo_ref[...])

    @pl.when(j == pl.num_programs(1) - 1)
    def _():
        y = x_ref[...] + 0.5 * acc_scr[...]
        if final_norm:
            y = _rmsnorm(y, gf_ref[...])
        o_ref[...] = y


def _ffn(x2d, g, w_in_bf, w_out_bf, gf, final_norm):
    T = x2d.shape[0]
    nf = D_FF // FFN_TF
    return pl.pallas_call(
        functools.partial(_ffn_kernel, final_norm=final_norm),
        grid=(T // FFN_TM, nf),
        in_specs=[
            pl.BlockSpec((FFN_TM, D_MODEL), lambda i, j: (i, 0)),
            pl.BlockSpec((1, D_MODEL), lambda i, j: (0, 0)),
            pl.BlockSpec((D_MODEL, FFN_TF), lambda i, j: (0, j)),
            pl.BlockSpec((D_MODEL, FFN_TF), lambda i, j: (0, j + nf)),
            pl.BlockSpec((FFN_TF, D_MODEL), lambda i, j: (j, 0)),
            pl.BlockSpec((1, D_MODEL), lambda i, j: (0, 0)),
        ],
        out_specs=pl.BlockSpec((FFN_TM, D_MODEL), lambda i, j: (i, 0)),
        out_shape=jax.ShapeDtypeStruct((T, D_MODEL), F32),
        scratch_shapes=[pltpu.VMEM((FFN_TM, D_MODEL), BF16), pltpu.VMEM((FFN_TM, D_MODEL), F32)],
        compiler_params=pltpu.CompilerParams(
            dimension_semantics=("parallel", "arbitrary"), vmem_limit_bytes=VMEM_LIMIT),
        name="ffn_final" if final_norm else "ffn",
    )(x2d, g, w_in_bf, w_in_bf, w_out_bf, gf)


def _rope_rows(x, cos, sin, n_heads):
    pieces = []
    for h in range(n_heads):
        r0 = h * HEAD_DIM
        t1 = x[r0:r0 + ROT_HALF]
        t2 = x[r0 + ROT_HALF:r0 + ROT_DIM]
        pieces.append(t1 * cos - t2 * sin)
        pieces.append(t2 * cos + t1 * sin)
        pieces.append(x[r0 + ROT_DIM:r0 + HEAD_DIM])
    return jnp.concatenate(pieces, axis=0)


def _rope_lanes(x, c, sa, sb):
    return x * c + pltpu.roll(x, LANES - ROT_HALF, 1) * sa + pltpu.roll(x, ROT_HALF, 1) * sb


def _inproj_kernel(x_ref, g_ref, wt_ref, wk_ref, wg_ref, cos_ref, sin_ref, c_ref, sa_ref, sb_ref,
                   qat_ref, vat_ref, qbt_ref, vbt_ref, qit_ref, wit_ref,
                   ka_ref, kb_ref, ki_ref, ga_ref, gb_ref):
    h = _rmsnorm(x_ref[...], g_ref[...]).astype(BF16)
    cos = cos_ref[...]
    sin = sin_ref[...]

    ga_ref[...] = _dot(h, wg_ref[:, 0:D_MODEL]).astype(BF16)
    gb_ref[...] = _dot(h, wg_ref[:, D_MODEL:2 * D_MODEL]).astype(BF16)

    r = 0
    t = _dot_nt(wt_ref[r:r + WIDTH_A, :], h)
    qat_ref[...] = (t * QK_SCALE).astype(BF16)
    r += WIDTH_A
    vat_ref[...] = _dot_nt(wt_ref[r:r + WIDTH_A, :], h).astype(BF16)
    r += WIDTH_A
    t = _dot_nt(wt_ref[r:r + WIDTH_B, :], h)
    qbt_ref[...] = (_rope_rows(t, cos, sin, N_HEADS_B) * QK_SCALE).astype(BF16)
    r += WIDTH_B
    vbt_ref[...] = _dot_nt(wt_ref[r:r + WIDTH_KV_B, :], h).astype(BF16)
    r += WIDTH_KV_B
    t = _dot_nt(wt_ref[r:r + WIDTH_IDX, :], h)
    qit_ref[...] = (_rope_rows(t, cos, sin, N_IDX_HEADS) * IDX_SCALE).astype(BF16)
    r += WIDTH_IDX
    wit_ref[...] = _dot_nt(wt_ref[r:r + WI_ROWS, :], h) * (N_IDX_HEADS ** -0.5)

    ka_ref[...] = _dot(h, wk_ref[:, 0:WIDTH_A]).astype(BF16)
    c, sa, sb = c_ref[...], sa_ref[...], sb_ref[...]
    t = _dot(h, wk_ref[:, WIDTH_A:WIDTH_A + LANES])
    kb_ref[...] = _rope_lanes(t, c, sa, sb).astype(BF16)
    t = _dot(h, wk_ref[:, WIDTH_A + LANES:WIDTH_A + 2 * LANES])
    ki_ref[...] = _rope_lanes(t, c, sa, sb).astype(BF16)


def _inproj(x3d, g, wt, wk, wg, cos_t, sin_t, c_tab, sa_tab, sb_tab):
    B, S, _ = x3d.shape
    tm = PROJ_TM
    full = lambda shape: pl.BlockSpec(shape, lambda b, s: (0,) * len(shape))
    tmaj = lambda w: pl.BlockSpec((None, tm, w), lambda b, s: (b, s, 0))
    fmaj = lambda w: pl.BlockSpec((None, w, tm), lambda b, s: (b, 0, s))
    tshape = lambda w, dt: jax.ShapeDtypeStruct((B, S, w), dt)
    fshape = lambda w, dt: jax.ShapeDtypeStruct((B, w, S), dt)
    return pl.pallas_call(
        _inproj_kernel,
        grid=(B, S // tm),
        in_specs=[
            tmaj(D_MODEL), full((1, D_MODEL)), full(wt.shape), full(wk.shape), full(wg.shape),
            pl.BlockSpec((ROT_HALF, tm), lambda b, s: (0, s)),
            pl.BlockSpec((ROT_HALF, tm), lambda b, s: (0, s)),
            pl.BlockSpec((tm, LANES), lambda b, s: (s, 0)),
            pl.BlockSpec((tm, LANES), lambda b, s: (s, 0)),
            pl.BlockSpec((tm, LANES), lambda b, s: (s, 0)),
        ],
        out_specs=[fmaj(WIDTH_A), fmaj(WIDTH_A), fmaj(WIDTH_B), fmaj(WIDTH_KV_B), fmaj(WIDTH_IDX),
                   fmaj(WI_ROWS), tmaj(WIDTH_A), tmaj(LANES), tmaj(LANES), tmaj(D_MODEL), tmaj(D_MODEL)],
        out_shape=[fshape(WIDTH_A, BF16), fshape(WIDTH_A, BF16), fshape(WIDTH_B, BF16),
                   fshape(WIDTH_KV_B, BF16), fshape(WIDTH_IDX, BF16), fshape(WI_ROWS, F32),
                   tshape(WIDTH_A, BF16), tshape(LANES, BF16), tshape(LANES, BF16),
                   tshape(D_MODEL, BF16), tshape(D_MODEL, BF16)],
        compiler_params=pltpu.CompilerParams(
            dimension_semantics=("parallel", "parallel"), vmem_limit_bytes=VMEM_LIMIT),
        name="inproj",
    )(x3d, g, wt, wk, wg, cos_t, sin_t, c_tab, sa_tab, sb_tab)


def _padded_rhs(qt_h, slot):
    z = jnp.zeros_like(qt_h)
    return jnp.concatenate([qt_h, z] if slot == 0 else [z, qt_h], axis=0)


def _mixer_a_kernel(qt_ref, k0_ref, k1_ref, k2_ref, v0_ref, v1_ref, v2_ref, ext_ref, o_ref,
                    bias_scr, s_scr):
    jb = pl.program_id(1)
    k_refs = (k0_ref, k1_ref, k2_ref)
    v_refs = (v0_ref, v1_ref, v2_ref)

    @pl.when((pl.program_id(0) == 0) & (jb == 0))
    def _():
        qc = lax.broadcasted_iota(jnp.int32, (QB, QB), 1) // CHUNK
        for m in range(BAND_BLOCKS):
            kc = lax.broadcasted_iota(jnp.int32, (QB, QB), 0) // CHUNK + m * (QB // CHUNK)
            valid = (kc >= qc) & (kc <= qc + N_PREV_CHUNKS)
            for h in range(N_HEADS_A):
                rows = jnp.broadcast_to(ext_ref[h:h + 1, :], (QB, BIAS_SPAN))
                skew = pltpu.roll(rows, m * QB, 1, stride=1, stride_axis=0)
                bias_scr[h, m * QB:(m + 1) * QB, :] = jnp.where(valid, skew[:, 0:QB] * LOG2E, -jnp.inf)

    ones_rows = jnp.ones((ONES_ROWS, QB), BF16)

    def logits_block(h, m, rhs):
        lane0 = (h // 2) * LANES
        sm = _dot(k_refs[m][:, lane0:lane0 + LANES], rhs) + bias_scr[h, m * QB:(m + 1) * QB, :]
        if m < BAND_BLOCKS - 1:
            sm = jnp.where(jb + m >= BAND_BLOCKS - 1, sm, -jnp.inf)
        s_scr[h % 2, m] = sm
        return sm.max(axis=0, keepdims=True)

    def pv_block(h, m, mx):
        r0 = h * HEAD_DIM
        p = jnp.exp2((s_scr[h % 2, m] - mx).astype(BF16))
        v_ext = jnp.concatenate([v_refs[m][r0:r0 + HEAD_DIM, :], ones_rows], axis=0)
        return _dot(v_ext, p)

    def head_rhs(h):
        return _padded_rhs(qt_ref[h * HEAD_DIM:(h + 1) * HEAD_DIM, :], h % 2)

    rhs = head_rhs(0)
    mx = functools.reduce(jnp.maximum, [logits_block(0, m, rhs) for m in range(BAND_BLOCKS)])
    for h in range(N_HEADS_A):
        r0 = h * HEAD_DIM
        acc = jnp.zeros((HEAD_DIM + ONES_ROWS, QB), F32)
        next_max = []
        if h + 1 < N_HEADS_A:
            rhs = head_rhs(h + 1)
        for m in range(BAND_BLOCKS):
            if h + 1 < N_HEADS_A:
                next_max.append(logits_block(h + 1, m, rhs))
            acc = acc + pv_block(h, m, mx)
        o_ref[r0:r0 + HEAD_DIM, :] = (acc[0:HEAD_DIM] / acc[HEAD_DIM:HEAD_DIM + 1]).astype(BF16)
        if next_max:
            mx = functools.reduce(jnp.maximum, next_max)


def _mixer_a(qat, ka, vat, bias_ext):
    B, _, S = qat.shape
    assert S % QB == 0 and QB % CHUNK == 0 and (N_PREV_CHUNKS * CHUNK) % QB == 0
    assert QB - 1 >= REL_CLIP
    nb = BAND_BLOCKS - 1
    kspec = lambda m: pl.BlockSpec((None, QB, WIDTH_A), lambda b, j: (b, jnp.maximum(j + m - nb, 0), 0))
    vspec = lambda m: pl.BlockSpec((None, WIDTH_A, QB), lambda b, j: (b, 0, jnp.maximum(j + m - nb, 0)))
    return pl.pallas_call(
        _mixer_a_kernel,
        grid=(B, S // QB),
        in_specs=[pl.BlockSpec((None, WIDTH_A, QB), lambda b, j: (b, 0, j)),
                  kspec(0), kspec(1), kspec(2), vspec(0), vspec(1), vspec(2),
                  pl.BlockSpec(bias_ext.shape, lambda b, j: (0, 0))],
        out_specs=pl.BlockSpec((None, WIDTH_A, QB), lambda b, j: (b, 0, j)),
        out_shape=jax.ShapeDtypeStruct((B, WIDTH_A, S), BF16),
        scratch_shapes=[pltpu.VMEM((N_HEADS_A, BAND_BLOCKS * QB, QB), F32),
                        pltpu.VMEM((2, BAND_BLOCKS, QB, QB), F32)],
        compiler_params=pltpu.CompilerParams(
            dimension_semantics=("arbitrary", "arbitrary"), vmem_limit_bytes=VMEM_LIMIT),
        name="mixer_a",
    )(qat, ka, ka, ka, vat, vat, vat, bias_ext)


def _ordered_code_to_f32(u):
    bits = jnp.where(u < 0, u ^ jnp.int32(-2 ** 31), ~u)
    return lax.bitcast_convert_type(bits, F32)


def _mixer_b_kernel(qit_ref, wit_ref, qbt_ref, ki_ref, kb_ref, vbt_ref, o_ref,
                    score_scr, sb_scr, s_scr, s2_scr, tmax_scr, tmax2_scr, rhs_scr, acc_scr, m_scr, l_scr,
                    thr_scr, need_scr, ties_scr, *, topk):
    jb = pl.program_id(1)
    n_tiles = jb + 1
    diag0 = pl.multiple_of(jb * KT, KT)
    q_chunk = lax.broadcasted_iota(jnp.int32, (1, QB), 1) // CHUNK
    adm_diag = lax.broadcasted_iota(jnp.int32, (KT, QB), 0) < (q_chunk + 1) * CHUNK
    zero_rows = jnp.zeros((HEAD_DIM, QB), BF16)

    def tile_loop(body, init):
        def wrapped(t, carry):
            return body(pl.multiple_of(t * KT, KT), carry)
        return lax.fori_loop(0, n_tiles, wrapped, init)

    def col_count(hit):
        return hit.reshape(KT // 8, 8, QB).sum(axis=0)

    for h in range(N_IDX_HEADS):
        rhs_scr[h, 0:IDX_DIM, :] = qit_ref[h * IDX_DIM:(h + 1) * IDX_DIM, :]
        rhs_scr[h, IDX_DIM:, :] = zero_rows
    w = wit_ref[...]

    n_pairs = (n_tiles + 1) // 2
    pad0 = pl.multiple_of(n_tiles * KT, KT)

    def pair_loop(body, init):
        def wrapped(t, carry):
            return body(pl.multiple_of(t * 2 * KT, 2 * KT), carry)
        return lax.fori_loop(0, n_pairs, wrapped, init)

    def score_pair(k0, carry):
        ki_a = ki_ref[pl.ds(k0, KT), :]
        ki_b = ki_ref[pl.ds(k0 + KT, KT), :]
        acc_a = jnp.zeros((KT, QB), F32)
        acc_b = jnp.zeros((KT, QB), F32)
        for h in range(N_IDX_HEADS):
            acc_a = acc_a + w[h:h + 1, :] * jnp.maximum(_dot(ki_a, rhs_scr[h]), 0.0)
            acc_b = acc_b + w[h:h + 1, :] * jnp.maximum(_dot(ki_b, rhs_scr[h]), 0.0)
        score_scr[pl.ds(k0, KT), :] = acc_a
        score_scr[pl.ds(k0 + KT, KT), :] = acc_b
        return carry

    pair_loop(score_pair, 0)
    score_scr[pl.ds(diag0, KT), :] = jnp.where(adm_diag, score_scr[pl.ds(diag0, KT), :], -jnp.inf)
    score_scr[pl.ds(pad0, KT), :] = jnp.full((KT, QB), -jnp.inf, F32)

    def round_pair(k0, carry):
        sb_scr[pl.ds(k0, 2 * KT), :] = score_scr[pl.ds(k0, 2 * KT), :].astype(BF16)
        return carry

    pair_loop(round_pair, 0)

    def count_pair(ref, k0, cand, rows, one, zero):
        chains = [None] * FOLD_CHAINS
        view = ref.at[pl.ds(k0, 2 * KT), :]
        for n, r in enumerate(range(0, 2 * KT, rows)):
            hit = jnp.where(view[r:r + rows, :] >= cand, one, zero)
            c = n % FOLD_CHAINS
            chains[c] = hit if chains[c] is None else chains[c] + hit
        while len(chains) > 1:
            chains = [a + b for a, b in zip(chains[0::2], chains[1::2])]
        return chains[0]

    def count_ge_bf16(cand):
        cand = jnp.broadcast_to(cand.astype(BF16), (16, QB))

        def body(k0, cnt):
            return cnt + count_pair(sb_scr, k0, cand, 16, jnp.int16(1), jnp.int16(0))
        cnt = pair_loop(body, jnp.zeros((16, QB), jnp.int16))
        return cnt.astype(jnp.int32).sum(axis=0, keepdims=True)

    def count_ge_f32(cand):
        cand = jnp.broadcast_to(cand, (8, QB))

        def body(k0, cnt):
            return cnt + count_pair(score_scr, k0, cand, 8, 1.0, 0.0)
        cnt = pair_loop(body, jnp.zeros((8, QB), F32))
        return cnt.sum(axis=0, keepdims=True).astype(jnp.int32)

    k_int = jnp.int32(int(topk))
    neg_inf_code = jnp.int32(0x007FFFFF)

    def accept(cnt, code):
        return (cnt >= k_int) | ((code >= 0) & (code <= neg_inf_code))

    def coarse_step(i, t_u):
        cand_u = t_u | lax.shift_left(jnp.int32(1), 31 - i)
        cnt = count_ge_bf16(_ordered_code_to_f32(cand_u))
        return jnp.where(accept(cnt, cand_u), cand_u, t_u)

    t_coarse = lax.fori_loop(0, 16, coarse_step, jnp.zeros((1, QB), jnp.int32))
    base = t_coarse - jnp.int32(1 << 16)

    def fine_step(i, carry):
        off, cnt_at = carry
        cand_off = off | lax.shift_left(jnp.int32(1), 16 - i)
        cand_u = base + cand_off
        cnt = count_ge_f32(_ordered_code_to_f32(cand_u))
        ok = accept(cnt, cand_u)
        return jnp.where(ok, cand_off, off), jnp.where(ok, cnt, cnt_at)

    off, cnt_at = lax.fori_loop(0, FINE_STEPS_FIRST, fine_step,
                                (jnp.zeros((1, QB), jnp.int32), jnp.full((1, QB), -1, jnp.int32)))
    thr_scr[...] = _ordered_code_to_f32(base + off)
    ties_scr[0] = jnp.int32(0)
    unsettled = jnp.max(jnp.where(cnt_at == k_int, 0.0, 1.0)) > 0.0

    @pl.when(unsettled)
    def _():
        off_full, _ = lax.fori_loop(FINE_STEPS_FIRST, FINE_STEPS, fine_step, (off, cnt_at))
        thr_full = _ordered_code_to_f32(base + off_full)
        thr_scr[...] = thr_full

        def count_both(k0, carry):
            ge, gt = carry
            sc = score_scr[pl.ds(k0, KT), :]
            return (ge + col_count(jnp.where(sc >= thr_full, 1.0, 0.0)),
                    gt + col_count(jnp.where(sc > thr_full, 1.0, 0.0)))

        ge8, gt8 = tile_loop(count_both, (jnp.zeros((8, QB), F32), jnp.zeros((8, QB), F32)))
        cnt_ge = ge8.sum(axis=0, keepdims=True)
        need_scr[...] = topk - gt8.sum(axis=0, keepdims=True)
        n_inadm = (KT - (q_chunk + 1) * CHUNK).astype(F32)
        n_sel = cnt_ge - jnp.where(thr_full == -jnp.inf, n_inadm, 0.0)
        ties_scr[0] = (jnp.max(jnp.where(n_sel > topk, 1.0, 0.0)) > 0.0).astype(jnp.int32)

    thr = thr_scr[...]
    has_ties = ties_scr[0] > 0

    @pl.when(jnp.logical_not(has_ties))
    def _():
        def body(k0, carry):
            score_scr[pl.ds(k0, KT), :] = jnp.where(score_scr[pl.ds(k0, KT), :] >= thr, 0.0, -jnp.inf)
            return carry
        tile_loop(body, 0)

    @pl.when(has_ties)
    def _():
        tri = jnp.where(lax.broadcasted_iota(jnp.int32, (KT, KT), 1)
                        < lax.broadcasted_iota(jnp.int32, (KT, KT), 0), 1.0, 0.0).astype(BF16)

        def body(k0, need):
            sc = score_scr[pl.ds(k0, KT), :]
            eq_f = jnp.where(sc == thr, 1.0, 0.0)
            before = _dot(tri, eq_f.astype(BF16))
            take = jnp.where(sc > thr, 1.0, jnp.where(before < need, eq_f, 0.0))
            score_scr[pl.ds(k0, KT), :] = jnp.where(take > 0.0, 0.0, -jnp.inf)
            return need - eq_f.sum(axis=0, keepdims=True)
        tile_loop(body, need_scr[...])

    score_scr[pl.ds(diag0, KT), :] = jnp.where(adm_diag, score_scr[pl.ds(diag0, KT), :], -jnp.inf)

    for h in range(N_HEADS_B):
        g = h // (N_HEADS_B // N_KV_B)
        q_h = qbt_ref[h * HEAD_DIM:(h + 1) * HEAD_DIM, :]
        rhs_scr[h, 0:HEAD_DIM, :] = q_h if g == 0 else zero_rows
        rhs_scr[h, HEAD_DIM:, :] = zero_rows if g == 0 else q_h
    m_scr[...] = jnp.full_like(m_scr, NEG_BIG)
    l_scr[...] = jnp.zeros_like(l_scr)
    acc_scr[...] = jnp.zeros_like(acc_scr)
    ones_rows = jnp.ones((ONES_ROWS, KT), BF16)

    def logits_head(h, neg, kb_t, s_buf, tmax_buf):
        s = _dot(kb_t, rhs_scr[h]) + neg
        s_buf[h] = s
        tmax_buf[h:h + 1, :] = s.max(axis=0, keepdims=True)

    def softmax_pv_head(h, v_ext, s_buf, tmax_buf):
        g = h // (N_HEADS_B // N_KV_B)
        m_old = m_scr[h:h + 1, :]
        m_new = jnp.maximum(m_old, tmax_buf[h:h + 1, :])
        m_scr[h:h + 1, :] = m_new
        alpha = jnp.exp2(m_old - m_new)
        p = jnp.exp2((s_buf[h] - m_new).astype(BF16))
        pv = _dot(v_ext[g], p)
        r0 = h * HEAD_DIM
        acc_scr[r0:r0 + HEAD_DIM, :] = alpha * acc_scr[r0:r0 + HEAD_DIM, :] + pv[0:HEAD_DIM]
        l_scr[h:h + 1, :] = alpha * l_scr[h:h + 1, :] + pv[HEAD_DIM:HEAD_DIM + 1]

    def stage(t_next, next_bufs, t_cur, cur_bufs):
        k_next = pl.multiple_of(jnp.minimum(t_next, 2 * n_pairs - 1) * KT, KT)
        neg = score_scr[pl.ds(k_next, KT), :]
        kb_t = kb_ref[pl.ds(k_next, KT), :]
        if t_cur is not None:
            k_cur = pl.multiple_of(t_cur * KT, KT)
            v_ext = [jnp.concatenate([vbt_ref[g * HEAD_DIM:(g + 1) * HEAD_DIM, pl.ds(k_cur, KT)], ones_rows],
                                     axis=0) for g in range(N_KV_B)]
        for h in range(N_HEADS_B):
            logits_head(h, neg, kb_t, *next_bufs)
            if t_cur is not None:
                softmax_pv_head(h, v_ext, *cur_bufs)

    bufs_a, bufs_b = (s_scr, tmax_scr), (s2_scr, tmax2_scr)
    stage(0, bufs_a, None, None)

    def attn_pair(i, carry):
        t = 2 * i
        stage(t + 1, bufs_b, t, bufs_a)
        stage(t + 2, bufs_a, t + 1, bufs_b)
        return carry

    lax.fori_loop(0, n_pairs, attn_pair, 0)

    for h in range(N_HEADS_B):
        r0 = h * HEAD_DIM
        o_ref[r0:r0 + HEAD_DIM, :] = (acc_scr[r0:r0 + HEAD_DIM, :] / l_scr[h:h + 1, :]).astype(BF16)


def _mixer_b(qit, wit, qbt, ki, kb, vbt):
    B, _, S = qbt.shape
    topk = min(TOPK_MAX, S // 4)
    assert KT == QB and S % QB == 0 and QB % CHUNK == 0 and topk <= KT
    qspec = lambda w: pl.BlockSpec((None, w, QB), lambda b, j: (b, 0, j))
    return pl.pallas_call(
        functools.partial(_mixer_b_kernel, topk=float(topk)),
        grid=(B, S // QB),
        in_specs=[qspec(WIDTH_IDX), qspec(WI_ROWS), qspec(WIDTH_B),
                  pl.BlockSpec((None, S, LANES), lambda b, j: (b, 0, 0)),
                  pl.BlockSpec((None, S, LANES), lambda b, j: (b, 0, 0)),
                  pl.BlockSpec((None, WIDTH_KV_B, S), lambda b, j: (b, 0, 0))],
        out_specs=qspec(WIDTH_B),
        out_shape=jax.ShapeDtypeStruct((B, WIDTH_B, S), BF16),
        scratch_shapes=[pltpu.VMEM((S + KT, QB), F32),
                        pltpu.VMEM((S + KT, QB), BF16),
                        pltpu.VMEM((N_HEADS_B, KT, QB), F32),
                        pltpu.VMEM((N_HEADS_B, KT, QB), F32),
                        pltpu.VMEM((N_HEADS_B, QB), F32),
                        pltpu.VMEM((N_HEADS_B, QB), F32),
                        pltpu.VMEM((N_HEADS_B, 2 * HEAD_DIM, QB), BF16),
                        pltpu.VMEM((WIDTH_B, QB), F32),
                        pltpu.VMEM((N_HEADS_B, QB), F32),
                        pltpu.VMEM((N_HEADS_B, QB), F32),
                        pltpu.VMEM((1, QB), F32),
                        pltpu.VMEM((1, QB), F32),
                        pltpu.SMEM((1,), jnp.int32)],
        compiler_params=pltpu.CompilerParams(
            dimension_semantics=("parallel", "arbitrary"), vmem_limit_bytes=VMEM_LIMIT),
        name="mixer_b",
    )(qit, wit, qbt, ki, kb, vbt)


def _merge_kernel(x_ref, oat_ref, obt_ref, ga_ref, gb_ref, wa_ref, wb_ref, wo_ref, o_ref):
    ya = _dot_tn(oat_ref[...], wa_ref[...])
    yb = _dot_tn(obt_ref[...], wb_ref[...])
    merged = (jax.nn.sigmoid(ga_ref[...].astype(F32)) * ya
              + jax.nn.sigmoid(gb_ref[...].astype(F32)) * yb)
    o_ref[...] = x_ref[...] + _dot(merged.astype(BF16), wo_ref[...])


def _merge(x3d, oat, obt, ga, gb, wa, wb, wo):
    B, S, _ = x3d.shape
    tm = PROJ_TM
    tmaj = pl.BlockSpec((None, tm, D_MODEL), lambda b, s: (b, s, 0))
    fmaj = lambda w: pl.BlockSpec((None, w, tm), lambda b, s: (b, 0, s))
    full = lambda a: pl.BlockSpec(a.shape, lambda b, s: (0, 0))
    return pl.pallas_call(
        _merge_kernel,
        grid=(B, S // tm),
        in_specs=[tmaj, fmaj(WIDTH_A), fmaj(WIDTH_B), tmaj, tmaj, full(wa), full(wb), full(wo)],
        out_specs=tmaj,
        out_shape=jax.ShapeDtypeStruct((B, S, D_MODEL), F32),
        compiler_params=pltpu.CompilerParams(
            dimension_semantics=("parallel", "parallel"), vmem_limit_bytes=VMEM_LIMIT),
        name="merge",
    )(x3d, oat, obt, ga, gb, wa, wb, wo)


def _rope_tables(seq):
    inv_freq = jnp.power(jnp.float32(ROPE_THETA), -jnp.arange(0, ROT_DIM, 2, dtype=F32) / ROT_DIM)
    ang = jnp.arange(seq, dtype=F32)[:, None] * inv_freq[None, :]
    cos, sin = jnp.cos(ang), jnp.sin(ang)
    ones = jnp.ones((seq, HEAD_DIM - ROT_DIM), F32)
    zeros = jnp.zeros((seq, HEAD_DIM - ROT_DIM), F32)
    zh = jnp.zeros((seq, ROT_HALF), F32)
    c = jnp.concatenate([cos, cos, ones], axis=1)
    sa = jnp.concatenate([-sin, zh, zeros], axis=1)
    sb = jnp.concatenate([zh, sin, zeros], axis=1)
    rep = LANES // HEAD_DIM
    return cos.T, sin.T, jnp.tile(c, (1, rep)), jnp.tile(sa, (1, rep)), jnp.tile(sb, (1, rep))


def _band_bias_ext(rel_bias):
    n_keys = BAND_BLOCKS * QB
    n_h = rel_bias.shape[0]
    lo = jnp.broadcast_to(rel_bias[:, :1], (n_h, QB - 1 - REL_CLIP))
    hi = jnp.broadcast_to(rel_bias[:, -1:], (n_h, BIAS_SPAN - (QB - 1 - REL_CLIP) - (2 * REL_CLIP + 1)))
    ext = jnp.concatenate([lo, rel_bias, hi], axis=1)
    return jnp.roll(ext, -(n_keys - 1), axis=1).astype(F32)


def kernel(x, n1_g, ffn1_w_in, ffn1_w_out, n2_g, w_in, rel_bias, w_branch_a, w_branch_b, w_out,
           n3_g, ffn2_w_in, ffn2_w_out, nf_g):
    B, S, D = x.shape
    depth = n1_g.shape[0]
    cos_t, sin_t, c_tab, sa_tab, sb_tab = _rope_tables(S)
    nf = nf_g.reshape(1, D)
    offs = np.cumsum([0, WIDTH_A, WIDTH_A, WIDTH_A, WIDTH_B, WIDTH_KV_B, WIDTH_KV_B,
                      WIDTH_IDX, IDX_DIM, N_IDX_HEADS, D_MODEL, D_MODEL])
    seg = lambda w, i: w[:, offs[i]:offs[i + 1]]

    for l in range(depth):
        w = w_in[l]
        wt = jnp.concatenate(
            [seg(w, 0), seg(w, 2), seg(w, 3), seg(w, 5), seg(w, 6),
             jnp.pad(seg(w, 8), ((0, 0), (0, WI_ROWS - N_IDX_HEADS)))], axis=1).T.astype(BF16)
        wk = jnp.concatenate(
            [seg(w, 1), seg(w, 4), jnp.pad(seg(w, 7), ((0, 0), (0, LANES - IDX_DIM)))], axis=1).astype(BF16)
        wg = jnp.concatenate([seg(w, 9), seg(w, 10)], axis=1).astype(BF16)

        x2d = _ffn(x.reshape(B * S, D), n1_g[l].reshape(1, D), ffn1_w_in[l].astype(BF16),
                   ffn1_w_out[l].astype(BF16), nf, False)
        x = x2d.reshape(B, S, D)
        (qat, vat, qbt, vbt, qit, wit, ka, kb, ki, ga, gb) = _inproj(
            x, n2_g[l].reshape(1, D), wt, wk, wg, cos_t, sin_t, c_tab, sa_tab, sb_tab)
        oat = _mixer_a(qat, ka, vat, _band_bias_ext(rel_bias[l]))
        obt = _mixer_b(qit, wit, qbt, ki, kb, vbt)
        x = _merge(x, oat, obt, ga, gb, w_branch_a[l].astype(BF16), w_branch_b[l].astype(BF16),
                   w_out[l].astype(BF16))
        last = l == depth - 1
        x2d = _ffn(x.reshape(B * S, D), n3_g[l].reshape(1, D), ffn2_w_in[l].astype(BF16),
                   ffn2_w_out[l].astype(BF16), nf, last)
        x = x2d.reshape(B, S, D)
    return x
```

```python
import functools

import jax
import jax.numpy as jnp
import numpy as np
from jax import lax
from jax.experimental import pallas as pl
from jax.experimental.pallas import tpu as pltpu

F32 = jnp.float32
BF16 = jnp.bfloat16

D_MODEL = 1024
D_FF = 2816
HEAD_DIM = 64
CHUNK = 64
N_PREV_CHUNKS = 8
N_HEADS_A = 8
REL_CLIP = 128
N_HEADS_B = 8
N_KV_B = 2
N_IDX_HEADS = 8
IDX_DIM = 64
TOPK_MAX = 256
ROPE_THETA = 500000.0
ROT_DIM = HEAD_DIM // 4
ROT_HALF = ROT_DIM // 2
EPS = 1e-6
WIDTH_A = N_HEADS_A * HEAD_DIM
WIDTH_B = N_HEADS_B * HEAD_DIM
WIDTH_KV_B = N_KV_B * HEAD_DIM
WIDTH_IDX = N_IDX_HEADS * IDX_DIM

LANES = 128
LOG2E = 1.4426950408889634
QK_SCALE = HEAD_DIM ** -0.5 * LOG2E
IDX_SCALE = IDX_DIM ** -0.5

VMEM_LIMIT = 56 * 1024 * 1024

FFN_TM = 1024
FFN_TF = D_FF // 2
PROJ_TM = 512
QB = 256
KT = 256
BAND_BLOCKS = N_PREV_CHUNKS * CHUNK // QB + 1
BIAS_SPAN = (BAND_BLOCKS + 1) * QB
WI_ROWS = 16
ONES_ROWS = 16
NEG_BIG = -1e30
FINE_STEPS = 17
FINE_STEPS_FIRST = 11
FOLD_CHAINS = 8


def _dot(a, b):
    return jnp.dot(a, b, preferred_element_type=F32)


def _dot_tt(a, b):
    return lax.dot_general(a, b, (((0,), (1,)), ((), ())), preferred_element_type=F32)


def _dot_tn(a, b):
    return lax.dot_general(a, b, (((0,), (0,)), ((), ())), preferred_element_type=F32)


def _rmsnorm(x, g):
    ms = jnp.mean(x * x, axis=-1, keepdims=True)
    return x * lax.rsqrt(ms + EPS) * g


def _ffn_kernel(x_ref, g_ref, wg_ref, wu_ref, wo_ref, gf_ref, o_ref, h_scr, acc_scr, *, final_norm):
    j = pl.program_id(1)

    @pl.when(j == 0)
    def _():
        h_scr[...] = _rmsnorm(x_ref[...], g_ref[...]).astype(BF16)
        acc_scr[...] = jnp.zeros_like(acc_scr)

    h = h_scr[...]
    gate = _dot(h, wg_ref[...])
    up = _dot(h, wu_ref[...])
    a = (gate * jax.nn.sigmoid(gate) * up).astype(BF16)
    acc_scr[...] += _dot(a, wo_ref[...])

    @pl.when(j == pl.num_programs(1) - 1)
    def _():
        y = x_ref[...] + 0.5 * acc_scr[...]
        if final_norm:
            y = _rmsnorm(y, gf_ref[...])
        o_ref[...] = y


def _ffn(x2d, g, w_in_bf, w_out_bf, gf, final_norm):
    T = x2d.shape[0]
    nf = D_FF // FFN_TF
    return pl.pallas_call(
        functools.partial(_ffn_kernel, final_norm=final_norm),
        grid=(T // FFN_TM, nf),
        in_specs=[
            pl.BlockSpec((FFN_TM, D_MODEL), lambda i, j: (i, 0)),
            pl.BlockSpec((1, D_MODEL), lambda i, j: (0, 0)),
            pl.BlockSpec((D_MODEL, FFN_TF), lambda i, j: (0, j)),
            pl.BlockSpec((D_MODEL, FFN_TF), lambda i, j: (0, j + nf)),
            pl.BlockSpec((FFN_TF, D_MODEL), lambda i, j: (j, 0)),
            pl.BlockSpec((1, D_MODEL), lambda i, j: (0, 0)),
        ],
        out_specs=pl.BlockSpec((FFN_TM, D_MODEL), lambda i, j: (i, 0)),
        out_shape=jax.ShapeDtypeStruct((T, D_MODEL), F32),
        scratch_shapes=[pltpu.VMEM((FFN_TM, D_MODEL), BF16), pltpu.VMEM((FFN_TM, D_MODEL), F32)],
        compiler_params=pltpu.CompilerParams(
            dimension_semantics=("parallel", "arbitrary"), vmem_limit_bytes=VMEM_LIMIT),
        name="ffn_final" if final_norm else "ffn",
    )(x2d, g, w_in_bf, w_in_bf, w_out_bf, gf)


def _rope_rows(x, cos, sin, n_heads):
    pieces = []
    for h in range(n_heads):
        r0 = h * HEAD_DIM
        t1 = x[r0:r0 + ROT_HALF]
        t2 = x[r0 + ROT_HALF:r0 + ROT_DIM]
        pieces.append(t1 * cos - t2 * sin)
        pieces.append(t2 * cos + t1 * sin)
        pieces.append(x[r0 + ROT_DIM:r0 + HEAD_DIM])
    return jnp.concatenate(pieces, axis=0)


def _rope_lanes(x, c, sa, sb):
    return x * c + pltpu.roll(x, LANES - ROT_HALF, 1) * sa + pltpu.roll(x, ROT_HALF, 1) * sb


def _inproj_kernel(x_ref, g_ref, wt_ref, wk_ref, wg_ref, cos_ref, sin_ref, c_ref, sa_ref, sb_ref,
                   qat_ref, vat_ref, qbt_ref, vbt_ref, qit_ref, wit_ref,
                   ka_ref, kb_ref, ki_ref, ga_ref, gb_ref):
    h = _rmsnorm(x_ref[...], g_ref[...]).astype(BF16)
    cos = cos_ref[...]
    sin = sin_ref[...]

    ga_ref[...] = _dot(h, wg_ref[:, 0:D_MODEL]).astype(BF16)
    gb_ref[...] = _dot(h, wg_ref[:, D_MODEL:2 * D_MODEL]).astype(BF16)

    r = 0
    t = _dot_tt(wt_ref[:, r:r + WIDTH_A], h)
    qat_ref[...] = (t * QK_SCALE).astype(BF16)
    r += WIDTH_A
    vat_ref[...] = _dot_tt(wt_ref[:, r:r + WIDTH_A], h).astype(BF16)
    r += WIDTH_A
    t = _dot_tt(wt_ref[:, r:r + WIDTH_B], h)
    qbt_ref[...] = (_rope_rows(t, cos, sin, N_HEADS_B) * QK_SCALE).astype(BF16)
    r += WIDTH_B
    vbt_ref[...] = _dot_tt(wt_ref[:, r:r + WIDTH_KV_B], h).astype(BF16)
    r += WIDTH_KV_B
    t = _dot_tt(wt_ref[:, r:r + WIDTH_IDX], h)
    qit_ref[...] = (_rope_rows(t, cos, sin, N_IDX_HEADS) * IDX_SCALE).astype(BF16)
    r += WIDTH_IDX
    wit_ref[...] = _dot_tt(wt_ref[:, r:r + LANES], h)[0:WI_ROWS] * (N_IDX_HEADS ** -0.5)

    ka_ref[...] = _dot(h, wk_ref[:, 0:WIDTH_A]).astype(BF16)
    c, sa, sb = c_ref[...], sa_ref[...], sb_ref[...]
    t = _dot(h, wk_ref[:, WIDTH_A:WIDTH_A + LANES])
    kb_ref[...] = _rope_lanes(t, c, sa, sb).astype(BF16)
    t = _dot(h, wk_ref[:, WIDTH_A + LANES:WIDTH_A + 2 * LANES])
    ki_ref[...] = _rope_lanes(t, c, sa, sb).astype(BF16)


def _inproj(x3d, g, wt, wk, wg, cos_t, sin_t, c_tab, sa_tab, sb_tab):
    B, S, _ = x3d.shape
    tm = PROJ_TM
    full = lambda shape: pl.BlockSpec(shape, lambda b, s: (0,) * len(shape))
    tmaj = lambda w: pl.BlockSpec((None, tm, w), lambda b, s: (b, s, 0))
    fmaj = lambda w: pl.BlockSpec((None, w, tm), lambda b, s: (b, 0, s))
    tshape = lambda w, dt: jax.ShapeDtypeStruct((B, S, w), dt)
    fshape = lambda w, dt: jax.ShapeDtypeStruct((B, w, S), dt)
    return pl.pallas_call(
        _inproj_kernel,
        grid=(B, S // tm),
        in_specs=[
            tmaj(D_MODEL), full((1, D_MODEL)), full(wt.shape), full(wk.shape), full(wg.shape),
            pl.BlockSpec((ROT_HALF, tm), lambda b, s: (0, s)),
            pl.BlockSpec((ROT_HALF, tm), lambda b, s: (0, s)),
            pl.BlockSpec((tm, LANES), lambda b, s: (s, 0)),
            pl.BlockSpec((tm, LANES), lambda b, s: (s, 0)),
            pl.BlockSpec((tm, LANES), lambda b, s: (s, 0)),
        ],
        out_specs=[fmaj(WIDTH_A), fmaj(WIDTH_A), fmaj(WIDTH_B), fmaj(WIDTH_KV_B), fmaj(WIDTH_IDX),
                   fmaj(WI_ROWS), tmaj(WIDTH_A), tmaj(LANES), tmaj(LANES), tmaj(D_MODEL), tmaj(D_MODEL)],
        out_shape=[fshape(WIDTH_A, BF16), fshape(WIDTH_A, BF16), fshape(WIDTH_B, BF16),
                   fshape(WIDTH_KV_B, BF16), fshape(WIDTH_IDX, BF16), fshape(WI_ROWS, F32),
                   tshape(WIDTH_A, BF16), tshape(LANES, BF16), tshape(LANES, BF16),
                   tshape(D_MODEL, BF16), tshape(D_MODEL, BF16)],
        compiler_params=pltpu.CompilerParams(
            dimension_semantics=("parallel", "parallel"), vmem_limit_bytes=VMEM_LIMIT),
        name="inproj",
    )(x3d, g, wt, wk, wg, cos_t, sin_t, c_tab, sa_tab, sb_tab)


def _padded_rhs(qt_h, slot):
    z = jnp.zeros_like(qt_h)
    return jnp.concatenate([qt_h, z] if slot == 0 else [z, qt_h], axis=0)


def _mixer_a_kernel(qt_ref, k0_ref, k1_ref, k2_ref, v0_ref, v1_ref, v2_ref, ext_ref, o_ref,
                    bias_scr, s_scr):
    jb = pl.program_id(1)
    k_refs = (k0_ref, k1_ref, k2_ref)
    v_refs = (v0_ref, v1_ref, v2_ref)

    @pl.when((pl.program_id(0) == 0) & (jb == 0))
    def _():
        qc = lax.broadcasted_iota(jnp.int32, (QB, QB), 1) // CHUNK
        for m in range(BAND_BLOCKS):
            kc = lax.broadcasted_iota(jnp.int32, (QB, QB), 0) // CHUNK + m * (QB // CHUNK)
            valid = (kc >= qc) & (kc <= qc + N_PREV_CHUNKS)
            for h in range(N_HEADS_A):
                rows = jnp.broadcast_to(ext_ref[h:h + 1, :], (QB, BIAS_SPAN))
                skew = pltpu.roll(rows, m * QB, 1, stride=1, stride_axis=0)
                bias_scr[h, m * QB:(m + 1) * QB, :] = jnp.where(valid, skew[:, 0:QB] * LOG2E, -jnp.inf)

    ones_rows = jnp.ones((ONES_ROWS, QB), BF16)

    def logits_block(h, m, rhs):
        lane0 = (h // 2) * LANES
        sm = _dot(k_refs[m][:, lane0:lane0 + LANES], rhs) + bias_scr[h, m * QB:(m + 1) * QB, :]
        if m < BAND_BLOCKS - 1:
            sm = jnp.where(jb + m >= BAND_BLOCKS - 1, sm, -jnp.inf)
        s_scr[h % 2, m] = sm
        return sm.max(axis=0, keepdims=True)

    def pv_block(h, m, mx):
        r0 = h * HEAD_DIM
        p = jnp.exp2((s_scr[h % 2, m] - mx).astype(BF16))
        v_ext = jnp.concatenate([v_refs[m][r0:r0 + HEAD_DIM, :], ones_rows], axis=0)
        return _dot(v_ext, p)

    def head_rhs(h):
        return _padded_rhs(qt_ref[h * HEAD_DIM:(h + 1) * HEAD_DIM, :], h % 2)

    rhs = head_rhs(0)
    mx = functools.reduce(jnp.maximum, [logits_block(0, m, rhs) for m in range(BAND_BLOCKS)])
    for h in range(N_HEADS_A):
        r0 = h * HEAD_DIM
        acc = jnp.zeros((HEAD_DIM + ONES_ROWS, QB), F32)
        next_max = []
        if h + 1 < N_HEADS_A:
            rhs = head_rhs(h + 1)
        for m in range(BAND_BLOCKS):
            if h + 1 < N_HEADS_A:
                next_max.append(logits_block(h + 1, m, rhs))
            acc = acc + pv_block(h, m, mx)
        o_ref[r0:r0 + HEAD_DIM, :] = (acc[0:HEAD_DIM] / acc[HEAD_DIM:HEAD_DIM + 1]).astype(BF16)
        if next_max:
            mx = functools.reduce(jnp.maximum, next_max)


def _mixer_a(qat, ka, vat, bias_ext):
    B, _, S = qat.shape
    assert S % QB == 0 and QB % CHUNK == 0 and (N_PREV_CHUNKS * CHUNK) % QB == 0
    assert QB - 1 >= REL_CLIP
    nb = BAND_BLOCKS - 1
    kspec = lambda m: pl.BlockSpec((None, QB, WIDTH_A), lambda b, j: (b, jnp.maximum(j + m - nb, 0), 0))
    vspec = lambda m: pl.BlockSpec((None, WIDTH_A, QB), lambda b, j: (b, 0, jnp.maximum(j + m - nb, 0)))
    return pl.pallas_call(
        _mixer_a_kernel,
        grid=(B, S // QB),
        in_specs=[pl.BlockSpec((None, WIDTH_A, QB), lambda b, j: (b, 0, j)),
                  kspec(0), kspec(1), kspec(2), vspec(0), vspec(1), vspec(2),
                  pl.BlockSpec(bias_ext.shape, lambda b, j: (0, 0))],
        out_specs=pl.BlockSpec((None, WIDTH_A, QB), lambda b, j: (b, 0, j)),
        out_shape=jax.ShapeDtypeStruct((B, WIDTH_A, S), BF16),
        scratch_shapes=[pltpu.VMEM((N_HEADS_A, BAND_BLOCKS * QB, QB), F32),
                        pltpu.VMEM((2, BAND_BLOCKS, QB, QB), F32)],
        compiler_params=pltpu.CompilerParams(
            dimension_semantics=("arbitrary", "arbitrary"), vmem_limit_bytes=VMEM_LIMIT),
        name="mixer_a",
    )(qat, ka, ka, ka, vat, vat, vat, bias_ext)


def _ordered_code_to_f32(u):
    bits = jnp.where(u < 0, u ^ jnp.int32(-2 ** 31), ~u)
    return lax.bitcast_convert_type(bits, F32)


def _mixer_b_kernel(qit_ref, wit_ref, qbt_ref, ki_ref, kb_ref, vbt_ref, o_ref,
                    score_scr, sb_scr, s_scr, s2_scr, tmax_scr, tmax2_scr, rhs_scr, acc_scr, m_scr, l_scr,
                    thr_scr, need_scr, ties_scr, *, topk):
    jb = pl.program_id(1)
    n_tiles = jb + 1
    diag0 = pl.multiple_of(jb * KT, KT)
    q_chunk = lax.broadcasted_iota(jnp.int32, (1, QB), 1) // CHUNK
    adm_diag = lax.broadcasted_iota(jnp.int32, (KT, QB), 0) < (q_chunk + 1) * CHUNK
    zero_rows = jnp.zeros((HEAD_DIM, QB), BF16)

    def tile_loop(body, init):
        def wrapped(t, carry):
            return body(pl.multiple_of(t * KT, KT), carry)
        return lax.fori_loop(0, n_tiles, wrapped, init)

    def col_count(hit):
        return hit.reshape(KT // 8, 8, QB).sum(axis=0)

    for h in range(N_IDX_HEADS):
        rhs_scr[h, 0:IDX_DIM, :] = qit_ref[h * IDX_DIM:(h + 1) * IDX_DIM, :]
        rhs_scr[h, IDX_DIM:, :] = zero_rows
    w = wit_ref[...]

    n_pairs = (n_tiles + 1) // 2
    pad0 = pl.multiple_of(n_tiles * KT, KT)

    def pair_loop(body, init):
        def wrapped(t, carry):
            return body(pl.multiple_of(t * 2 * KT, 2 * KT), carry)
        return lax.fori_loop(0, n_pairs, wrapped, init)

    def score_pair(k0, carry):
        ki_a = ki_ref[pl.ds(k0, KT), :]
        ki_b = ki_ref[pl.ds(k0 + KT, KT), :]
        acc_a = jnp.zeros((KT, QB), F32)
        acc_b = jnp.zeros((KT, QB), F32)
        for h in range(N_IDX_HEADS):
            acc_a = acc_a + w[h:h + 1, :] * jnp.maximum(_dot(ki_a, rhs_scr[h]), 0.0)
            acc_b = acc_b + w[h:h + 1, :] * jnp.maximum(_dot(ki_b, rhs_scr[h]), 0.0)
        score_scr[pl.ds(k0, KT), :] = acc_a
        score_scr[pl.ds(k0 + KT, KT), :] = acc_b
        return carry

    pair_loop(score_pair, 0)
    score_scr[pl.ds(diag0, KT), :] = jnp.where(adm_diag, score_scr[pl.ds(diag0, KT), :], -jnp.inf)
    score_scr[pl.ds(pad0, KT), :] = jnp.full((KT, QB), -jnp.inf, F32)

    def round_pair(k0, carry):
        sb_scr[pl.ds(k0, 2 * KT), :] = score_scr[pl.ds(k0, 2 * KT), :].astype(BF16)
        return carry

    pair_loop(round_pair, 0)

    def count_pair(ref, k0, cand, rows, one, zero):
        chains = [None] * FOLD_CHAINS
        view = ref.at[pl.ds(k0, 2 * KT), :]
        for n, r in enumerate(range(0, 2 * KT, rows)):
            hit = jnp.where(view[r:r + rows, :] >= cand, one, zero)
            c = n % FOLD_CHAINS
            chains[c] = hit if chains[c] is None else chains[c] + hit
        while len(chains) > 1:
            chains = [a + b for a, b in zip(chains[0::2], chains[1::2])]
        return chains[0]

    def count_ge_bf16(cand):
        cand = jnp.broadcast_to(cand.astype(BF16), (16, QB))

        def body(k0, cnt):
            return cnt + count_pair(sb_scr, k0, cand, 16, jnp.int16(1), jnp.int16(0))
        cnt = pair_loop(body, jnp.zeros((16, QB), jnp.int16))
        return cnt.astype(jnp.int32).sum(axis=0, keepdims=True)

    def count_ge_f32(cand):
        cand = jnp.broadcast_to(cand, (8, QB))

        def body(k0, cnt):
            return cnt + count_pair(score_scr, k0, cand, 8, 1.0, 0.0)
        cnt = pair_loop(body, jnp.zeros((8, QB), F32))
        return cnt.sum(axis=0, keepdims=True).astype(jnp.int32)

    k_int = jnp.int32(int(topk))
    neg_inf_code = jnp.int32(0x007FFFFF)

    def accept(cnt, code):
        return (cnt >= k_int) | ((code >= 0) & (code <= neg_inf_code))

    def coarse_step(i, t_u):
        cand_u = t_u | lax.shift_left(jnp.int32(1), 31 - i)
        cnt = count_ge_bf16(_ordered_code_to_f32(cand_u))
        return jnp.where(accept(cnt, cand_u), cand_u, t_u)

    t_coarse = lax.fori_loop(0, 16, coarse_step, jnp.zeros((1, QB), jnp.int32))
    base = t_coarse - jnp.int32(1 << 16)

    def fine_step(i, carry):
        off, cnt_at = carry
        cand_off = off | lax.shift_left(jnp.int32(1), 16 - i)
        cand_u = base + cand_off
        cnt = count_ge_f32(_ordered_code_to_f32(cand_u))
        ok = accept(cnt, cand_u)
        return jnp.where(ok, cand_off, off), jnp.where(ok, cnt, cnt_at)

    off, cnt_at = lax.fori_loop(0, FINE_STEPS_FIRST, fine_step,
                                (jnp.zeros((1, QB), jnp.int32), jnp.full((1, QB), -1, jnp.int32)))
    thr_first = _ordered_code_to_f32(base + off)
    thr_scr[...] = thr_first
    ties_scr[0] = jnp.int32(0)
    settled = cnt_at == k_int
    unsettled = jnp.max(jnp.where(settled, 0.0, 1.0)) > 0.0

    def tie_analysis(thr_q):
        thr_scr[...] = thr_q

        def count_both(k0, carry):
            ge, gt = carry
            sc = score_scr[pl.ds(k0, KT), :]
            return (ge + col_count(jnp.where(sc >= thr_q, 1.0, 0.0)),
                    gt + col_count(jnp.where(sc > thr_q, 1.0, 0.0)))

        ge8, gt8 = tile_loop(count_both, (jnp.zeros((8, QB), F32), jnp.zeros((8, QB), F32)))
        cnt_ge = ge8.sum(axis=0, keepdims=True)
        need_scr[...] = topk - gt8.sum(axis=0, keepdims=True)
        n_inadm = (KT - (q_chunk + 1) * CHUNK).astype(F32)
        n_sel = cnt_ge - jnp.where(thr_q == -jnp.inf, n_inadm, 0.0)
        ties_scr[0] = (jnp.max(jnp.where(n_sel > topk, 1.0, 0.0)) > 0.0).astype(jnp.int32)
        return cnt_ge

    @pl.when(unsettled)
    def _():
        hi_code = base + off + jnp.int32(1 << (FINE_STEPS - FINE_STEPS_FIRST))
        hi = _ordered_code_to_f32(hi_code)
        hi = jnp.where((hi != hi) & (hi_code < 0), jnp.inf, hi)

        def below_max(k0, m):
            sc = score_scr[pl.ds(k0, KT), :]
            return jnp.maximum(m, jnp.where(sc < hi, sc, -jnp.inf).reshape(KT // 8, 8, QB).max(axis=0))

        m8 = tile_loop(below_max, jnp.full((8, QB), -jnp.inf, F32))
        thr_try = jnp.where(settled, thr_first, m8.max(axis=0, keepdims=True))
        cnt_try = tie_analysis(thr_try)
        still_short = jnp.max(jnp.where(cnt_try >= topk, 0.0, 1.0)) > 0.0

        @pl.when(still_short)
        def _():
            off_full, _ = lax.fori_loop(FINE_STEPS_FIRST, FINE_STEPS, fine_step, (off, cnt_at))
            tie_analysis(_ordered_code_to_f32(base + off_full))

    thr = thr_scr[...]
    has_ties = ties_scr[0] > 0

    @pl.when(jnp.logical_not(has_ties))
    def _():
        def body(k0, carry):
            score_scr[pl.ds(k0, KT), :] = jnp.where(score_scr[pl.ds(k0, KT), :] >= thr, 0.0, -jnp.inf)
            return carry
        tile_loop(body, 0)

    @pl.when(has_ties)
    def _():
        tri = jnp.where(lax.broadcasted_iota(jnp.int32, (KT, KT), 1)
                        < lax.broadcasted_iota(jnp.int32, (KT, KT), 0), 1.0, 0.0).astype(BF16)

        def body(k0, need):
            sc = score_scr[pl.ds(k0, KT), :]
            eq_f = jnp.where(sc == thr, 1.0, 0.0)
            before = _dot(tri, eq_f.astype(BF16))
            take = jnp.where(sc > thr, 1.0, jnp.where(before < need, eq_f, 0.0))
            score_scr[pl.ds(k0, KT), :] = jnp.where(take > 0.0, 0.0, -jnp.inf)
            return need - eq_f.sum(axis=0, keepdims=True)
        tile_loop(body, need_scr[...])

    score_scr[pl.ds(diag0, KT), :] = jnp.where(adm_diag, score_scr[pl.ds(diag0, KT), :], -jnp.inf)

    for h in range(N_HEADS_B):
        g = h // (N_HEADS_B // N_KV_B)
        q_h = qbt_ref[h * HEAD_DIM:(h + 1) * HEAD_DIM, :]
        rhs_scr[h, 0:HEAD_DIM, :] = q_h if g == 0 else zero_rows
        rhs_scr[h, HEAD_DIM:, :] = zero_rows if g == 0 else q_h
    m_scr[...] = jnp.full_like(m_scr, NEG_BIG)
    l_scr[...] = jnp.zeros_like(l_scr)
    acc_scr[...] = jnp.zeros_like(acc_scr)
    ones_rows = jnp.ones((ONES_ROWS, KT), BF16)

    def logits_head(h, neg, kb_t, s_buf, tmax_buf):
        s = _dot(kb_t, rhs_scr[h]) + neg
        s_buf[h] = s
        tmax_buf[h:h + 1, :] = s.max(axis=0, keepdims=True)

    def softmax_pv_head(h, v_ext, s_buf, tmax_buf):
        g = h // (N_HEADS_B // N_KV_B)
        m_old = m_scr[h:h + 1, :]
        m_new = jnp.maximum(m_old, tmax_buf[h:h + 1, :])
        m_scr[h:h + 1, :] = m_new
        alpha = jnp.exp2(m_old - m_new)
        p = jnp.exp2((s_buf[h] - m_new).astype(BF16))
        pv = _dot(v_ext[g], p)
        r0 = h * HEAD_DIM
        acc_scr[r0:r0 + HEAD_DIM, :] = alpha * acc_scr[r0:r0 + HEAD_DIM, :] + pv[0:HEAD_DIM]
        l_scr[h:h + 1, :] = alpha * l_scr[h:h + 1, :] + pv[HEAD_DIM:HEAD_DIM + 1]

    def stage(t_next, next_bufs, t_cur, cur_bufs):
        k_next = pl.multiple_of(jnp.minimum(t_next, 2 * n_pairs - 1) * KT, KT)
        neg = score_scr[pl.ds(k_next, KT), :]
        kb_t = kb_ref[pl.ds(k_next, KT), :]
        if t_cur is not None:
            k_cur = pl.multiple_of(t_cur * KT, KT)
            v_ext = [jnp.concatenate([vbt_ref[g * HEAD_DIM:(g + 1) * HEAD_DIM, pl.ds(k_cur, KT)], ones_rows],
                                     axis=0) for g in range(N_KV_B)]
        for h in range(N_HEADS_B):
            logits_head(h, neg, kb_t, *next_bufs)
            if t_cur is not None:
                softmax_pv_head(h, v_ext, *cur_bufs)

    bufs_a, bufs_b = (s_scr, tmax_scr), (s2_scr, tmax2_scr)
    stage(0, bufs_a, None, None)

    def attn_pair(i, carry):
        t = 2 * i
        stage(t + 1, bufs_b, t, bufs_a)
        stage(t + 2, bufs_a, t + 1, bufs_b)
        return carry

    lax.fori_loop(0, n_pairs, attn_pair, 0)

    for h in range(N_HEADS_B):
        r0 = h * HEAD_DIM
        o_ref[r0:r0 + HEAD_DIM, :] = (acc_scr[r0:r0 + HEAD_DIM, :] / l_scr[h:h + 1, :]).astype(BF16)


def _mixer_b(qit, wit, qbt, ki, kb, vbt):
    B, _, S = qbt.shape
    topk = min(TOPK_MAX, S // 4)
    assert KT == QB and S % QB == 0 and QB % CHUNK == 0 and topk <= KT
    qspec = lambda w: pl.BlockSpec((None, w, QB), lambda b, j: (b, 0, j))
    return pl.pallas_call(
        functools.partial(_mixer_b_kernel, topk=float(topk)),
        grid=(B, S // QB),
        in_specs=[qspec(WIDTH_IDX), qspec(WI_ROWS), qspec(WIDTH_B),
                  pl.BlockSpec((None, S, LANES), lambda b, j: (b, 0, 0)),
                  pl.BlockSpec((None, S, LANES), lambda b, j: (b, 0, 0)),
                  pl.BlockSpec((None, WIDTH_KV_B, S), lambda b, j: (b, 0, 0))],
        out_specs=qspec(WIDTH_B),
        out_shape=jax.ShapeDtypeStruct((B, WIDTH_B, S), BF16),
        scratch_shapes=[pltpu.VMEM((S + KT, QB), F32),
                        pltpu.VMEM((S + KT, QB), BF16),
                        pltpu.VMEM((N_HEADS_B, KT, QB), F32),
                        pltpu.VMEM((N_HEADS_B, KT, QB), F32),
                        pltpu.VMEM((N_HEADS_B, QB), F32),
                        pltpu.VMEM((N_HEADS_B, QB), F32),
                        pltpu.VMEM((N_HEADS_B, 2 * HEAD_DIM, QB), BF16),
                        pltpu.VMEM((WIDTH_B, QB), F32),
                        pltpu.VMEM((N_HEADS_B, QB), F32),
                        pltpu.VMEM((N_HEADS_B, QB), F32),
                        pltpu.VMEM((1, QB), F32),
                        pltpu.VMEM((1, QB), F32),
                        pltpu.SMEM((1,), jnp.int32)],
        compiler_params=pltpu.CompilerParams(
            dimension_semantics=("parallel", "arbitrary"), vmem_limit_bytes=VMEM_LIMIT),
        name="mixer_b",
    )(qit, wit, qbt, ki, kb, vbt)


def _merge_kernel(x_ref, oat_ref, obt_ref, ga_ref, gb_ref, wa_ref, wb_ref, wo_ref, o_ref):
    ya = _dot_tn(oat_ref[...], wa_ref[...])
    yb = _dot_tn(obt_ref[...], wb_ref[...])
    merged = (jax.nn.sigmoid(ga_ref[...].astype(F32)) * ya
              + jax.nn.sigmoid(gb_ref[...].astype(F32)) * yb)
    o_ref[...] = x_ref[...] + _dot(merged.astype(BF16), wo_ref[...])


def _merge(x3d, oat, obt, ga, gb, wa, wb, wo):
    B, S, _ = x3d.shape
    tm = PROJ_TM
    tmaj = pl.BlockSpec((None, tm, D_MODEL), lambda b, s: (b, s, 0))
    fmaj = lambda w: pl.BlockSpec((None, w, tm), lambda b, s: (b, 0, s))
    full = lambda a: pl.BlockSpec(a.shape, lambda b, s: (0, 0))
    return pl.pallas_call(
        _merge_kernel,
        grid=(B, S // tm),
        in_specs=[tmaj, fmaj(WIDTH_A), fmaj(WIDTH_B), tmaj, tmaj, full(wa), full(wb), full(wo)],
        out_specs=tmaj,
        out_shape=jax.ShapeDtypeStruct((B, S, D_MODEL), F32),
        compiler_params=pltpu.CompilerParams(
            dimension_semantics=("parallel", "parallel"), vmem_limit_bytes=VMEM_LIMIT),
        name="merge",
    )(x3d, oat, obt, ga, gb, wa, wb, wo)


def _rope_tables(seq):
    inv_freq = jnp.power(jnp.float32(ROPE_THETA), -jnp.arange(0, ROT_DIM, 2, dtype=F32) / ROT_DIM)
    ang = jnp.arange(seq, dtype=F32)[:, None] * inv_freq[None, :]
    cos, sin = jnp.cos(ang), jnp.sin(ang)
    ones = jnp.ones((seq, HEAD_DIM - ROT_DIM), F32)
    zeros = jnp.zeros((seq, HEAD_DIM - ROT_DIM), F32)
    zh = jnp.zeros((seq, ROT_HALF), F32)
    c = jnp.concatenate([cos, cos, ones], axis=1)
    sa = jnp.concatenate([-sin, zh, zeros], axis=1)
    sb = jnp.concatenate([zh, sin, zeros], axis=1)
    rep = LANES // HEAD_DIM
    return cos.T, sin.T, jnp.tile(c, (1, rep)), jnp.tile(sa, (1, rep)), jnp.tile(sb, (1, rep))


def _band_bias_ext(rel_bias):
    n_keys = BAND_BLOCKS * QB
    n_h = rel_bias.shape[0]
    lo = jnp.broadcast_to(rel_bias[:, :1], (n_h, QB - 1 - REL_CLIP))
    hi = jnp.broadcast_to(rel_bias[:, -1:], (n_h, BIAS_SPAN - (QB - 1 - REL_CLIP) - (2 * REL_CLIP + 1)))
    ext = jnp.concatenate([lo, rel_bias, hi], axis=1)
    return jnp.roll(ext, -(n_keys - 1), axis=1).astype(F32)


def kernel(x, n1_g, ffn1_w_in, ffn1_w_out, n2_g, w_in, rel_bias, w_branch_a, w_branch_b, w_out,
           n3_g, ffn2_w_in, ffn2_w_out, nf_g):
    B, S, D = x.shape
    depth = n1_g.shape[0]
    cos_t, sin_t, c_tab, sa_tab, sb_tab = _rope_tables(S)
    nf = nf_g.reshape(1, D)
    offs = np.cumsum([0, WIDTH_A, WIDTH_A, WIDTH_A, WIDTH_B, WIDTH_KV_B, WIDTH_KV_B,
                      WIDTH_IDX, IDX_DIM, N_IDX_HEADS, D_MODEL, D_MODEL])
    seg = lambda w, i: w[:, offs[i]:offs[i + 1]]

    for l in range(depth):
        w = w_in[l]
        wt = jnp.concatenate(
            [seg(w, 0), seg(w, 2), seg(w, 3), seg(w, 5), seg(w, 6),
             jnp.pad(seg(w, 8), ((0, 0), (0, LANES - N_IDX_HEADS)))], axis=1).astype(BF16)
        wk = jnp.concatenate(
            [seg(w, 1), seg(w, 4), jnp.pad(seg(w, 7), ((0, 0), (0, LANES - IDX_DIM)))], axis=1).astype(BF16)
        wg = jnp.concatenate([seg(w, 9), seg(w, 10)], axis=1).astype(BF16)

        x2d = _ffn(x.reshape(B * S, D), n1_g[l].reshape(1, D), ffn1_w_in[l].astype(BF16),
                   ffn1_w_out[l].astype(BF16), nf, False)
        x = x2d.reshape(B, S, D)
        (qat, vat, qbt, vbt, qit, wit, ka, kb, ki, ga, gb) = _inproj(
            x, n2_g[l].reshape(1, D), wt, wk, wg, cos_t, sin_t, c_tab, sa_tab, sb_tab)
        oat = _mixer_a(qat, ka, vat, _band_bias_ext(rel_bias[l]))
        obt = _mixer_b(qit, wit, qbt, ki, kb, vbt)
        x = _merge(x, oat, obt, ga, gb, w_branch_a[l].astype(BF16), w_branch_b[l].astype(BF16),
                   w_out[l].astype(BF16))
        last = l == depth - 1
        x2d = _ffn(x.reshape(B * S, D), n3_g[l].reshape(1, D), ffn2_w_in[l].astype(BF16),
                   ffn2_w_out[l].astype(BF16), nf, last)
        x = x2d.reshape(B, S, D)
    return x
```

```python
import functools

import jax
import jax.numpy as jnp
import numpy as np
from jax import lax
from jax.experimental import pallas as pl
from jax.experimental.pallas import tpu as pltpu

F32 = jnp.float32
BF16 = jnp.bfloat16

D_MODEL = 1024
D_FF = 2816
HEAD_DIM = 64
CHUNK = 64
N_PREV_CHUNKS = 8
N_HEADS_A = 8
REL_CLIP = 128
N_HEADS_B = 8
N_KV_B = 2
N_IDX_HEADS = 8
IDX_DIM = 64
TOPK_MAX = 256
ROPE_THETA = 500000.0
ROT_DIM = HEAD_DIM // 4
ROT_HALF = ROT_DIM // 2
EPS = 1e-6
WIDTH_A = N_HEADS_A * HEAD_DIM
WIDTH_B = N_HEADS_B * HEAD_DIM
WIDTH_KV_B = N_KV_B * HEAD_DIM
WIDTH_IDX = N_IDX_HEADS * IDX_DIM

LANES = 128
LOG2E = 1.4426950408889634
QK_SCALE = HEAD_DIM ** -0.5 * LOG2E
IDX_SCALE = IDX_DIM ** -0.5

VMEM_LIMIT = 56 * 1024 * 1024

FFN_TM = 1024
FFN_TF = D_FF // 2
PROJ_TM = 512
QB = 256
KT = 256
BAND_BLOCKS = N_PREV_CHUNKS * CHUNK // QB + 1
BIAS_SPAN = (BAND_BLOCKS + 1) * QB
WI_ROWS = 16
ONES_ROWS = 16
NEG_BIG = -1e30
FINE_STEPS = 17
FINE_STEPS_FIRST = 11
FOLD_CHAINS = 8


def _dot(a, b):
    return jnp.dot(a, b, preferred_element_type=F32)


def _dot_tt(a, b):
    return lax.dot_general(a, b, (((0,), (1,)), ((), ())), preferred_element_type=F32)


def _dot_tn(a, b):
    return lax.dot_general(a, b, (((0,), (0,)), ((), ())), preferred_element_type=F32)


def _rmsnorm(x, g):
    ms = jnp.mean(x * x, axis=-1, keepdims=True)
    return x * lax.rsqrt(ms + EPS) * g


def _ffn_kernel(x_ref, g_ref, wg_ref, wu_ref, wo_ref, gf_ref, o_ref, h_scr, acc_scr, *, final_norm):
    j = pl.program_id(1)

    @pl.when(j == 0)
    def _():
        h_scr[...] = _rmsnorm(x_ref[...], g_ref[...]).astype(BF16)
        acc_scr[...] = jnp.zeros_like(acc_scr)

    h = h_scr[...]
    gate = _dot(h, wg_ref[...])
    up = _dot(h, wu_ref[...])
    a = (gate * jax.nn.sigmoid(gate) * up).astype(BF16)
    acc_scr[...] += _dot(a, wo_ref[...])

    @pl.when(j == pl.num_programs(1) - 1)
    def _():
        y = x_ref[...] + 0.5 * acc_scr[...]
        if final_norm:
            y = _rmsnorm(y, gf_ref[...])
        o_ref[...] = y


def _ffn(x2d, g, w_in_bf, w_out_bf, gf, final_norm):
    T = x2d.shape[0]
    nf = D_FF // FFN_TF
    return pl.pallas_call(
        functools.partial(_ffn_kernel, final_norm=final_norm),
        grid=(T // FFN_TM, nf),
        in_specs=[
            pl.BlockSpec((FFN_TM, D_MODEL), lambda i, j: (i, 0)),
            pl.BlockSpec((1, D_MODEL), lambda i, j: (0, 0)),
            pl.BlockSpec((D_MODEL, FFN_TF), lambda i, j: (0, j)),
            pl.BlockSpec((D_MODEL, FFN_TF), lambda i, j: (0, j + nf)),
            pl.BlockSpec((FFN_TF, D_MODEL), lambda i, j: (j, 0)),
            pl.BlockSpec((1, D_MODEL), lambda i, j: (0, 0)),
        ],
        out_specs=pl.BlockSpec((FFN_TM, D_MODEL), lambda i, j: (i, 0)),
        out_shape=jax.ShapeDtypeStruct((T, D_MODEL), F32),
        scratch_shapes=[pltpu.VMEM((FFN_TM, D_MODEL), BF16), pltpu.VMEM((FFN_TM, D_MODEL), F32)],
        compiler_params=pltpu.CompilerParams(
            dimension_semantics=("parallel", "arbitrary"), vmem_limit_bytes=VMEM_LIMIT),
        name="ffn_final" if final_norm else "ffn",
    )(x2d, g, w_in_bf, w_in_bf, w_out_bf, gf)


def _rope_rows(x, cos, sin, n_heads):
    pieces = []
    for h in range(n_heads):
        r0 = h * HEAD_DIM
        t1 = x[r0:r0 + ROT_HALF]
        t2 = x[r0 + ROT_HALF:r0 + ROT_DIM]
        pieces.append(t1 * cos - t2 * sin)
        pieces.append(t2 * cos + t1 * sin)
        pieces.append(x[r0 + ROT_DIM:r0 + HEAD_DIM])
    return jnp.concatenate(pieces, axis=0)


def _rope_lanes(x, c, sa, sb):
    return x * c + pltpu.roll(x, LANES - ROT_HALF, 1) * sa + pltpu.roll(x, ROT_HALF, 1) * sb


def _inproj_kernel(x_ref, g_ref, wt_ref, wk_ref, wg_ref, cos_ref, sin_ref, c_ref, sa_ref, sb_ref,
                   qat_ref, vat_ref, qbt_ref, vbt_ref, qit_ref, wit_ref,
                   ka_ref, kb_ref, ki_ref, ga_ref, gb_ref):
    h = _rmsnorm(x_ref[...], g_ref[...]).astype(BF16)
    cos = cos_ref[...]
    sin = sin_ref[...]

    ga_ref[...] = _dot(h, wg_ref[:, 0:D_MODEL]).astype(BF16)
    gb_ref[...] = _dot(h, wg_ref[:, D_MODEL:2 * D_MODEL]).astype(BF16)

    r = 0
    t = _dot_tt(wt_ref[:, r:r + WIDTH_A], h)
    qat_ref[...] = (t * QK_SCALE).astype(BF16)
    r += WIDTH_A
    vat_ref[...] = _dot_tt(wt_ref[:, r:r + WIDTH_A], h).astype(BF16)
    r += WIDTH_A
    t = _dot_tt(wt_ref[:, r:r + WIDTH_B], h)
    qbt_ref[...] = (_rope_rows(t, cos, sin, N_HEADS_B) * QK_SCALE).astype(BF16)
    r += WIDTH_B
    vbt_ref[...] = _dot_tt(wt_ref[:, r:r + WIDTH_KV_B], h).astype(BF16)
    r += WIDTH_KV_B
    t = _dot_tt(wt_ref[:, r:r + WIDTH_IDX], h)
    qit_ref[...] = (_rope_rows(t, cos, sin, N_IDX_HEADS) * IDX_SCALE).astype(BF16)
    r += WIDTH_IDX
    wit_ref[...] = _dot_tt(wt_ref[:, r:r + LANES], h)[0:WI_ROWS] * (N_IDX_HEADS ** -0.5)

    ka_ref[...] = _dot(h, wk_ref[:, 0:WIDTH_A]).astype(BF16)
    c, sa, sb = c_ref[...], sa_ref[...], sb_ref[...]
    t = _dot(h, wk_ref[:, WIDTH_A:WIDTH_A + LANES])
    kb_ref[...] = _rope_lanes(t, c, sa, sb).astype(BF16)
    t = _dot(h, wk_ref[:, WIDTH_A + LANES:WIDTH_A + 2 * LANES])
    ki_ref[...] = _rope_lanes(t, c, sa, sb).astype(BF16)


def _inproj(x3d, g, wt, wk, wg, cos_t, sin_t, c_tab, sa_tab, sb_tab):
    B, S, _ = x3d.shape
    tm = PROJ_TM
    full = lambda shape: pl.BlockSpec(shape, lambda b, s: (0,) * len(shape))
    tmaj = lambda w: pl.BlockSpec((None, tm, w), lambda b, s: (b, s, 0))
    fmaj = lambda w: pl.BlockSpec((None, w, tm), lambda b, s: (b, 0, s))
    tshape = lambda w, dt: jax.ShapeDtypeStruct((B, S, w), dt)
    fshape = lambda w, dt: jax.ShapeDtypeStruct((B, w, S), dt)
    return pl.pallas_call(
        _inproj_kernel,
        grid=(B, S // tm),
        in_specs=[
            tmaj(D_MODEL), full((1, D_MODEL)), full(wt.shape), full(wk.shape), full(wg.shape),
            pl.BlockSpec((ROT_HALF, tm), lambda b, s: (0, s)),
            pl.BlockSpec((ROT_HALF, tm), lambda b, s: (0, s)),
            pl.BlockSpec((tm, LANES), lambda b, s: (s, 0)),
            pl.BlockSpec((tm, LANES), lambda b, s: (s, 0)),
            pl.BlockSpec((tm, LANES), lambda b, s: (s, 0)),
        ],
        out_specs=[fmaj(WIDTH_A), fmaj(WIDTH_A), fmaj(WIDTH_B), fmaj(WIDTH_KV_B), fmaj(WIDTH_IDX),
                   fmaj(WI_ROWS), tmaj(WIDTH_A), tmaj(LANES), tmaj(LANES), tmaj(D_MODEL), tmaj(D_MODEL)],
        out_shape=[fshape(WIDTH_A, BF16), fshape(WIDTH_A, BF16), fshape(WIDTH_B, BF16),
                   fshape(WIDTH_KV_B, BF16), fshape(WIDTH_IDX, BF16), fshape(WI_ROWS, F32),
                   tshape(WIDTH_A, BF16), tshape(LANES, BF16), tshape(LANES, BF16),
                   tshape(D_MODEL, BF16), tshape(D_MODEL, BF16)],
        compiler_params=pltpu.CompilerParams(
            dimension_semantics=("parallel", "parallel"), vmem_limit_bytes=VMEM_LIMIT),
        name="inproj",
    )(x3d, g, wt, wk, wg, cos_t, sin_t, c_tab, sa_tab, sb_tab)


def _padded_rhs(qt_h, slot):
    z = jnp.zeros_like(qt_h)
    return jnp.concatenate([qt_h, z] if slot == 0 else [z, qt_h], axis=0)


def _mixer_a_kernel(qt_ref, k0_ref, k1_ref, k2_ref, v0_ref, v1_ref, v2_ref, ext_ref, o_ref,
                    bias_scr, s_scr):
    jb = pl.program_id(1)
    k_refs = (k0_ref, k1_ref, k2_ref)
    v_refs = (v0_ref, v1_ref, v2_ref)

    @pl.when((pl.program_id(0) == 0) & (jb == 0))
    def _():
        qc = lax.broadcasted_iota(jnp.int32, (QB, QB), 1) // CHUNK
        for m in range(BAND_BLOCKS):
            kc = lax.broadcasted_iota(jnp.int32, (QB, QB), 0) // CHUNK + m * (QB // CHUNK)
            valid = (kc >= qc) & (kc <= qc + N_PREV_CHUNKS)
            for h in range(N_HEADS_A):
                rows = jnp.broadcast_to(ext_ref[h:h + 1, :], (QB, BIAS_SPAN))
                skew = pltpu.roll(rows, m * QB, 1, stride=1, stride_axis=0)
                bias_scr[h, m * QB:(m + 1) * QB, :] = jnp.where(valid, skew[:, 0:QB] * LOG2E, -jnp.inf)

    ones_rows = jnp.ones((ONES_ROWS, QB), BF16)

    def logits_block(h, m, rhs):
        lane0 = (h // 2) * LANES
        sm = _dot(k_refs[m][:, lane0:lane0 + LANES], rhs) + bias_scr[h, m * QB:(m + 1) * QB, :]
        if m < BAND_BLOCKS - 1:
            sm = jnp.where(jb + m >= BAND_BLOCKS - 1, sm, -jnp.inf)
        s_scr[h % 2, m] = sm
        return sm.max(axis=0, keepdims=True)

    def pv_block(h, m, mx):
        r0 = h * HEAD_DIM
        p = jnp.exp2((s_scr[h % 2, m] - mx).astype(BF16))
        v_ext = jnp.concatenate([v_refs[m][r0:r0 + HEAD_DIM, :], ones_rows], axis=0)
        return _dot(v_ext, p)

    def head_rhs(h):
        return _padded_rhs(qt_ref[h * HEAD_DIM:(h + 1) * HEAD_DIM, :], h % 2)

    rhs = head_rhs(0)
    mx = functools.reduce(jnp.maximum, [logits_block(0, m, rhs) for m in range(BAND_BLOCKS)])
    for h in range(N_HEADS_A):
        r0 = h * HEAD_DIM
        acc = jnp.zeros((HEAD_DIM + ONES_ROWS, QB), F32)
        next_max = []
        if h + 1 < N_HEADS_A:
            rhs = head_rhs(h + 1)
        for m in range(BAND_BLOCKS):
            if h + 1 < N_HEADS_A:
                next_max.append(logits_block(h + 1, m, rhs))
            acc = acc + pv_block(h, m, mx)
        o_ref[r0:r0 + HEAD_DIM, :] = (acc[0:HEAD_DIM] / acc[HEAD_DIM:HEAD_DIM + 1]).astype(BF16)
        if next_max:
            mx = functools.reduce(jnp.maximum, next_max)


def _mixer_a(qat, ka, vat, bias_ext):
    B, _, S = qat.shape
    assert S % QB == 0 and QB % CHUNK == 0 and (N_PREV_CHUNKS * CHUNK) % QB == 0
    assert QB - 1 >= REL_CLIP
    nb = BAND_BLOCKS - 1
    kspec = lambda m: pl.BlockSpec((None, QB, WIDTH_A), lambda b, j: (b, jnp.maximum(j + m - nb, 0), 0))
    vspec = lambda m: pl.BlockSpec((None, WIDTH_A, QB), lambda b, j: (b, 0, jnp.maximum(j + m - nb, 0)))
    return pl.pallas_call(
        _mixer_a_kernel,
        grid=(B, S // QB),
        in_specs=[pl.BlockSpec((None, WIDTH_A, QB), lambda b, j: (b, 0, j)),
                  kspec(0), kspec(1), kspec(2), vspec(0), vspec(1), vspec(2),
                  pl.BlockSpec(bias_ext.shape, lambda b, j: (0, 0))],
        out_specs=pl.BlockSpec((None, WIDTH_A, QB), lambda b, j: (b, 0, j)),
        out_shape=jax.ShapeDtypeStruct((B, WIDTH_A, S), BF16),
        scratch_shapes=[pltpu.VMEM((N_HEADS_A, BAND_BLOCKS * QB, QB), F32),
                        pltpu.VMEM((2, BAND_BLOCKS, QB, QB), F32)],
        compiler_params=pltpu.CompilerParams(
            dimension_semantics=("arbitrary", "arbitrary"), vmem_limit_bytes=VMEM_LIMIT),
        name="mixer_a",
    )(qat, ka, ka, ka, vat, vat, vat, bias_ext)


def _ordered_code_to_f32(u):
    bits = jnp.where(u < 0, u ^ jnp.int32(-2 ** 31), ~u)
    return lax.bitcast_convert_type(bits, F32)


def _mixer_b_kernel(qit_ref, wit_ref, qbt_ref, ki_ref, kb_ref, vbt_ref, o_ref,
                    score_scr, sb_scr, s_scr, tmax_scr, rhs_scr, acc_scr, m_scr, l_scr,
                    thr_scr, need_scr, ties_scr, *, topk):
    jb = pl.program_id(1)
    n_tiles = jb + 1
    diag0 = pl.multiple_of(jb * KT, KT)
    q_chunk = lax.broadcasted_iota(jnp.int32, (1, QB), 1) // CHUNK
    adm_diag = lax.broadcasted_iota(jnp.int32, (KT, QB), 0) < (q_chunk + 1) * CHUNK
    zero_rows = jnp.zeros((HEAD_DIM, QB), BF16)

    def tile_loop(body, init):
        def wrapped(t, carry):
            return body(pl.multiple_of(t * KT, KT), carry)
        return lax.fori_loop(0, n_tiles, wrapped, init)

    def col_count(hit):
        return hit.reshape(KT // 8, 8, QB).sum(axis=0)

    for h in range(N_IDX_HEADS):
        rhs_scr[h, 0:IDX_DIM, :] = qit_ref[h * IDX_DIM:(h + 1) * IDX_DIM, :]
        rhs_scr[h, IDX_DIM:, :] = zero_rows
    w = wit_ref[...]

    n_pairs = (n_tiles + 1) // 2
    pad0 = pl.multiple_of(n_tiles * KT, KT)

    def pair_loop(body, init):
        def wrapped(t, carry):
            return body(pl.multiple_of(t * 2 * KT, 2 * KT), carry)
        return lax.fori_loop(0, n_pairs, wrapped, init)

    def score_pair(k0, carry):
        ki_t = ki_ref[pl.ds(k0, 2 * KT), :]
        acc = jnp.zeros((2 * KT, QB), F32)
        for h in range(N_IDX_HEADS):
            acc = acc + w[h:h + 1, :] * jnp.maximum(_dot(ki_t, rhs_scr[h]), 0.0)
        score_scr[pl.ds(k0, 2 * KT), :] = acc
        return carry

    pair_loop(score_pair, 0)
    score_scr[pl.ds(diag0, KT), :] = jnp.where(adm_diag, score_scr[pl.ds(diag0, KT), :], -jnp.inf)
    score_scr[pl.ds(pad0, KT), :] = jnp.full((KT, QB), -jnp.inf, F32)

    def round_pair(k0, carry):
        sb_scr[pl.ds(k0, 2 * KT), :] = score_scr[pl.ds(k0, 2 * KT), :].astype(BF16)
        return carry

    pair_loop(round_pair, 0)

    def count_pair(ref, k0, cand, rows, one, zero):
        chains = [None] * FOLD_CHAINS
        view = ref.at[pl.ds(k0, 2 * KT), :]
        for n, r in enumerate(range(0, 2 * KT, rows)):
            hit = jnp.where(view[r:r + rows, :] >= cand, one, zero)
            c = n % FOLD_CHAINS
            chains[c] = hit if chains[c] is None else chains[c] + hit
        while len(chains) > 1:
            chains = [a + b for a, b in zip(chains[0::2], chains[1::2])]
        return chains[0]

    def count_ge_bf16(cand):
        cand = jnp.broadcast_to(cand.astype(BF16), (16, QB))

        def body(k0, cnt):
            return cnt + count_pair(sb_scr, k0, cand, 16, jnp.int16(1), jnp.int16(0))
        cnt = pair_loop(body, jnp.zeros((16, QB), jnp.int16))
        return cnt.astype(jnp.int32).sum(axis=0, keepdims=True)

    def count_ge_f32(cand):
        cand = jnp.broadcast_to(cand, (8, QB))

        def body(k0, cnt):
            return cnt + count_pair(score_scr, k0, cand, 8, 1.0, 0.0)
        cnt = pair_loop(body, jnp.zeros((8, QB), F32))
        return cnt.sum(axis=0, keepdims=True).astype(jnp.int32)

    k_int = jnp.int32(int(topk))
    neg_inf_code = jnp.int32(0x007FFFFF)

    def accept(cnt, code):
        return (cnt >= k_int) | ((code >= 0) & (code <= neg_inf_code))

    def coarse_step(i, t_u):
        cand_u = t_u | lax.shift_left(jnp.int32(1), 31 - i)
        cnt = count_ge_bf16(_ordered_code_to_f32(cand_u))
        return jnp.where(accept(cnt, cand_u), cand_u, t_u)

    t_coarse = lax.fori_loop(0, 16, coarse_step, jnp.zeros((1, QB), jnp.int32))
    base = t_coarse - jnp.int32(1 << 16)

    def fine_step(i, carry):
        off, cnt_at = carry
        cand_off = off | lax.shift_left(jnp.int32(1), 16 - i)
        cand_u = base + cand_off
        cnt = count_ge_f32(_ordered_code_to_f32(cand_u))
        ok = accept(cnt, cand_u)
        return jnp.where(ok, cand_off, off), jnp.where(ok, cnt, cnt_at)

    off, cnt_at = lax.fori_loop(0, FINE_STEPS_FIRST, fine_step,
                                (jnp.zeros((1, QB), jnp.int32), jnp.full((1, QB), -1, jnp.int32)))
    thr_first = _ordered_code_to_f32(base + off)
    thr_scr[...] = thr_first
    ties_scr[0] = jnp.int32(0)
    settled = cnt_at == k_int
    unsettled = jnp.max(jnp.where(settled, 0.0, 1.0)) > 0.0

    def tie_analysis(thr_q):
        thr_scr[...] = thr_q

        def count_both(k0, carry):
            ge, gt = carry
            sc = score_scr[pl.ds(k0, KT), :]
            return (ge + col_count(jnp.where(sc >= thr_q, 1.0, 0.0)),
                    gt + col_count(jnp.where(sc > thr_q, 1.0, 0.0)))

        ge8, gt8 = tile_loop(count_both, (jnp.zeros((8, QB), F32), jnp.zeros((8, QB), F32)))
        cnt_ge = ge8.sum(axis=0, keepdims=True)
        need_scr[...] = topk - gt8.sum(axis=0, keepdims=True)
        n_inadm = (KT - (q_chunk + 1) * CHUNK).astype(F32)
        n_sel = cnt_ge - jnp.where(thr_q == -jnp.inf, n_inadm, 0.0)
        ties_scr[0] = (jnp.max(jnp.where(n_sel > topk, 1.0, 0.0)) > 0.0).astype(jnp.int32)
        return cnt_ge

    @pl.when(unsettled)
    def _():
        hi_code = base + off + jnp.int32(1 << (FINE_STEPS - FINE_STEPS_FIRST))
        hi = _ordered_code_to_f32(hi_code)
        hi = jnp.where((hi != hi) & (hi_code < 0), jnp.inf, hi)

        def below_max(k0, m):
            sc = score_scr[pl.ds(k0, KT), :]
            return jnp.maximum(m, jnp.where(sc < hi, sc, -jnp.inf).reshape(KT // 8, 8, QB).max(axis=0))

        m8 = tile_loop(below_max, jnp.full((8, QB), -jnp.inf, F32))
        thr_try = jnp.where(settled, thr_first, m8.max(axis=0, keepdims=True))
        cnt_try = tie_analysis(thr_try)
        still_short = jnp.max(jnp.where(cnt_try >= topk, 0.0, 1.0)) > 0.0

        @pl.when(still_short)
        def _():
            off_full, _ = lax.fori_loop(FINE_STEPS_FIRST, FINE_STEPS, fine_step, (off, cnt_at))
            tie_analysis(_ordered_code_to_f32(base + off_full))

    thr = thr_scr[...]
    has_ties = ties_scr[0] > 0

    @pl.when(jnp.logical_not(has_ties))
    def _():
        def body(k0, carry):
            score_scr[pl.ds(k0, KT), :] = jnp.where(score_scr[pl.ds(k0, KT), :] >= thr, 0.0, -jnp.inf)
            return carry
        tile_loop(body, 0)

    @pl.when(has_ties)
    def _():
        tri = jnp.where(lax.broadcasted_iota(jnp.int32, (KT, KT), 1)
                        < lax.broadcasted_iota(jnp.int32, (KT, KT), 0), 1.0, 0.0).astype(BF16)

        def body(k0, need):
            sc = score_scr[pl.ds(k0, KT), :]
            eq_f = jnp.where(sc == thr, 1.0, 0.0)
            before = _dot(tri, eq_f.astype(BF16))
            take = jnp.where(sc > thr, 1.0, jnp.where(before < need, eq_f, 0.0))
            score_scr[pl.ds(k0, KT), :] = jnp.where(take > 0.0, 0.0, -jnp.inf)
            return need - eq_f.sum(axis=0, keepdims=True)
        tile_loop(body, need_scr[...])

    score_scr[pl.ds(diag0, KT), :] = jnp.where(adm_diag, score_scr[pl.ds(diag0, KT), :], -jnp.inf)

    for h in range(N_HEADS_B):
        g = h // (N_HEADS_B // N_KV_B)
        q_h = qbt_ref[h * HEAD_DIM:(h + 1) * HEAD_DIM, :]
        rhs_scr[h, 0:HEAD_DIM, :] = q_h if g == 0 else zero_rows
        rhs_scr[h, HEAD_DIM:, :] = zero_rows if g == 0 else q_h
    m_scr[...] = jnp.full_like(m_scr, NEG_BIG)
    l_scr[...] = jnp.zeros_like(l_scr)
    acc_scr[...] = jnp.zeros_like(acc_scr)
    ones_rows = jnp.ones((ONES_ROWS, 2 * KT), BF16)

    def logits_head(h, neg, kb_t):
        s = _dot(kb_t, rhs_scr[h]) + neg
        s_scr[h] = s
        tmax_scr[h:h + 1, :] = s.max(axis=0, keepdims=True)

    def softmax_pv_head(h, v_ext):
        g = h // (N_HEADS_B // N_KV_B)
        m_old = m_scr[h:h + 1, :]
        m_new = jnp.maximum(m_old, tmax_scr[h:h + 1, :])
        m_scr[h:h + 1, :] = m_new
        alpha = jnp.exp2(m_old - m_new)
        p = jnp.exp2((s_scr[h] - m_new).astype(BF16))
        pv = _dot(v_ext[g], p)
        r0 = h * HEAD_DIM
        acc_scr[r0:r0 + HEAD_DIM, :] = alpha * acc_scr[r0:r0 + HEAD_DIM, :] + pv[0:HEAD_DIM]
        l_scr[h:h + 1, :] = alpha * l_scr[h:h + 1, :] + pv[HEAD_DIM:HEAD_DIM + 1]

    def stage(p_next, p_cur):
        k_next = pl.multiple_of(jnp.minimum(p_next, n_pairs - 1) * 2 * KT, 2 * KT)
        neg = score_scr[pl.ds(k_next, 2 * KT), :]
        kb_t = kb_ref[pl.ds(k_next, 2 * KT), :]
        if p_cur is not None:
            k_cur = pl.multiple_of(p_cur * 2 * KT, 2 * KT)
            v_ext = [jnp.concatenate([vbt_ref[g * HEAD_DIM:(g + 1) * HEAD_DIM, pl.ds(k_cur, 2 * KT)],
                                      ones_rows], axis=0) for g in range(N_KV_B)]
        for h in range(N_HEADS_B):
            if p_cur is not None:
                softmax_pv_head(h, v_ext)
            logits_head(h, neg, kb_t)

    stage(0, None)

    def attn_pair(i, carry):
        stage(i + 1, i)
        return carry

    lax.fori_loop(0, n_pairs, attn_pair, 0)

    for h in range(N_HEADS_B):
        r0 = h * HEAD_DIM
        o_ref[r0:r0 + HEAD_DIM, :] = (acc_scr[r0:r0 + HEAD_DIM, :] / l_scr[h:h + 1, :]).astype(BF16)


def _mixer_b(qit, wit, qbt, ki, kb, vbt):
    B, _, S = qbt.shape
    topk = min(TOPK_MAX, S // 4)
    assert KT == QB and S % QB == 0 and QB % CHUNK == 0 and topk <= KT
    qspec = lambda w: pl.BlockSpec((None, w, QB), lambda b, j: (b, 0, j))
    return pl.pallas_call(
        functools.partial(_mixer_b_kernel, topk=float(topk)),
        grid=(B, S // QB),
        in_specs=[qspec(WIDTH_IDX), qspec(WI_ROWS), qspec(WIDTH_B),
                  pl.BlockSpec((None, S, LANES), lambda b, j: (b, 0, 0)),
                  pl.BlockSpec((None, S, LANES), lambda b, j: (b, 0, 0)),
                  pl.BlockSpec((None, WIDTH_KV_B, S), lambda b, j: (b, 0, 0))],
        out_specs=qspec(WIDTH_B),
        out_shape=jax.ShapeDtypeStruct((B, WIDTH_B, S), BF16),
        scratch_shapes=[pltpu.VMEM((S + KT, QB), F32),
                        pltpu.VMEM((S + KT, QB), BF16),
                        pltpu.VMEM((N_HEADS_B, 2 * KT, QB), F32),
                        pltpu.VMEM((N_HEADS_B, QB), F32),
                        pltpu.VMEM((N_HEADS_B, 2 * HEAD_DIM, QB), BF16),
                        pltpu.VMEM((WIDTH_B, QB), F32),
                        pltpu.VMEM((N_HEADS_B, QB), F32),
                        pltpu.VMEM((N_HEADS_B, QB), F32),
                        pltpu.VMEM((1, QB), F32),
                        pltpu.VMEM((1, QB), F32),
                        pltpu.SMEM((1,), jnp.int32)],
        compiler_params=pltpu.CompilerParams(
            dimension_semantics=("parallel", "arbitrary"), vmem_limit_bytes=VMEM_LIMIT),
        name="mixer_b",
    )(qit, wit, qbt, ki, kb, vbt)


def _merge_kernel(x_ref, oat_ref, obt_ref, ga_ref, gb_ref, wa_ref, wb_ref, wo_ref, o_ref):
    ya = _dot_tn(oat_ref[...], wa_ref[...])
    yb = _dot_tn(obt_ref[...], wb_ref[...])
    merged = (jax.nn.sigmoid(ga_ref[...].astype(F32)) * ya
              + jax.nn.sigmoid(gb_ref[...].astype(F32)) * yb)
    o_ref[...] = x_ref[...] + _dot(merged.astype(BF16), wo_ref[...])


def _merge(x3d, oat, obt, ga, gb, wa, wb, wo):
    B, S, _ = x3d.shape
    tm = PROJ_TM
    tmaj = pl.BlockSpec((None, tm, D_MODEL), lambda b, s: (b, s, 0))
    fmaj = lambda w: pl.BlockSpec((None, w, tm), lambda b, s: (b, 0, s))
    full = lambda a: pl.BlockSpec(a.shape, lambda b, s: (0, 0))
    return pl.pallas_call(
        _merge_kernel,
        grid=(B, S // tm),
        in_specs=[tmaj, fmaj(WIDTH_A), fmaj(WIDTH_B), tmaj, tmaj, full(wa), full(wb), full(wo)],
        out_specs=tmaj,
        out_shape=jax.ShapeDtypeStruct((B, S, D_MODEL), F32),
        compiler_params=pltpu.CompilerParams(
            dimension_semantics=("parallel", "parallel"), vmem_limit_bytes=VMEM_LIMIT),
        name="merge",
    )(x3d, oat, obt, ga, gb, wa, wb, wo)


def _rope_tables(seq):
    inv_freq = jnp.power(jnp.float32(ROPE_THETA), -jnp.arange(0, ROT_DIM, 2, dtype=F32) / ROT_DIM)
    ang = jnp.arange(seq, dtype=F32)[:, None] * inv_freq[None, :]
    cos, sin = jnp.cos(ang), jnp.sin(ang)
    ones = jnp.ones((seq, HEAD_DIM - ROT_DIM), F32)
    zeros = jnp.zeros((seq, HEAD_DIM - ROT_DIM), F32)
    zh = jnp.zeros((seq, ROT_HALF), F32)
    c = jnp.concatenate([cos, cos, ones], axis=1)
    sa = jnp.concatenate([-sin, zh, zeros], axis=1)
    sb = jnp.concatenate([zh, sin, zeros], axis=1)
    rep = LANES // HEAD_DIM
    return cos.T, sin.T, jnp.tile(c, (1, rep)), jnp.tile(sa, (1, rep)), jnp.tile(sb, (1, rep))


def _band_bias_ext(rel_bias):
    n_keys = BAND_BLOCKS * QB
    n_h = rel_bias.shape[0]
    lo = jnp.broadcast_to(rel_bias[:, :1], (n_h, QB - 1 - REL_CLIP))
    hi = jnp.broadcast_to(rel_bias[:, -1:], (n_h, BIAS_SPAN - (QB - 1 - REL_CLIP) - (2 * REL_CLIP + 1)))
    ext = jnp.concatenate([lo, rel_bias, hi], axis=1)
    return jnp.roll(ext, -(n_keys - 1), axis=1).astype(F32)


def kernel(x, n1_g, ffn1_w_in, ffn1_w_out, n2_g, w_in, rel_bias, w_branch_a, w_branch_b, w_out,
           n3_g, ffn2_w_in, ffn2_w_out, nf_g):
    B, S, D = x.shape
    depth = n1_g.shape[0]
    cos_t, sin_t, c_tab, sa_tab, sb_tab = _rope_tables(S)
    nf = nf_g.reshape(1, D)
    offs = np.cumsum([0, WIDTH_A, WIDTH_A, WIDTH_A, WIDTH_B, WIDTH_KV_B, WIDTH_KV_B,
                      WIDTH_IDX, IDX_DIM, N_IDX_HEADS, D_MODEL, D_MODEL])
    seg = lambda w, i: w[:, offs[i]:offs[i + 1]]

    for l in range(depth):
        w = w_in[l]
        wt = jnp.concatenate(
            [seg(w, 0), seg(w, 2), seg(w, 3), seg(w, 5), seg(w, 6),
             jnp.pad(seg(w, 8), ((0, 0), (0, LANES - N_IDX_HEADS)))], axis=1).astype(BF16)
        wk = jnp.concatenate(
            [seg(w, 1), seg(w, 4), jnp.pad(seg(w, 7), ((0, 0), (0, LANES - IDX_DIM)))], axis=1).astype(BF16)
        wg = jnp.concatenate([seg(w, 9), seg(w, 10)], axis=1).astype(BF16)

        x2d = _ffn(x.reshape(B * S, D), n1_g[l].reshape(1, D), ffn1_w_in[l].astype(BF16),
                   ffn1_w_out[l].astype(BF16), nf, False)
        x = x2d.reshape(B, S, D)
        (qat, vat, qbt, vbt, qit, wit, ka, kb, ki, ga, gb) = _inproj(
            x, n2_g[l].reshape(1, D), wt, wk, wg, cos_t, sin_t, c_tab, sa_tab, sb_tab)
        oat = _mixer_a(qat, ka, vat, _band_bias_ext(rel_bias[l]))
        obt = _mixer_b(qit, wit, qbt, ki, kb, vbt)
        x = _merge(x, oat, obt, ga, gb, w_branch_a[l].astype(BF16), w_branch_b[l].astype(BF16),
                   w_out[l].astype(BF16))
        last = l == depth - 1
        x2d = _ffn(x.reshape(B * S, D), n3_g[l].reshape(1, D), ffn2_w_in[l].astype(BF16),
                   ffn2_w_out[l].astype(BF16), nf, last)
        x = x2d.reshape(B, S, D)
    return x
```

```python
import functools

import jax
import jax.numpy as jnp
import numpy as np
from jax import lax
from jax.experimental import pallas as pl
from jax.experimental.pallas import tpu as pltpu

F32 = jnp.float32
BF16 = jnp.bfloat16

D_MODEL = 1024
D_FF = 2816
HEAD_DIM = 64
CHUNK = 64
N_PREV_CHUNKS = 8
N_HEADS_A = 8
REL_CLIP = 128
N_HEADS_B = 8
N_KV_B = 2
N_IDX_HEADS = 8
IDX_DIM = 64
TOPK_MAX = 256
ROPE_THETA = 500000.0
ROT_DIM = HEAD_DIM // 4
ROT_HALF = ROT_DIM // 2
EPS = 1e-6
WIDTH_A = N_HEADS_A * HEAD_DIM
WIDTH_B = N_HEADS_B * HEAD_DIM
WIDTH_KV_B = N_KV_B * HEAD_DIM
WIDTH_IDX = N_IDX_HEADS * IDX_DIM

LANES = 128
LOG2E = 1.4426950408889634
QK_SCALE = HEAD_DIM ** -0.5 * LOG2E
IDX_SCALE = IDX_DIM ** -0.5

VMEM_LIMIT = 56 * 1024 * 1024

FFN_TM = 1024
FFN_TF = D_FF // 2
PROJ_TM = 512
QB = 256
KT = 256
BAND_BLOCKS = N_PREV_CHUNKS * CHUNK // QB + 1
BIAS_SPAN = (BAND_BLOCKS + 1) * QB
WI_ROWS = 16
ONES_ROWS = 16
NEG_BIG = -1e30
FINE_STEPS = 17
FINE_STEPS_FIRST = 11
FOLD_CHAINS = 8


def _dot(a, b):
    return jnp.dot(a, b, preferred_element_type=F32)


def _dot_tt(a, b):
    return lax.dot_general(a, b, (((0,), (1,)), ((), ())), preferred_element_type=F32)


def _dot_tn(a, b):
    return lax.dot_general(a, b, (((0,), (0,)), ((), ())), preferred_element_type=F32)


def _rmsnorm(x, g):
    ms = jnp.mean(x * x, axis=-1, keepdims=True)
    return x * lax.rsqrt(ms + EPS) * g


def _ffn_kernel(x_ref, g_ref, wg_ref, wu_ref, wo_ref, gf_ref, o_ref, *, final_norm):
    x = x_ref[...]
    h = _rmsnorm(x, g_ref[...]).astype(BF16)
    gate = _dot(h, wg_ref[...])
    up = _dot(h, wu_ref[...])
    a = (gate * jax.nn.sigmoid(gate) * up).astype(BF16)
    y = x + 0.5 * _dot(a, wo_ref[...])
    if final_norm:
        y = _rmsnorm(y, gf_ref[...])
    o_ref[...] = y


def _ffn(x2d, g, w_in_bf, w_out_bf, gf, final_norm):
    T = x2d.shape[0]
    resident = lambda shape, col: pl.BlockSpec(shape, lambda i: (0, col), pipeline_mode=pl.Buffered(1))
    return pl.pallas_call(
        functools.partial(_ffn_kernel, final_norm=final_norm),
        grid=(T // FFN_TM,),
        in_specs=[
            pl.BlockSpec((FFN_TM, D_MODEL), lambda i: (i, 0)),
            pl.BlockSpec((1, D_MODEL), lambda i: (0, 0)),
            resident((D_MODEL, D_FF), 0),
            resident((D_MODEL, D_FF), 1),
            resident((D_FF, D_MODEL), 0),
            pl.BlockSpec((1, D_MODEL), lambda i: (0, 0)),
        ],
        out_specs=pl.BlockSpec((FFN_TM, D_MODEL), lambda i: (i, 0)),
        out_shape=jax.ShapeDtypeStruct((T, D_MODEL), F32),
        compiler_params=pltpu.CompilerParams(
            dimension_semantics=("parallel",), vmem_limit_bytes=VMEM_LIMIT),
        name="ffn_final" if final_norm else "ffn",
    )(x2d, g, w_in_bf, w_in_bf, w_out_bf, gf)


def _rope_rows(x, cos, sin, n_heads):
    pieces = []
    for h in range(n_heads):
        r0 = h * HEAD_DIM
        t1 = x[r0:r0 + ROT_HALF]
        t2 = x[r0 + ROT_HALF:r0 + ROT_DIM]
        pieces.append(t1 * cos - t2 * sin)
        pieces.append(t2 * cos + t1 * sin)
        pieces.append(x[r0 + ROT_DIM:r0 + HEAD_DIM])
    return jnp.concatenate(pieces, axis=0)


def _rope_lanes(x, c, sa, sb):
    return x * c + pltpu.roll(x, LANES - ROT_HALF, 1) * sa + pltpu.roll(x, ROT_HALF, 1) * sb


def _inproj_kernel(x_ref, g_ref, wt_ref, wk_ref, wg_ref, cos_ref, sin_ref, c_ref, sa_ref, sb_ref,
                   qat_ref, vat_ref, qbt_ref, vbt_ref, qit_ref, wit_ref,
                   ka_ref, kb_ref, ki_ref, ga_ref, gb_ref):
    h = _rmsnorm(x_ref[...], g_ref[...]).astype(BF16)
    cos = cos_ref[...]
    sin = sin_ref[...]

    ga_ref[...] = _dot(h, wg_ref[:, 0:D_MODEL]).astype(BF16)
    gb_ref[...] = _dot(h, wg_ref[:, D_MODEL:2 * D_MODEL]).astype(BF16)

    r = 0
    t = _dot_tt(wt_ref[:, r:r + WIDTH_A], h)
    qat_ref[...] = (t * QK_SCALE).astype(BF16)
    r += WIDTH_A
    vat_ref[...] = _dot_tt(wt_ref[:, r:r + WIDTH_A], h).astype(BF16)
    r += WIDTH_A
    t = _dot_tt(wt_ref[:, r:r + WIDTH_B], h)
    qbt_ref[...] = (_rope_rows(t, cos, sin, N_HEADS_B) * QK_SCALE).astype(BF16)
    r += WIDTH_B
    vbt_ref[...] = _dot_tt(wt_ref[:, r:r + WIDTH_KV_B], h).astype(BF16)
    r += WIDTH_KV_B
    t = _dot_tt(wt_ref[:, r:r + WIDTH_IDX], h)
    qit_ref[...] = (_rope_rows(t, cos, sin, N_IDX_HEADS) * IDX_SCALE).astype(BF16)
    r += WIDTH_IDX
    wit_ref[...] = _dot_tt(wt_ref[:, r:r + LANES], h)[0:WI_ROWS] * (N_IDX_HEADS ** -0.5)

    ka_ref[...] = _dot(h, wk_ref[:, 0:WIDTH_A]).astype(BF16)
    c, sa, sb = c_ref[...], sa_ref[...], sb_ref[...]
    t = _dot(h, wk_ref[:, WIDTH_A:WIDTH_A + LANES])
    kb_ref[...] = _rope_lanes(t, c, sa, sb).astype(BF16)
    t = _dot(h, wk_ref[:, WIDTH_A + LANES:WIDTH_A + 2 * LANES])
    ki_ref[...] = _rope_lanes(t, c, sa, sb).astype(BF16)


def _inproj(x3d, g, wt, wk, wg, cos_t, sin_t, c_tab, sa_tab, sb_tab):
    B, S, _ = x3d.shape
    tm = PROJ_TM
    full = lambda shape: pl.BlockSpec(shape, lambda b, s: (0,) * len(shape))
    tmaj = lambda w: pl.BlockSpec((None, tm, w), lambda b, s: (b, s, 0))
    fmaj = lambda w: pl.BlockSpec((None, w, tm), lambda b, s: (b, 0, s))
    tshape = lambda w, dt: jax.ShapeDtypeStruct((B, S, w), dt)
    fshape = lambda w, dt: jax.ShapeDtypeStruct((B, w, S), dt)
    return pl.pallas_call(
        _inproj_kernel,
        grid=(B, S // tm),
        in_specs=[
            tmaj(D_MODEL), full((1, D_MODEL)), full(wt.shape), full(wk.shape), full(wg.shape),
            pl.BlockSpec((ROT_HALF, tm), lambda b, s: (0, s)),
            pl.BlockSpec((ROT_HALF, tm), lambda b, s: (0, s)),
            pl.BlockSpec((tm, LANES), lambda b, s: (s, 0)),
            pl.BlockSpec((tm, LANES), lambda b, s: (s, 0)),
            pl.BlockSpec((tm, LANES), lambda b, s: (s, 0)),
        ],
        out_specs=[fmaj(WIDTH_A), fmaj(WIDTH_A), fmaj(WIDTH_B), fmaj(WIDTH_KV_B), fmaj(WIDTH_IDX),
                   fmaj(WI_ROWS), tmaj(WIDTH_A), tmaj(LANES), tmaj(LANES), tmaj(D_MODEL), tmaj(D_MODEL)],
        out_shape=[fshape(WIDTH_A, BF16), fshape(WIDTH_A, BF16), fshape(WIDTH_B, BF16),
                   fshape(WIDTH_KV_B, BF16), fshape(WIDTH_IDX, BF16), fshape(WI_ROWS, F32),
                   tshape(WIDTH_A, BF16), tshape(LANES, BF16), tshape(LANES, BF16),
                   tshape(D_MODEL, BF16), tshape(D_MODEL, BF16)],
        compiler_params=pltpu.CompilerParams(
            dimension_semantics=("parallel", "parallel"), vmem_limit_bytes=VMEM_LIMIT),
        name="inproj",
    )(x3d, g, wt, wk, wg, cos_t, sin_t, c_tab, sa_tab, sb_tab)


def _padded_rhs(qt_h, slot):
    z = jnp.zeros_like(qt_h)
    return jnp.concatenate([qt_h, z] if slot == 0 else [z, qt_h], axis=0)


def _mixer_a_kernel(qt_ref, k0_ref, k1_ref, k2_ref, v0_ref, v1_ref, v2_ref, ext_ref, o_ref,
                    bias_scr, s_scr):
    jb = pl.program_id(1)
    k_refs = (k0_ref, k1_ref, k2_ref)
    v_refs = (v0_ref, v1_ref, v2_ref)

    @pl.when((pl.program_id(0) == 0) & (jb == 0))
    def _():
        qc = lax.broadcasted_iota(jnp.int32, (QB, QB), 1) // CHUNK
        for m in range(BAND_BLOCKS):
            kc = lax.broadcasted_iota(jnp.int32, (QB, QB), 0) // CHUNK + m * (QB // CHUNK)
            valid = (kc >= qc) & (kc <= qc + N_PREV_CHUNKS)
            for h in range(N_HEADS_A):
                rows = jnp.broadcast_to(ext_ref[h:h + 1, :], (QB, BIAS_SPAN))
                skew = pltpu.roll(rows, m * QB, 1, stride=1, stride_axis=0)
                bias_scr[h, m * QB:(m + 1) * QB, :] = jnp.where(valid, skew[:, 0:QB] * LOG2E, -jnp.inf)

    ones_rows = jnp.ones((ONES_ROWS, QB), BF16)

    def logits_block(h, m, rhs):
        lane0 = (h // 2) * LANES
        sm = _dot(k_refs[m][:, lane0:lane0 + LANES], rhs) + bias_scr[h, m * QB:(m + 1) * QB, :]
        if m < BAND_BLOCKS - 1:
            sm = jnp.where(jb + m >= BAND_BLOCKS - 1, sm, -jnp.inf)
        s_scr[h % 2, m] = sm
        return sm.max(axis=0, keepdims=True)

    def pv_block(h, m, mx):
        r0 = h * HEAD_DIM
        p = jnp.exp2((s_scr[h % 2, m] - mx).astype(BF16))
        v_ext = jnp.concatenate([v_refs[m][r0:r0 + HEAD_DIM, :], ones_rows], axis=0)
        return _dot(v_ext, p)

    def head_rhs(h):
        return _padded_rhs(qt_ref[h * HEAD_DIM:(h + 1) * HEAD_DIM, :], h % 2)

    rhs = head_rhs(0)
    mx = functools.reduce(jnp.maximum, [logits_block(0, m, rhs) for m in range(BAND_BLOCKS)])
    for h in range(N_HEADS_A):
        r0 = h * HEAD_DIM
        acc = jnp.zeros((HEAD_DIM + ONES_ROWS, QB), F32)
        next_max = []
        if h + 1 < N_HEADS_A:
            rhs = head_rhs(h + 1)
        for m in range(BAND_BLOCKS):
            if h + 1 < N_HEADS_A:
                next_max.append(logits_block(h + 1, m, rhs))
            acc = acc + pv_block(h, m, mx)
        o_ref[r0:r0 + HEAD_DIM, :] = (acc[0:HEAD_DIM] / acc[HEAD_DIM:HEAD_DIM + 1]).astype(BF16)
        if next_max:
            mx = functools.reduce(jnp.maximum, next_max)


def _mixer_a(qat, ka, vat, bias_ext):
    B, _, S = qat.shape
    assert S % QB == 0 and QB % CHUNK == 0 and (N_PREV_CHUNKS * CHUNK) % QB == 0
    assert QB - 1 >= REL_CLIP
    nb = BAND_BLOCKS - 1
    kspec = lambda m: pl.BlockSpec((None, QB, WIDTH_A), lambda b, j: (b, jnp.maximum(j + m - nb, 0), 0))
    vspec = lambda m: pl.BlockSpec((None, WIDTH_A, QB), lambda b, j: (b, 0, jnp.maximum(j + m - nb, 0)))
    return pl.pallas_call(
        _mixer_a_kernel,
        grid=(B, S // QB),
        in_specs=[pl.BlockSpec((None, WIDTH_A, QB), lambda b, j: (b, 0, j)),
                  kspec(0), kspec(1), kspec(2), vspec(0), vspec(1), vspec(2),
                  pl.BlockSpec(bias_ext.shape, lambda b, j: (0, 0))],
        out_specs=pl.BlockSpec((None, WIDTH_A, QB), lambda b, j: (b, 0, j)),
        out_shape=jax.ShapeDtypeStruct((B, WIDTH_A, S), BF16),
        scratch_shapes=[pltpu.VMEM((N_HEADS_A, BAND_BLOCKS * QB, QB), F32),
                        pltpu.VMEM((2, BAND_BLOCKS, QB, QB), F32)],
        compiler_params=pltpu.CompilerParams(
            dimension_semantics=("arbitrary", "arbitrary"), vmem_limit_bytes=VMEM_LIMIT),
        name="mixer_a",
    )(qat, ka, ka, ka, vat, vat, vat, bias_ext)


def _ordered_code_to_f32(u):
    bits = jnp.where(u < 0, u ^ jnp.int32(-2 ** 31), ~u)
    return lax.bitcast_convert_type(bits, F32)


def _mixer_b_kernel(qit_ref, wit_ref, qbt_ref, ki_ref, kb_ref, vbt_ref, o_ref,
                    score_scr, sb_scr, s_scr, tmax_scr, rhs_scr, acc_scr, m_scr, l_scr,
                    thr_scr, need_scr, ties_scr, *, topk):
    jb = pl.program_id(1)
    n_tiles = jb + 1
    diag0 = pl.multiple_of(jb * KT, KT)
    q_chunk = lax.broadcasted_iota(jnp.int32, (1, QB), 1) // CHUNK
    adm_diag = lax.broadcasted_iota(jnp.int32, (KT, QB), 0) < (q_chunk + 1) * CHUNK
    zero_rows = jnp.zeros((HEAD_DIM, QB), BF16)

    def tile_loop(body, init):
        def wrapped(t, carry):
            return body(pl.multiple_of(t * KT, KT), carry)
        return lax.fori_loop(0, n_tiles, wrapped, init)

    def col_count(hit):
        return hit.reshape(KT // 8, 8, QB).sum(axis=0)

    for h in range(N_IDX_HEADS):
        rhs_scr[h, 0:IDX_DIM, :] = qit_ref[h * IDX_DIM:(h + 1) * IDX_DIM, :]
        rhs_scr[h, IDX_DIM:, :] = zero_rows
    w = wit_ref[...]

    n_pairs = (n_tiles + 1) // 2
    pad0 = pl.multiple_of(n_tiles * KT, KT)

    def pair_loop(body, init):
        def wrapped(t, carry):
            return body(pl.multiple_of(t * 2 * KT, 2 * KT), carry)
        return lax.fori_loop(0, n_pairs, wrapped, init)

    def score_pair(k0, carry):
        ki_t = ki_ref[pl.ds(k0, 2 * KT), :]
        acc = jnp.zeros((2 * KT, QB), F32)
        for h in range(N_IDX_HEADS):
            acc = acc + w[h:h + 1, :] * jnp.maximum(_dot(ki_t, rhs_scr[h]), 0.0)
        score_scr[pl.ds(k0, 2 * KT), :] = acc
        return carry

    pair_loop(score_pair, 0)
    score_scr[pl.ds(diag0, KT), :] = jnp.where(adm_diag, score_scr[pl.ds(diag0, KT), :], -jnp.inf)
    score_scr[pl.ds(pad0, KT), :] = jnp.full((KT, QB), -jnp.inf, F32)

    def round_pair(k0, carry):
        sb_scr[pl.ds(k0, 2 * KT), :] = score_scr[pl.ds(k0, 2 * KT), :].astype(BF16)
        return carry

    pair_loop(round_pair, 0)

    def count_pair(ref, k0, cand, rows, one, zero):
        chains = [None] * FOLD_CHAINS
        view = ref.at[pl.ds(k0, 2 * KT), :]
        for n, r in enumerate(range(0, 2 * KT, rows)):
            hit = jnp.where(view[r:r + rows, :] >= cand, one, zero)
            c = n % FOLD_CHAINS
            chains[c] = hit if chains[c] is None else chains[c] + hit
        while len(chains) > 1:
            chains = [a + b for a, b in zip(chains[0::2], chains[1::2])]
        return chains[0]

    def count_ge_bf16(cand):
        cand = jnp.broadcast_to(cand.astype(BF16), (16, QB))

        def body(k0, cnt):
            return cnt + count_pair(sb_scr, k0, cand, 16, jnp.int16(1), jnp.int16(0))
        cnt = pair_loop(body, jnp.zeros((16, QB), jnp.int16))
        return cnt.astype(jnp.int32).sum(axis=0, keepdims=True)

    def count_ge_f32(cand):
        cand = jnp.broadcast_to(cand, (8, QB))

        def body(k0, cnt):
            return cnt + count_pair(score_scr, k0, cand, 8, 1.0, 0.0)
        cnt = pair_loop(body, jnp.zeros((8, QB), F32))
        return cnt.sum(axis=0, keepdims=True).astype(jnp.int32)

    k_int = jnp.int32(int(topk))
    neg_inf_code = jnp.int32(0x007FFFFF)

    def accept(cnt, code):
        return (cnt >= k_int) | ((code >= 0) & (code <= neg_inf_code))

    def coarse_step(i, t_u):
        cand_u = t_u | lax.shift_left(jnp.int32(1), 31 - i)
        cnt = count_ge_bf16(_ordered_code_to_f32(cand_u))
        return jnp.where(accept(cnt, cand_u), cand_u, t_u)

    t_coarse = lax.fori_loop(0, 16, coarse_step, jnp.zeros((1, QB), jnp.int32))
    base = t_coarse - jnp.int32(1 << 16)

    def fine_step(i, carry):
        off, cnt_at = carry
        cand_off = off | lax.shift_left(jnp.int32(1), 16 - i)
        cand_u = base + cand_off
        cnt = count_ge_f32(_ordered_code_to_f32(cand_u))
        ok = accept(cnt, cand_u)
        return jnp.where(ok, cand_off, off), jnp.where(ok, cnt, cnt_at)

    off, cnt_at = lax.fori_loop(0, FINE_STEPS_FIRST, fine_step,
                                (jnp.zeros((1, QB), jnp.int32), jnp.full((1, QB), -1, jnp.int32)))
    thr_first = _ordered_code_to_f32(base + off)
    thr_scr[...] = thr_first
    ties_scr[0] = jnp.int32(0)
    settled = cnt_at == k_int
    unsettled = jnp.max(jnp.where(settled, 0.0, 1.0)) > 0.0

    def tie_analysis(thr_q):
        thr_scr[...] = thr_q

        def count_both(k0, carry):
            ge, gt = carry
            sc = score_scr[pl.ds(k0, KT), :]
            return (ge + col_count(jnp.where(sc >= thr_q, 1.0, 0.0)),
                    gt + col_count(jnp.where(sc > thr_q, 1.0, 0.0)))

        ge8, gt8 = tile_loop(count_both, (jnp.zeros((8, QB), F32), jnp.zeros((8, QB), F32)))
        cnt_ge = ge8.sum(axis=0, keepdims=True)
        need_scr[...] = topk - gt8.sum(axis=0, keepdims=True)
        n_inadm = (KT - (q_chunk + 1) * CHUNK).astype(F32)
        n_sel = cnt_ge - jnp.where(thr_q == -jnp.inf, n_inadm, 0.0)
        ties_scr[0] = (jnp.max(jnp.where(n_sel > topk, 1.0, 0.0)) > 0.0).astype(jnp.int32)
        return cnt_ge

    @pl.when(unsettled)
    def _():
        hi_code = base + off + jnp.int32(1 << (FINE_STEPS - FINE_STEPS_FIRST))
        hi = _ordered_code_to_f32(hi_code)
        hi = jnp.where((hi != hi) & (hi_code < 0), jnp.inf, hi)

        def below_max(k0, m):
            sc = score_scr[pl.ds(k0, KT), :]
            return jnp.maximum(m, jnp.where(sc < hi, sc, -jnp.inf).reshape(KT // 8, 8, QB).max(axis=0))

        m8 = tile_loop(below_max, jnp.full((8, QB), -jnp.inf, F32))
        thr_try = jnp.where(settled, thr_first, m8.max(axis=0, keepdims=True))
        cnt_try = tie_analysis(thr_try)
        still_short = jnp.max(jnp.where(cnt_try >= topk, 0.0, 1.0)) > 0.0

        @pl.when(still_short)
        def _():
            off_full, _ = lax.fori_loop(FINE_STEPS_FIRST, FINE_STEPS, fine_step, (off, cnt_at))
            tie_analysis(_ordered_code_to_f32(base + off_full))

    thr = thr_scr[...]
    has_ties = ties_scr[0] > 0

    @pl.when(jnp.logical_not(has_ties))
    def _():
        def body(k0, carry):
            score_scr[pl.ds(k0, KT), :] = jnp.where(score_scr[pl.ds(k0, KT), :] >= thr, 0.0, -jnp.inf)
            return carry
        tile_loop(body, 0)

    @pl.when(has_ties)
    def _():
        tri = jnp.where(lax.broadcasted_iota(jnp.int32, (KT, KT), 1)
                        < lax.broadcasted_iota(jnp.int32, (KT, KT), 0), 1.0, 0.0).astype(BF16)

        def body(k0, need):
            sc = score_scr[pl.ds(k0, KT), :]
            eq_f = jnp.where(sc == thr, 1.0, 0.0)
            before = _dot(tri, eq_f.astype(BF16))
            take = jnp.where(sc > thr, 1.0, jnp.where(before < need, eq_f, 0.0))
            score_scr[pl.ds(k0, KT), :] = jnp.where(take > 0.0, 0.0, -jnp.inf)
            return need - eq_f.sum(axis=0, keepdims=True)
        tile_loop(body, need_scr[...])

    score_scr[pl.ds(diag0, KT), :] = jnp.where(adm_diag, score_scr[pl.ds(diag0, KT), :], -jnp.inf)

    for h in range(N_HEADS_B):
        g = h // (N_HEADS_B // N_KV_B)
        q_h = qbt_ref[h * HEAD_DIM:(h + 1) * HEAD_DIM, :]
        rhs_scr[h, 0:HEAD_DIM, :] = q_h if g == 0 else zero_rows
        rhs_scr[h, HEAD_DIM:, :] = zero_rows if g == 0 else q_h
    m_scr[...] = jnp.full_like(m_scr, NEG_BIG)
    l_scr[...] = jnp.zeros_like(l_scr)
    acc_scr[...] = jnp.zeros_like(acc_scr)
    ones_rows = jnp.ones((ONES_ROWS, 2 * KT), BF16)

    def logits_head(h, neg, kb_t):
        s = _dot(kb_t, rhs_scr[h]) + neg
        s_scr[h] = s
        tmax_scr[h:h + 1, :] = s.max(axis=0, keepdims=True)

    def softmax_pv_head(h, v_ext):
        g = h // (N_HEADS_B // N_KV_B)
        m_old = m_scr[h:h + 1, :]
        m_new = jnp.maximum(m_old, tmax_scr[h:h + 1, :])
        m_scr[h:h + 1, :] = m_new
        alpha = jnp.exp2(m_old - m_new)
        p = jnp.exp2((s_scr[h] - m_new).astype(BF16))
        pv = _dot(v_ext[g], p)
        r0 = h * HEAD_DIM
        acc_scr[r0:r0 + HEAD_DIM, :] = alpha * acc_scr[r0:r0 + HEAD_DIM, :] + pv[0:HEAD_DIM]
        l_scr[h:h + 1, :] = alpha * l_scr[h:h + 1, :] + pv[HEAD_DIM:HEAD_DIM + 1]

    def stage(p_next, p_cur):
        k_next = pl.multiple_of(jnp.minimum(p_next, n_pairs - 1) * 2 * KT, 2 * KT)
        neg = score_scr[pl.ds(k_next, 2 * KT), :]
        kb_t = kb_ref[pl.ds(k_next, 2 * KT), :]
        if p_cur is not None:
            k_cur = pl.multiple_of(p_cur * 2 * KT, 2 * KT)
            v_ext = [jnp.concatenate([vbt_ref[g * HEAD_DIM:(g + 1) * HEAD_DIM, pl.ds(k_cur, 2 * KT)],
                                      ones_rows], axis=0) for g in range(N_KV_B)]
        for h in range(N_HEADS_B):
            if p_cur is not None:
                softmax_pv_head(h, v_ext)
            logits_head(h, neg, kb_t)

    stage(0, None)

    def attn_pair(i, carry):
        stage(i + 1, i)
        return carry

    lax.fori_loop(0, n_pairs, attn_pair, 0)

    for h in range(N_HEADS_B):
        r0 = h * HEAD_DIM
        o_ref[r0:r0 + HEAD_DIM, :] = (acc_scr[r0:r0 + HEAD_DIM, :] / l_scr[h:h + 1, :]).astype(BF16)


def _mixer_b(qit, wit, qbt, ki, kb, vbt):
    B, _, S = qbt.shape
    topk = min(TOPK_MAX, S // 4)
    assert KT == QB and S % QB == 0 and QB % CHUNK == 0 and topk <= KT
    qspec = lambda w: pl.BlockSpec((None, w, QB), lambda b, j: (b, 0, j))
    return pl.pallas_call(
        functools.partial(_mixer_b_kernel, topk=float(topk)),
        grid=(B, S // QB),
        in_specs=[qspec(WIDTH_IDX), qspec(WI_ROWS), qspec(WIDTH_B),
                  pl.BlockSpec((None, S, LANES), lambda b, j: (b, 0, 0)),
                  pl.BlockSpec((None, S, LANES), lambda b, j: (b, 0, 0)),
                  pl.BlockSpec((None, WIDTH_KV_B, S), lambda b, j: (b, 0, 0))],
        out_specs=qspec(WIDTH_B),
        out_shape=jax.ShapeDtypeStruct((B, WIDTH_B, S), BF16),
        scratch_shapes=[pltpu.VMEM((S + KT, QB), F32),
                        pltpu.VMEM((S + KT, QB), BF16),
                        pltpu.VMEM((N_HEADS_B, 2 * KT, QB), F32),
                        pltpu.VMEM((N_HEADS_B, QB), F32),
                        pltpu.VMEM((N_HEADS_B, 2 * HEAD_DIM, QB), BF16),
                        pltpu.VMEM((WIDTH_B, QB), F32),
                        pltpu.VMEM((N_HEADS_B, QB), F32),
                        pltpu.VMEM((N_HEADS_B, QB), F32),
                        pltpu.VMEM((1, QB), F32),
                        pltpu.VMEM((1, QB), F32),
                        pltpu.SMEM((1,), jnp.int32)],
        compiler_params=pltpu.CompilerParams(
            dimension_semantics=("parallel", "arbitrary"), vmem_limit_bytes=VMEM_LIMIT),
        name="mixer_b",
    )(qit, wit, qbt, ki, kb, vbt)


def _merge_kernel(x_ref, oat_ref, obt_ref, ga_ref, gb_ref, wa_ref, wb_ref, wo_ref, o_ref):
    ya = _dot_tn(oat_ref[...], wa_ref[...])
    yb = _dot_tn(obt_ref[...], wb_ref[...])
    merged = (jax.nn.sigmoid(ga_ref[...].astype(F32)) * ya
              + jax.nn.sigmoid(gb_ref[...].astype(F32)) * yb)
    o_ref[...] = x_ref[...] + _dot(merged.astype(BF16), wo_ref[...])


def _merge(x3d, oat, obt, ga, gb, wa, wb, wo):
    B, S, _ = x3d.shape
    tm = PROJ_TM
    tmaj = pl.BlockSpec((None, tm, D_MODEL), lambda b, s: (b, s, 0))
    fmaj = lambda w: pl.BlockSpec((None, w, tm), lambda b, s: (b, 0, s))
    full = lambda a: pl.BlockSpec(a.shape, lambda b, s: (0, 0))
    return pl.pallas_call(
        _merge_kernel,
        grid=(B, S // tm),
        in_specs=[tmaj, fmaj(WIDTH_A), fmaj(WIDTH_B), tmaj, tmaj, full(wa), full(wb), full(wo)],
        out_specs=tmaj,
        out_shape=jax.ShapeDtypeStruct((B, S, D_MODEL), F32),
        compiler_params=pltpu.CompilerParams(
            dimension_semantics=("parallel", "parallel"), vmem_limit_bytes=VMEM_LIMIT),
        name="merge",
    )(x3d, oat, obt, ga, gb, wa, wb, wo)


def _rope_tables(seq):
    inv_freq = jnp.power(jnp.float32(ROPE_THETA), -jnp.arange(0, ROT_DIM, 2, dtype=F32) / ROT_DIM)
    ang = jnp.arange(seq, dtype=F32)[:, None] * inv_freq[None, :]
    cos, sin = jnp.cos(ang), jnp.sin(ang)
    ones = jnp.ones((seq, HEAD_DIM - ROT_DIM), F32)
    zeros = jnp.zeros((seq, HEAD_DIM - ROT_DIM), F32)
    zh = jnp.zeros((seq, ROT_HALF), F32)
    c = jnp.concatenate([cos, cos, ones], axis=1)
    sa = jnp.concatenate([-sin, zh, zeros], axis=1)
    sb = jnp.concatenate([zh, sin, zeros], axis=1)
    rep = LANES // HEAD_DIM
    return cos.T, sin.T, jnp.tile(c, (1, rep)), jnp.tile(sa, (1, rep)), jnp.tile(sb, (1, rep))


def _band_bias_ext(rel_bias):
    n_keys = BAND_BLOCKS * QB
    n_h = rel_bias.shape[0]
    lo = jnp.broadcast_to(rel_bias[:, :1], (n_h, QB - 1 - REL_CLIP))
    hi = jnp.broadcast_to(rel_bias[:, -1:], (n_h, BIAS_SPAN - (QB - 1 - REL_CLIP) - (2 * REL_CLIP + 1)))
    ext = jnp.concatenate([lo, rel_bias, hi], axis=1)
    return jnp.roll(ext, -(n_keys - 1), axis=1).astype(F32)


def kernel(x, n1_g, ffn1_w_in, ffn1_w_out, n2_g, w_in, rel_bias, w_branch_a, w_branch_b, w_out,
           n3_g, ffn2_w_in, ffn2_w_out, nf_g):
    B, S, D = x.shape
    depth = n1_g.shape[0]
    cos_t, sin_t, c_tab, sa_tab, sb_tab = _rope_tables(S)
    nf = nf_g.reshape(1, D)
    offs = np.cumsum([0, WIDTH_A, WIDTH_A, WIDTH_A, WIDTH_B, WIDTH_KV_B, WIDTH_KV_B,
                      WIDTH_IDX, IDX_DIM, N_IDX_HEADS, D_MODEL, D_MODEL])
    seg = lambda w, i: w[:, offs[i]:offs[i + 1]]

    for l in range(depth):
        w = w_in[l]
        wt = jnp.concatenate(
            [seg(w, 0), seg(w, 2), seg(w, 3), seg(w, 5), seg(w, 6),
             jnp.pad(seg(w, 8), ((0, 0), (0, LANES - N_IDX_HEADS)))], axis=1).astype(BF16)
        wk = jnp.concatenate(
            [seg(w, 1), seg(w, 4), jnp.pad(seg(w, 7), ((0, 0), (0, LANES - IDX_DIM)))], axis=1).astype(BF16)
        wg = jnp.concatenate([seg(w, 9), seg(w, 10)], axis=1).astype(BF16)

        x2d = _ffn(x.reshape(B * S, D), n1_g[l].reshape(1, D), ffn1_w_in[l].astype(BF16),
                   ffn1_w_out[l].astype(BF16), nf, False)
        x = x2d.reshape(B, S, D)
        (qat, vat, qbt, vbt, qit, wit, ka, kb, ki, ga, gb) = _inproj(
            x, n2_g[l].reshape(1, D), wt, wk, wg, cos_t, sin_t, c_tab, sa_tab, sb_tab)
        oat = _mixer_a(qat, ka, vat, _band_bias_ext(rel_bias[l]))
        obt = _mixer_b(qit, wit, qbt, ki, kb, vbt)
        x = _merge(x, oat, obt, ga, gb, w_branch_a[l].astype(BF16), w_branch_b[l].astype(BF16),
                   w_out[l].astype(BF16))
        last = l == depth - 1
        x2d = _ffn(x.reshape(B * S, D), n3_g[l].reshape(1, D), ffn2_w_in[l].astype(BF16),
                   ffn2_w_out[l].astype(BF16), nf, last)
        x = x2d.reshape(B, S, D)
    return x
```

```python
import functools

import jax
import jax.numpy as jnp
import numpy as np
from jax import lax
from jax.experimental import pallas as pl
from jax.experimental.pallas import tpu as pltpu

F32 = jnp.float32
BF16 = jnp.bfloat16

D_MODEL = 1024
D_FF = 2816
HEAD_DIM = 64
CHUNK = 64
N_PREV_CHUNKS = 8
N_HEADS_A = 8
REL_CLIP = 128
N_HEADS_B = 8
N_KV_B = 2
N_IDX_HEADS = 8
IDX_DIM = 64
TOPK_MAX = 256
ROPE_THETA = 500000.0
ROT_DIM = HEAD_DIM // 4
ROT_HALF = ROT_DIM // 2
EPS = 1e-6
WIDTH_A = N_HEADS_A * HEAD_DIM
WIDTH_B = N_HEADS_B * HEAD_DIM
WIDTH_KV_B = N_KV_B * HEAD_DIM
WIDTH_IDX = N_IDX_HEADS * IDX_DIM

LANES = 128
LOG2E = 1.4426950408889634
QK_SCALE = HEAD_DIM ** -0.5 * LOG2E
IDX_SCALE = IDX_DIM ** -0.5

VMEM_LIMIT = 56 * 1024 * 1024

FFN_TM = 1024
FFN_TF = D_FF // 2
PROJ_TM = 1024
QB = 256
KT = 256
BAND_BLOCKS = N_PREV_CHUNKS * CHUNK // QB + 1
BIAS_SPAN = (BAND_BLOCKS + 1) * QB
WI_ROWS = 16
ONES_ROWS = 16
NEG_BIG = -1e30
FINE_STEPS = 17
FINE_STEPS_FIRST = 11
FOLD_CHAINS = 8


def _dot(a, b):
    return jnp.dot(a, b, preferred_element_type=F32)


def _dot_tt(a, b):
    return lax.dot_general(a, b, (((0,), (1,)), ((), ())), preferred_element_type=F32)


def _dot_tn(a, b):
    return lax.dot_general(a, b, (((0,), (0,)), ((), ())), preferred_element_type=F32)


def _rmsnorm(x, g):
    ms = jnp.mean(x * x, axis=-1, keepdims=True)
    return x * lax.rsqrt(ms + EPS) * g


def _ffn_kernel(x_ref, g_ref, wg_ref, wu_ref, wo_ref, gf_ref, o_ref, *, final_norm):
    x = x_ref[...]
    h = _rmsnorm(x, g_ref[...]).astype(BF16)
    gate = _dot(h, wg_ref[...])
    up = _dot(h, wu_ref[...])
    a = (gate * jax.nn.sigmoid(gate) * up).astype(BF16)
    y = x + 0.5 * _dot(a, wo_ref[...])
    if final_norm:
        y = _rmsnorm(y, gf_ref[...])
    o_ref[...] = y


def _ffn(x2d, g, w_in_bf, w_out_bf, gf, final_norm):
    T = x2d.shape[0]
    resident = lambda shape, col: pl.BlockSpec(shape, lambda i: (0, col), pipeline_mode=pl.Buffered(1))
    return pl.pallas_call(
        functools.partial(_ffn_kernel, final_norm=final_norm),
        grid=(T // FFN_TM,),
        in_specs=[
            pl.BlockSpec((FFN_TM, D_MODEL), lambda i: (i, 0)),
            pl.BlockSpec((1, D_MODEL), lambda i: (0, 0)),
            resident((D_MODEL, D_FF), 0),
            resident((D_MODEL, D_FF), 1),
            resident((D_FF, D_MODEL), 0),
            pl.BlockSpec((1, D_MODEL), lambda i: (0, 0)),
        ],
        out_specs=pl.BlockSpec((FFN_TM, D_MODEL), lambda i: (i, 0)),
        out_shape=jax.ShapeDtypeStruct((T, D_MODEL), F32),
        compiler_params=pltpu.CompilerParams(
            dimension_semantics=("parallel",), vmem_limit_bytes=VMEM_LIMIT),
        name="ffn_final" if final_norm else "ffn",
    )(x2d, g, w_in_bf, w_in_bf, w_out_bf, gf)


def _rope_rows(x, cos, sin, n_heads):
    pieces = []
    for h in range(n_heads):
        r0 = h * HEAD_DIM
        t1 = x[r0:r0 + ROT_HALF]
        t2 = x[r0 + ROT_HALF:r0 + ROT_DIM]
        pieces.append(t1 * cos - t2 * sin)
        pieces.append(t2 * cos + t1 * sin)
        pieces.append(x[r0 + ROT_DIM:r0 + HEAD_DIM])
    return jnp.concatenate(pieces, axis=0)


def _rope_lanes(x, c, sa, sb):
    return x * c + pltpu.roll(x, LANES - ROT_HALF, 1) * sa + pltpu.roll(x, ROT_HALF, 1) * sb


def _inproj_kernel(x_ref, g_ref, wt_ref, wk_ref, wg_ref, cos_ref, sin_ref, c_ref, sa_ref, sb_ref,
                   qat_ref, vat_ref, qbt_ref, vbt_ref, qit_ref, wit_ref,
                   ka_ref, kb_ref, ki_ref, ga_ref, gb_ref):
    h = _rmsnorm(x_ref[...], g_ref[...]).astype(BF16)
    cos = cos_ref[...]
    sin = sin_ref[...]

    ga_ref[...] = _dot(h, wg_ref[:, 0:D_MODEL]).astype(BF16)
    gb_ref[...] = _dot(h, wg_ref[:, D_MODEL:2 * D_MODEL]).astype(BF16)

    r = 0
    t = _dot_tt(wt_ref[:, r:r + WIDTH_A], h)
    qat_ref[...] = (t * QK_SCALE).astype(BF16)
    r += WIDTH_A
    vat_ref[...] = _dot_tt(wt_ref[:, r:r + WIDTH_A], h).astype(BF16)
    r += WIDTH_A
    t = _dot_tt(wt_ref[:, r:r + WIDTH_B], h)
    qbt_ref[...] = (_rope_rows(t, cos, sin, N_HEADS_B) * QK_SCALE).astype(BF16)
    r += WIDTH_B
    vbt_ref[...] = _dot_tt(wt_ref[:, r:r + WIDTH_KV_B], h).astype(BF16)
    r += WIDTH_KV_B
    t = _dot_tt(wt_ref[:, r:r + WIDTH_IDX], h)
    qit_ref[...] = (_rope_rows(t, cos, sin, N_IDX_HEADS) * IDX_SCALE).astype(BF16)
    r += WIDTH_IDX
    wit_ref[...] = _dot_tt(wt_ref[:, r:r + LANES], h)[0:WI_ROWS] * (N_IDX_HEADS ** -0.5)

    ka_ref[...] = _dot(h, wk_ref[:, 0:WIDTH_A]).astype(BF16)
    c, sa, sb = c_ref[...], sa_ref[...], sb_ref[...]
    t = _dot(h, wk_ref[:, WIDTH_A:WIDTH_A + LANES])
    kb_ref[...] = _rope_lanes(t, c, sa, sb).astype(BF16)
    t = _dot(h, wk_ref[:, WIDTH_A + LANES:WIDTH_A + 2 * LANES])
    ki_ref[...] = _rope_lanes(t, c, sa, sb).astype(BF16)


def _inproj(x3d, g, wt, wk, wg, cos_t, sin_t, c_tab, sa_tab, sb_tab):
    B, S, _ = x3d.shape
    tm = PROJ_TM
    full = lambda shape: pl.BlockSpec(shape, lambda b, s: (0,) * len(shape), pipeline_mode=pl.Buffered(1))
    tmaj = lambda w: pl.BlockSpec((None, tm, w), lambda b, s: (b, s, 0))
    fmaj = lambda w: pl.BlockSpec((None, w, tm), lambda b, s: (b, 0, s))
    tshape = lambda w, dt: jax.ShapeDtypeStruct((B, S, w), dt)
    fshape = lambda w, dt: jax.ShapeDtypeStruct((B, w, S), dt)
    return pl.pallas_call(
        _inproj_kernel,
        grid=(B, S // tm),
        in_specs=[
            tmaj(D_MODEL), full((1, D_MODEL)), full(wt.shape), full(wk.shape), full(wg.shape),
            pl.BlockSpec((ROT_HALF, tm), lambda b, s: (0, s)),
            pl.BlockSpec((ROT_HALF, tm), lambda b, s: (0, s)),
            pl.BlockSpec((tm, LANES), lambda b, s: (s, 0)),
            pl.BlockSpec((tm, LANES), lambda b, s: (s, 0)),
            pl.BlockSpec((tm, LANES), lambda b, s: (s, 0)),
        ],
        out_specs=[fmaj(WIDTH_A), fmaj(WIDTH_A), fmaj(WIDTH_B), fmaj(WIDTH_KV_B), fmaj(WIDTH_IDX),
                   fmaj(WI_ROWS), tmaj(WIDTH_A), tmaj(LANES), tmaj(LANES), tmaj(D_MODEL), tmaj(D_MODEL)],
        out_shape=[fshape(WIDTH_A, BF16), fshape(WIDTH_A, BF16), fshape(WIDTH_B, BF16),
                   fshape(WIDTH_KV_B, BF16), fshape(WIDTH_IDX, BF16), fshape(WI_ROWS, F32),
                   tshape(WIDTH_A, BF16), tshape(LANES, BF16), tshape(LANES, BF16),
                   tshape(D_MODEL, BF16), tshape(D_MODEL, BF16)],
        compiler_params=pltpu.CompilerParams(
            dimension_semantics=("parallel", "parallel"), vmem_limit_bytes=VMEM_LIMIT),
        name="inproj",
    )(x3d, g, wt, wk, wg, cos_t, sin_t, c_tab, sa_tab, sb_tab)


def _padded_rhs(qt_h, slot):
    z = jnp.zeros_like(qt_h)
    return jnp.concatenate([qt_h, z] if slot == 0 else [z, qt_h], axis=0)


def _mixer_a_kernel(qt_ref, k0_ref, k1_ref, k2_ref, v0_ref, v1_ref, v2_ref, ext_ref, o_ref,
                    bias_scr, s_scr):
    jb = pl.program_id(1)
    k_refs = (k0_ref, k1_ref, k2_ref)
    v_refs = (v0_ref, v1_ref, v2_ref)

    @pl.when((pl.program_id(0) == 0) & (jb == 0))
    def _():
        qc = lax.broadcasted_iota(jnp.int32, (QB, QB), 1) // CHUNK
        for m in range(BAND_BLOCKS):
            kc = lax.broadcasted_iota(jnp.int32, (QB, QB), 0) // CHUNK + m * (QB // CHUNK)
            valid = (kc >= qc) & (kc <= qc + N_PREV_CHUNKS)
            for h in range(N_HEADS_A):
                rows = jnp.broadcast_to(ext_ref[h:h + 1, :], (QB, BIAS_SPAN))
                skew = pltpu.roll(rows, m * QB, 1, stride=1, stride_axis=0)
                bias_scr[h, m * QB:(m + 1) * QB, :] = jnp.where(valid, skew[:, 0:QB] * LOG2E, -jnp.inf)

    ones_rows = jnp.ones((ONES_ROWS, QB), BF16)

    def logits_block(h, m, rhs):
        lane0 = (h // 2) * LANES
        sm = _dot(k_refs[m][:, lane0:lane0 + LANES], rhs) + bias_scr[h, m * QB:(m + 1) * QB, :]
        if m < BAND_BLOCKS - 1:
            sm = jnp.where(jb + m >= BAND_BLOCKS - 1, sm, -jnp.inf)
        s_scr[h % 2, m] = sm
        return sm.max(axis=0, keepdims=True)

    def pv_block(h, m, mx):
        r0 = h * HEAD_DIM
        p = jnp.exp2((s_scr[h % 2, m] - mx).astype(BF16))
        v_ext = jnp.concatenate([v_refs[m][r0:r0 + HEAD_DIM, :], ones_rows], axis=0)
        return _dot(v_ext, p)

    def head_rhs(h):
        return _padded_rhs(qt_ref[h * HEAD_DIM:(h + 1) * HEAD_DIM, :], h % 2)

    rhs = head_rhs(0)
    mx = functools.reduce(jnp.maximum, [logits_block(0, m, rhs) for m in range(BAND_BLOCKS)])
    for h in range(N_HEADS_A):
        r0 = h * HEAD_DIM
        acc = jnp.zeros((HEAD_DIM + ONES_ROWS, QB), F32)
        next_max = []
        if h + 1 < N_HEADS_A:
            rhs = head_rhs(h + 1)
        for m in range(BAND_BLOCKS):
            if h + 1 < N_HEADS_A:
                next_max.append(logits_block(h + 1, m, rhs))
            acc = acc + pv_block(h, m, mx)
        o_ref[r0:r0 + HEAD_DIM, :] = (acc[0:HEAD_DIM] / acc[HEAD_DIM:HEAD_DIM + 1]).astype(BF16)
        if next_max:
            mx = functools.reduce(jnp.maximum, next_max)


def _mixer_a(qat, ka, vat, bias_ext):
    B, _, S = qat.shape
    assert S % QB == 0 and QB % CHUNK == 0 and (N_PREV_CHUNKS * CHUNK) % QB == 0
    assert QB - 1 >= REL_CLIP
    nb = BAND_BLOCKS - 1
    kspec = lambda m: pl.BlockSpec((None, QB, WIDTH_A), lambda b, j: (b, jnp.maximum(j + m - nb, 0), 0))
    vspec = lambda m: pl.BlockSpec((None, WIDTH_A, QB), lambda b, j: (b, 0, jnp.maximum(j + m - nb, 0)))
    return pl.pallas_call(
        _mixer_a_kernel,
        grid=(B, S // QB),
        in_specs=[pl.BlockSpec((None, WIDTH_A, QB), lambda b, j: (b, 0, j)),
                  kspec(0), kspec(1), kspec(2), vspec(0), vspec(1), vspec(2),
                  pl.BlockSpec(bias_ext.shape, lambda b, j: (0, 0))],
        out_specs=pl.BlockSpec((None, WIDTH_A, QB), lambda b, j: (b, 0, j)),
        out_shape=jax.ShapeDtypeStruct((B, WIDTH_A, S), BF16),
        scratch_shapes=[pltpu.VMEM((N_HEADS_A, BAND_BLOCKS * QB, QB), F32),
                        pltpu.VMEM((2, BAND_BLOCKS, QB, QB), F32)],
        compiler_params=pltpu.CompilerParams(
            dimension_semantics=("arbitrary", "arbitrary"), vmem_limit_bytes=VMEM_LIMIT),
        name="mixer_a",
    )(qat, ka, ka, ka, vat, vat, vat, bias_ext)


def _ordered_code_to_f32(u):
    bits = jnp.where(u < 0, u ^ jnp.int32(-2 ** 31), ~u)
    return lax.bitcast_convert_type(bits, F32)


def _mixer_b_kernel(qit_ref, wit_ref, qbt_ref, ki_ref, kb_ref, vbt_ref, o_ref,
                    score_scr, sb_scr, s_scr, tmax_scr, rhs_scr, acc_scr, m_scr, l_scr,
                    thr_scr, need_scr, ties_scr, *, topk):
    jb = pl.program_id(1)
    n_tiles = jb + 1
    diag0 = pl.multiple_of(jb * KT, KT)
    q_chunk = lax.broadcasted_iota(jnp.int32, (1, QB), 1) // CHUNK
    adm_diag = lax.broadcasted_iota(jnp.int32, (KT, QB), 0) < (q_chunk + 1) * CHUNK
    zero_rows = jnp.zeros((HEAD_DIM, QB), BF16)

    def tile_loop(body, init):
        def wrapped(t, carry):
            return body(pl.multiple_of(t * KT, KT), carry)
        return lax.fori_loop(0, n_tiles, wrapped, init)

    def col_count(hit):
        return hit.reshape(KT // 8, 8, QB).sum(axis=0)

    for h in range(N_IDX_HEADS):
        rhs_scr[h, 0:IDX_DIM, :] = qit_ref[h * IDX_DIM:(h + 1) * IDX_DIM, :]
        rhs_scr[h, IDX_DIM:, :] = zero_rows
    w = wit_ref[...]

    n_pairs = (n_tiles + 1) // 2
    pad0 = pl.multiple_of(n_tiles * KT, KT)

    def pair_loop(body, init):
        def wrapped(t, carry):
            return body(pl.multiple_of(t * 2 * KT, 2 * KT), carry)
        return lax.fori_loop(0, n_pairs, wrapped, init)

    def score_pair(k0, carry):
        ki_t = ki_ref[pl.ds(k0, 2 * KT), :]
        acc = jnp.zeros((2 * KT, QB), F32)
        for h in range(N_IDX_HEADS):
            acc = acc + w[h:h + 1, :] * jnp.maximum(_dot(ki_t, rhs_scr[h]), 0.0)
        score_scr[pl.ds(k0, 2 * KT), :] = acc
        return carry

    pair_loop(score_pair, 0)
    score_scr[pl.ds(diag0, KT), :] = jnp.where(adm_diag, score_scr[pl.ds(diag0, KT), :], -jnp.inf)
    score_scr[pl.ds(pad0, KT), :] = jnp.full((KT, QB), -jnp.inf, F32)

    def round_pair(k0, carry):
        sb_scr[pl.ds(k0, 2 * KT), :] = score_scr[pl.ds(k0, 2 * KT), :].astype(BF16)
        return carry

    pair_loop(round_pair, 0)

    def count_pair(ref, k0, cand, rows, one, zero):
        chains = [None] * FOLD_CHAINS
        view = ref.at[pl.ds(k0, 2 * KT), :]
        for n, r in enumerate(range(0, 2 * KT, rows)):
            hit = jnp.where(view[r:r + rows, :] >= cand, one, zero)
            c = n % FOLD_CHAINS
            chains[c] = hit if chains[c] is None else chains[c] + hit
        while len(chains) > 1:
            chains = [a + b for a, b in zip(chains[0::2], chains[1::2])]
        return chains[0]

    def count_ge_bf16(cand):
        cand = jnp.broadcast_to(cand.astype(BF16), (16, QB))

        def body(k0, cnt):
            return cnt + count_pair(sb_scr, k0, cand, 16, jnp.int16(1), jnp.int16(0))
        cnt = pair_loop(body, jnp.zeros((16, QB), jnp.int16))
        return cnt.astype(jnp.int32).sum(axis=0, keepdims=True)

    def count_ge_f32(cand):
        cand = jnp.broadcast_to(cand, (8, QB))

        def body(k0, cnt):
            return cnt + count_pair(score_scr, k0, cand, 8, 1.0, 0.0)
        cnt = pair_loop(body, jnp.zeros((8, QB), F32))
        return cnt.sum(axis=0, keepdims=True).astype(jnp.int32)

    k_int = jnp.int32(int(topk))
    neg_inf_code = jnp.int32(0x007FFFFF)

    def accept(cnt, code):
        return (cnt >= k_int) | ((code >= 0) & (code <= neg_inf_code))

    def coarse_step(i, t_u):
        cand_u = t_u | lax.shift_left(jnp.int32(1), 31 - i)
        cnt = count_ge_bf16(_ordered_code_to_f32(cand_u))
        return jnp.where(accept(cnt, cand_u), cand_u, t_u)

    t_coarse = lax.fori_loop(0, 16, coarse_step, jnp.zeros((1, QB), jnp.int32))
    base = t_coarse - jnp.int32(1 << 16)

    def fine_step(i, carry):
        off, cnt_at = carry
        cand_off = off | lax.shift_left(jnp.int32(1), 16 - i)
        cand_u = base + cand_off
        cnt = count_ge_f32(_ordered_code_to_f32(cand_u))
        ok = accept(cnt, cand_u)
        return jnp.where(ok, cand_off, off), jnp.where(ok, cnt, cnt_at)

    off, cnt_at = lax.fori_loop(0, FINE_STEPS_FIRST, fine_step,
                                (jnp.zeros((1, QB), jnp.int32), jnp.full((1, QB), -1, jnp.int32)))
    thr_first = _ordered_code_to_f32(base + off)
    thr_scr[...] = thr_first
    ties_scr[0] = jnp.int32(0)
    settled = cnt_at == k_int
    unsettled = jnp.max(jnp.where(settled, 0.0, 1.0)) > 0.0

    def tie_analysis(thr_q):
        thr_scr[...] = thr_q

        def count_both(k0, carry):
            ge, gt = carry
            sc = score_scr[pl.ds(k0, KT), :]
            return (ge + col_count(jnp.where(sc >= thr_q, 1.0, 0.0)),
                    gt + col_count(jnp.where(sc > thr_q, 1.0, 0.0)))

        ge8, gt8 = tile_loop(count_both, (jnp.zeros((8, QB), F32), jnp.zeros((8, QB), F32)))
        cnt_ge = ge8.sum(axis=0, keepdims=True)
        need_scr[...] = topk - gt8.sum(axis=0, keepdims=True)
        n_inadm = (KT - (q_chunk + 1) * CHUNK).astype(F32)
        n_sel = cnt_ge - jnp.where(thr_q == -jnp.inf, n_inadm, 0.0)
        ties_scr[0] = (jnp.max(jnp.where(n_sel > topk, 1.0, 0.0)) > 0.0).astype(jnp.int32)
        return cnt_ge

    @pl.when(unsettled)
    def _():
        hi_code = base + off + jnp.int32(1 << (FINE_STEPS - FINE_STEPS_FIRST))
        hi = _ordered_code_to_f32(hi_code)
        hi = jnp.where((hi != hi) & (hi_code < 0), jnp.inf, hi)

        def below_max(k0, m):
            sc = score_scr[pl.ds(k0, KT), :]
            return jnp.maximum(m, jnp.where(sc < hi, sc, -jnp.inf).reshape(KT // 8, 8, QB).max(axis=0))

        m8 = tile_loop(below_max, jnp.full((8, QB), -jnp.inf, F32))
        thr_try = jnp.where(settled, thr_first, m8.max(axis=0, keepdims=True))
        cnt_try = tie_analysis(thr_try)
        still_short = jnp.max(jnp.where(cnt_try >= topk, 0.0, 1.0)) > 0.0

        @pl.when(still_short)
        def _():
            off_full, _ = lax.fori_loop(FINE_STEPS_FIRST, FINE_STEPS, fine_step, (off, cnt_at))
            tie_analysis(_ordered_code_to_f32(base + off_full))

    thr = thr_scr[...]
    has_ties = ties_scr[0] > 0

    @pl.when(jnp.logical_not(has_ties))
    def _():
        def body(k0, carry):
            score_scr[pl.ds(k0, KT), :] = jnp.where(score_scr[pl.ds(k0, KT), :] >= thr, 0.0, -jnp.inf)
            return carry
        tile_loop(body, 0)

    @pl.when(has_ties)
    def _():
        tri = jnp.where(lax.broadcasted_iota(jnp.int32, (KT, KT), 1)
                        < lax.broadcasted_iota(jnp.int32, (KT, KT), 0), 1.0, 0.0).astype(BF16)

        def body(k0, need):
            sc = score_scr[pl.ds(k0, KT), :]
            eq_f = jnp.where(sc == thr, 1.0, 0.0)
            before = _dot(tri, eq_f.astype(BF16))
            take = jnp.where(sc > thr, 1.0, jnp.where(before < need, eq_f, 0.0))
            score_scr[pl.ds(k0, KT), :] = jnp.where(take > 0.0, 0.0, -jnp.inf)
            return need - eq_f.sum(axis=0, keepdims=True)
        tile_loop(body, need_scr[...])

    score_scr[pl.ds(diag0, KT), :] = jnp.where(adm_diag, score_scr[pl.ds(diag0, KT), :], -jnp.inf)

    for h in range(N_HEADS_B):
        g = h // (N_HEADS_B // N_KV_B)
        q_h = qbt_ref[h * HEAD_DIM:(h + 1) * HEAD_DIM, :]
        rhs_scr[h, 0:HEAD_DIM, :] = q_h if g == 0 else zero_rows
        rhs_scr[h, HEAD_DIM:, :] = zero_rows if g == 0 else q_h
    m_scr[...] = jnp.full_like(m_scr, NEG_BIG)
    l_scr[...] = jnp.zeros_like(l_scr)
    acc_scr[...] = jnp.zeros_like(acc_scr)
    ones_rows = jnp.ones((ONES_ROWS, 2 * KT), BF16)

    def logits_head(h, neg, kb_t):
        s = _dot(kb_t, rhs_scr[h]) + neg
        s_scr[h] = s
        tmax_scr[h:h + 1, :] = s.max(axis=0, keepdims=True)

    def softmax_pv_head(h, v_ext):
        g = h // (N_HEADS_B // N_KV_B)
        m_old = m_scr[h:h + 1, :]
        m_new = jnp.maximum(m_old, tmax_scr[h:h + 1, :])
        m_scr[h:h + 1, :] = m_new
        alpha = jnp.exp2(m_old - m_new)
        p = jnp.exp2((s_scr[h] - m_new).astype(BF16))
        pv = _dot(v_ext[g], p)
        r0 = h * HEAD_DIM
        acc_scr[r0:r0 + HEAD_DIM, :] = alpha * acc_scr[r0:r0 + HEAD_DIM, :] + pv[0:HEAD_DIM]
        l_scr[h:h + 1, :] = alpha * l_scr[h:h + 1, :] + pv[HEAD_DIM:HEAD_DIM + 1]

    def stage(p_next, p_cur):
        k_next = pl.multiple_of(jnp.minimum(p_next, n_pairs - 1) * 2 * KT, 2 * KT)
        neg = score_scr[pl.ds(k_next, 2 * KT), :]
        kb_t = kb_ref[pl.ds(k_next, 2 * KT), :]
        if p_cur is not None:
            k_cur = pl.multiple_of(p_cur * 2 * KT, 2 * KT)
            v_ext = [jnp.concatenate([vbt_ref[g * HEAD_DIM:(g + 1) * HEAD_DIM, pl.ds(k_cur, 2 * KT)],
                                      ones_rows], axis=0) for g in range(N_KV_B)]
        for h in range(N_HEADS_B):
            if p_cur is not None:
                softmax_pv_head(h, v_ext)
            logits_head(h, neg, kb_t)

    stage(0, None)

    def attn_pair(i, carry):
        stage(i + 1, i)
        return carry

    lax.fori_loop(0, n_pairs, attn_pair, 0)

    for h in range(N_HEADS_B):
        r0 = h * HEAD_DIM
        o_ref[r0:r0 + HEAD_DIM, :] = (acc_scr[r0:r0 + HEAD_DIM, :] / l_scr[h:h + 1, :]).astype(BF16)


def _mixer_b(qit, wit, qbt, ki, kb, vbt):
    B, _, S = qbt.shape
    topk = min(TOPK_MAX, S // 4)
    assert KT == QB and S % QB == 0 and QB % CHUNK == 0 and topk <= KT
    qspec = lambda w: pl.BlockSpec((None, w, QB), lambda b, j: (b, 0, j))
    return pl.pallas_call(
        functools.partial(_mixer_b_kernel, topk=float(topk)),
        grid=(B, S // QB),
        in_specs=[qspec(WIDTH_IDX), qspec(WI_ROWS), qspec(WIDTH_B),
                  pl.BlockSpec((None, S, LANES), lambda b, j: (b, 0, 0)),
                  pl.BlockSpec((None, S, LANES), lambda b, j: (b, 0, 0)),
                  pl.BlockSpec((None, WIDTH_KV_B, S), lambda b, j: (b, 0, 0))],
        out_specs=qspec(WIDTH_B),
        out_shape=jax.ShapeDtypeStruct((B, WIDTH_B, S), BF16),
        scratch_shapes=[pltpu.VMEM((S + KT, QB), F32),
                        pltpu.VMEM((S + KT, QB), BF16),
                        pltpu.VMEM((N_HEADS_B, 2 * KT, QB), F32),
                        pltpu.VMEM((N_HEADS_B, QB), F32),
                        pltpu.VMEM((N_HEADS_B, 2 * HEAD_DIM, QB), BF16),
                        pltpu.VMEM((WIDTH_B, QB), F32),
                        pltpu.VMEM((N_HEADS_B, QB), F32),
                        pltpu.VMEM((N_HEADS_B, QB), F32),
                        pltpu.VMEM((1, QB), F32),
                        pltpu.VMEM((1, QB), F32),
                        pltpu.SMEM((1,), jnp.int32)],
        compiler_params=pltpu.CompilerParams(
            dimension_semantics=("parallel", "arbitrary"), vmem_limit_bytes=VMEM_LIMIT),
        name="mixer_b",
    )(qit, wit, qbt, ki, kb, vbt)


def _merge_kernel(x_ref, oat_ref, obt_ref, ga_ref, gb_ref, wa_ref, wb_ref, wo_ref, o_ref):
    ya = _dot_tn(oat_ref[...], wa_ref[...])
    yb = _dot_tn(obt_ref[...], wb_ref[...])
    merged = (jax.nn.sigmoid(ga_ref[...].astype(F32)) * ya
              + jax.nn.sigmoid(gb_ref[...].astype(F32)) * yb)
    o_ref[...] = x_ref[...] + _dot(merged.astype(BF16), wo_ref[...])


def _merge(x3d, oat, obt, ga, gb, wa, wb, wo):
    B, S, _ = x3d.shape
    tm = PROJ_TM
    tmaj = pl.BlockSpec((None, tm, D_MODEL), lambda b, s: (b, s, 0))
    fmaj = lambda w: pl.BlockSpec((None, w, tm), lambda b, s: (b, 0, s))
    full = lambda a: pl.BlockSpec(a.shape, lambda b, s: (0, 0))
    return pl.pallas_call(
        _merge_kernel,
        grid=(B, S // tm),
        in_specs=[tmaj, fmaj(WIDTH_A), fmaj(WIDTH_B), tmaj, tmaj, full(wa), full(wb), full(wo)],
        out_specs=tmaj,
        out_shape=jax.ShapeDtypeStruct((B, S, D_MODEL), F32),
        compiler_params=pltpu.CompilerParams(
            dimension_semantics=("parallel", "parallel"), vmem_limit_bytes=VMEM_LIMIT),
        name="merge",
    )(x3d, oat, obt, ga, gb, wa, wb, wo)


def _rope_tables(seq):
    inv_freq = jnp.power(jnp.float32(ROPE_THETA), -jnp.arange(0, ROT_DIM, 2, dtype=F32) / ROT_DIM)
    ang = jnp.arange(seq, dtype=F32)[:, None] * inv_freq[None, :]
    cos, sin = jnp.cos(ang), jnp.sin(ang)
    ones = jnp.ones((seq, HEAD_DIM - ROT_DIM), F32)
    zeros = jnp.zeros((seq, HEAD_DIM - ROT_DIM), F32)
    zh = jnp.zeros((seq, ROT_HALF), F32)
    c = jnp.concatenate([cos, cos, ones], axis=1)
    sa = jnp.concatenate([-sin, zh, zeros], axis=1)
    sb = jnp.concatenate([zh, sin, zeros], axis=1)
    rep = LANES // HEAD_DIM
    return cos.T, sin.T, jnp.tile(c, (1, rep)), jnp.tile(sa, (1, rep)), jnp.tile(sb, (1, rep))


def _band_bias_ext(rel_bias):
    n_keys = BAND_BLOCKS * QB
    n_h = rel_bias.shape[0]
    lo = jnp.broadcast_to(rel_bias[:, :1], (n_h, QB - 1 - REL_CLIP))
    hi = jnp.broadcast_to(rel_bias[:, -1:], (n_h, BIAS_SPAN - (QB - 1 - REL_CLIP) - (2 * REL_CLIP + 1)))
    ext = jnp.concatenate([lo, rel_bias, hi], axis=1)
    return jnp.roll(ext, -(n_keys - 1), axis=1).astype(F32)


def kernel(x, n1_g, ffn1_w_in, ffn1_w_out, n2_g, w_in, rel_bias, w_branch_a, w_branch_b, w_out,
           n3_g, ffn2_w_in, ffn2_w_out, nf_g):
    B, S, D = x.shape
    depth = n1_g.shape[0]
    cos_t, sin_t, c_tab, sa_tab, sb_tab = _rope_tables(S)
    nf = nf_g.reshape(1, D)
    offs = np.cumsum([0, WIDTH_A, WIDTH_A, WIDTH_A, WIDTH_B, WIDTH_KV_B, WIDTH_KV_B,
                      WIDTH_IDX, IDX_DIM, N_IDX_HEADS, D_MODEL, D_MODEL])
    seg = lambda w, i: w[:, offs[i]:offs[i + 1]]

    for l in range(depth):
        w = w_in[l]
        wt = jnp.concatenate(
            [seg(w, 0), seg(w, 2), seg(w, 3), seg(w, 5), seg(w, 6),
             jnp.pad(seg(w, 8), ((0, 0), (0, LANES - N_IDX_HEADS)))], axis=1).astype(BF16)
        wk = jnp.concatenate(
            [seg(w, 1), seg(w, 4), jnp.pad(seg(w, 7), ((0, 0), (0, LANES - IDX_DIM)))], axis=1).astype(BF16)
        wg = jnp.concatenate([seg(w, 9), seg(w, 10)], axis=1).astype(BF16)

        x2d = _ffn(x.reshape(B * S, D), n1_g[l].reshape(1, D), ffn1_w_in[l].astype(BF16),
                   ffn1_w_out[l].astype(BF16), nf, False)
        x = x2d.reshape(B, S, D)
        (qat, vat, qbt, vbt, qit, wit, ka, kb, ki, ga, gb) = _inproj(
            x, n2_g[l].reshape(1, D), wt, wk, wg, cos_t, sin_t, c_tab, sa_tab, sb_tab)
        oat = _mixer_a(qat, ka, vat, _band_bias_ext(rel_bias[l]))
        obt = _mixer_b(qit, wit, qbt, ki, kb, vbt)
        x = _merge(x, oat, obt, ga, gb, w_branch_a[l].astype(BF16), w_branch_b[l].astype(BF16),
                   w_out[l].astype(BF16))
        last = l == depth - 1
        x2d = _ffn(x.reshape(B * S, D), n3_g[l].reshape(1, D), ffn2_w_in[l].astype(BF16),
                   ffn2_w_out[l].astype(BF16), nf, last)
        x = x2d.reshape(B, S, D)
    return x
```

```python
import functools

import jax
import jax.numpy as jnp
import numpy as np
from jax import lax
from jax.experimental import pallas as pl
from jax.experimental.pallas import tpu as pltpu

F32 = jnp.float32
BF16 = jnp.bfloat16

D_MODEL = 1024
D_FF = 2816
HEAD_DIM = 64
CHUNK = 64
N_PREV_CHUNKS = 8
N_HEADS_A = 8
REL_CLIP = 128
N_HEADS_B = 8
N_KV_B = 2
N_IDX_HEADS = 8
IDX_DIM = 64
TOPK_MAX = 256
ROPE_THETA = 500000.0
ROT_DIM = HEAD_DIM // 4
ROT_HALF = ROT_DIM // 2
EPS = 1e-6
WIDTH_A = N_HEADS_A * HEAD_DIM
WIDTH_B = N_HEADS_B * HEAD_DIM
WIDTH_KV_B = N_KV_B * HEAD_DIM
WIDTH_IDX = N_IDX_HEADS * IDX_DIM

LANES = 128
LOG2E = 1.4426950408889634
QK_SCALE = HEAD_DIM ** -0.5 * LOG2E
IDX_SCALE = IDX_DIM ** -0.5

VMEM_LIMIT = 56 * 1024 * 1024

FFN_TM = 1024
FFN_TF = D_FF // 2
PROJ_TM = 1024
QB = 256
QBB = 512
KT = 256
BAND_BLOCKS = N_PREV_CHUNKS * CHUNK // QB + 1
BIAS_SPAN = (BAND_BLOCKS + 1) * QB
WI_ROWS = 16
ONES_ROWS = 16
NEG_BIG = -1e30
FINE_STEPS = 17
FINE_STEPS_FIRST = 11
FOLD_CHAINS = 8


def _dot(a, b):
    return jnp.dot(a, b, preferred_element_type=F32)


def _dot_tt(a, b):
    return lax.dot_general(a, b, (((0,), (1,)), ((), ())), preferred_element_type=F32)


def _dot_tn(a, b):
    return lax.dot_general(a, b, (((0,), (0,)), ((), ())), preferred_element_type=F32)


def _rmsnorm(x, g):
    ms = jnp.mean(x * x, axis=-1, keepdims=True)
    return x * lax.rsqrt(ms + EPS) * g


def _ffn_kernel(x_ref, g_ref, wg_ref, wu_ref, wo_ref, gf_ref, o_ref, *, final_norm):
    x = x_ref[...]
    h = _rmsnorm(x, g_ref[...]).astype(BF16)
    gate = _dot(h, wg_ref[...])
    up = _dot(h, wu_ref[...])
    a = (gate * jax.nn.sigmoid(gate) * up).astype(BF16)
    y = x + 0.5 * _dot(a, wo_ref[...])
    if final_norm:
        y = _rmsnorm(y, gf_ref[...])
    o_ref[...] = y


def _ffn(x2d, g, w_in_bf, w_out_bf, gf, final_norm):
    T = x2d.shape[0]
    resident = lambda shape, col: pl.BlockSpec(shape, lambda i: (0, col), pipeline_mode=pl.Buffered(1))
    return pl.pallas_call(
        functools.partial(_ffn_kernel, final_norm=final_norm),
        grid=(T // FFN_TM,),
        in_specs=[
            pl.BlockSpec((FFN_TM, D_MODEL), lambda i: (i, 0)),
            pl.BlockSpec((1, D_MODEL), lambda i: (0, 0)),
            resident((D_MODEL, D_FF), 0),
            resident((D_MODEL, D_FF), 1),
            resident((D_FF, D_MODEL), 0),
            pl.BlockSpec((1, D_MODEL), lambda i: (0, 0)),
        ],
        out_specs=pl.BlockSpec((FFN_TM, D_MODEL), lambda i: (i, 0)),
        out_shape=jax.ShapeDtypeStruct((T, D_MODEL), F32),
        compiler_params=pltpu.CompilerParams(
            dimension_semantics=("parallel",), vmem_limit_bytes=VMEM_LIMIT),
        name="ffn_final" if final_norm else "ffn",
    )(x2d, g, w_in_bf, w_in_bf, w_out_bf, gf)


def _rope_rows(x, cos, sin, n_heads):
    pieces = []
    for h in range(n_heads):
        r0 = h * HEAD_DIM
        t1 = x[r0:r0 + ROT_HALF]
        t2 = x[r0 + ROT_HALF:r0 + ROT_DIM]
        pieces.append(t1 * cos - t2 * sin)
        pieces.append(t2 * cos + t1 * sin)
        pieces.append(x[r0 + ROT_DIM:r0 + HEAD_DIM])
    return jnp.concatenate(pieces, axis=0)


def _rope_lanes(x, c, sa, sb):
    return x * c + pltpu.roll(x, LANES - ROT_HALF, 1) * sa + pltpu.roll(x, ROT_HALF, 1) * sb


def _inproj_kernel(x_ref, g_ref, wt_ref, wk_ref, wg_ref, cos_ref, sin_ref, c_ref, sa_ref, sb_ref,
                   qat_ref, vat_ref, qbt_ref, vbt_ref, qit_ref, wit_ref,
                   ka_ref, kb_ref, ki_ref, ga_ref, gb_ref):
    h = _rmsnorm(x_ref[...], g_ref[...]).astype(BF16)
    cos = cos_ref[...]
    sin = sin_ref[...]

    ga_ref[...] = _dot(h, wg_ref[:, 0:D_MODEL]).astype(BF16)
    gb_ref[...] = _dot(h, wg_ref[:, D_MODEL:2 * D_MODEL]).astype(BF16)

    r = 0
    t = _dot_tt(wt_ref[:, r:r + WIDTH_A], h)
    qat_ref[...] = (t * QK_SCALE).astype(BF16)
    r += WIDTH_A
    vat_ref[...] = _dot_tt(wt_ref[:, r:r + WIDTH_A], h).astype(BF16)
    r += WIDTH_A
    t = _dot_tt(wt_ref[:, r:r + WIDTH_B], h)
    qbt_ref[...] = (_rope_rows(t, cos, sin, N_HEADS_B) * QK_SCALE).astype(BF16)
    r += WIDTH_B
    vbt_ref[...] = _dot_tt(wt_ref[:, r:r + WIDTH_KV_B], h).astype(BF16)
    r += WIDTH_KV_B
    t = _dot_tt(wt_ref[:, r:r + WIDTH_IDX], h)
    qit_ref[...] = (_rope_rows(t, cos, sin, N_IDX_HEADS) * IDX_SCALE).astype(BF16)
    r += WIDTH_IDX
    wit_ref[...] = _dot_tt(wt_ref[:, r:r + LANES], h)[0:WI_ROWS] * (N_IDX_HEADS ** -0.5)

    ka_ref[...] = _dot(h, wk_ref[:, 0:WIDTH_A]).astype(BF16)
    c, sa, sb = c_ref[...], sa_ref[...], sb_ref[...]
    t = _dot(h, wk_ref[:, WIDTH_A:WIDTH_A + LANES])
    kb_ref[...] = _rope_lanes(t, c, sa, sb).astype(BF16)
    t = _dot(h, wk_ref[:, WIDTH_A + LANES:WIDTH_A + 2 * LANES])
    ki_ref[...] = _rope_lanes(t, c, sa, sb).astype(BF16)


def _inproj(x3d, g, wt, wk, wg, cos_t, sin_t, c_tab, sa_tab, sb_tab):
    B, S, _ = x3d.shape
    tm = PROJ_TM
    full = lambda shape: pl.BlockSpec(shape, lambda b, s: (0,) * len(shape), pipeline_mode=pl.Buffered(1))
    tmaj = lambda w: pl.BlockSpec((None, tm, w), lambda b, s: (b, s, 0))
    fmaj = lambda w: pl.BlockSpec((None, w, tm), lambda b, s: (b, 0, s))
    tshape = lambda w, dt: jax.ShapeDtypeStruct((B, S, w), dt)
    fshape = lambda w, dt: jax.ShapeDtypeStruct((B, w, S), dt)
    return pl.pallas_call(
        _inproj_kernel,
        grid=(B, S // tm),
        in_specs=[
            tmaj(D_MODEL), full((1, D_MODEL)), full(wt.shape), full(wk.shape), full(wg.shape),
            pl.BlockSpec((ROT_HALF, tm), lambda b, s: (0, s)),
            pl.BlockSpec((ROT_HALF, tm), lambda b, s: (0, s)),
            pl.BlockSpec((tm, LANES), lambda b, s: (s, 0)),
            pl.BlockSpec((tm, LANES), lambda b, s: (s, 0)),
            pl.BlockSpec((tm, LANES), lambda b, s: (s, 0)),
        ],
        out_specs=[fmaj(WIDTH_A), fmaj(WIDTH_A), fmaj(WIDTH_B), fmaj(WIDTH_KV_B), fmaj(WIDTH_IDX),
                   fmaj(WI_ROWS), tmaj(WIDTH_A), tmaj(LANES), tmaj(LANES), tmaj(D_MODEL), tmaj(D_MODEL)],
        out_shape=[fshape(WIDTH_A, BF16), fshape(WIDTH_A, BF16), fshape(WIDTH_B, BF16),
                   fshape(WIDTH_KV_B, BF16), fshape(WIDTH_IDX, BF16), fshape(WI_ROWS, F32),
                   tshape(WIDTH_A, BF16), tshape(LANES, BF16), tshape(LANES, BF16),
                   tshape(D_MODEL, BF16), tshape(D_MODEL, BF16)],
        compiler_params=pltpu.CompilerParams(
            dimension_semantics=("parallel", "parallel"), vmem_limit_bytes=VMEM_LIMIT),
        name="inproj",
    )(x3d, g, wt, wk, wg, cos_t, sin_t, c_tab, sa_tab, sb_tab)


def _padded_rhs(qt_h, slot):
    z = jnp.zeros_like(qt_h)
    return jnp.concatenate([qt_h, z] if slot == 0 else [z, qt_h], axis=0)


def _mixer_a_kernel(qt_ref, k0_ref, k1_ref, k2_ref, v0_ref, v1_ref, v2_ref, ext_ref, o_ref,
                    bias_scr, s_scr):
    jb = pl.program_id(1)
    k_refs = (k0_ref, k1_ref, k2_ref)
    v_refs = (v0_ref, v1_ref, v2_ref)

    @pl.when((pl.program_id(0) == 0) & (jb == 0))
    def _():
        qc = lax.broadcasted_iota(jnp.int32, (QB, QB), 1) // CHUNK
        for m in range(BAND_BLOCKS):
            kc = lax.broadcasted_iota(jnp.int32, (QB, QB), 0) // CHUNK + m * (QB // CHUNK)
            valid = (kc >= qc) & (kc <= qc + N_PREV_CHUNKS)
            for h in range(N_HEADS_A):
                rows = jnp.broadcast_to(ext_ref[h:h + 1, :], (QB, BIAS_SPAN))
                skew = pltpu.roll(rows, m * QB, 1, stride=1, stride_axis=0)
                bias_scr[h, m * QB:(m + 1) * QB, :] = jnp.where(valid, skew[:, 0:QB] * LOG2E, -jnp.inf)

    ones_rows = jnp.ones((ONES_ROWS, QB), BF16)

    def logits_block(h, m, rhs):
        lane0 = (h // 2) * LANES
        sm = _dot(k_refs[m][:, lane0:lane0 + LANES], rhs) + bias_scr[h, m * QB:(m + 1) * QB, :]
        if m < BAND_BLOCKS - 1:
            sm = jnp.where(jb + m >= BAND_BLOCKS - 1, sm, -jnp.inf)
        s_scr[h % 2, m] = sm
        return sm.max(axis=0, keepdims=True)

    def pv_block(h, m, mx):
        r0 = h * HEAD_DIM
        p = jnp.exp2((s_scr[h % 2, m] - mx).astype(BF16))
        v_ext = jnp.concatenate([v_refs[m][r0:r0 + HEAD_DIM, :], ones_rows], axis=0)
        return _dot(v_ext, p)

    def head_rhs(h):
        return _padded_rhs(qt_ref[h * HEAD_DIM:(h + 1) * HEAD_DIM, :], h % 2)

    rhs = head_rhs(0)
    mx = functools.reduce(jnp.maximum, [logits_block(0, m, rhs) for m in range(BAND_BLOCKS)])
    for h in range(N_HEADS_A):
        r0 = h * HEAD_DIM
        acc = jnp.zeros((HEAD_DIM + ONES_ROWS, QB), F32)
        next_max = []
        if h + 1 < N_HEADS_A:
            rhs = head_rhs(h + 1)
        for m in range(BAND_BLOCKS):
            if h + 1 < N_HEADS_A:
                next_max.append(logits_block(h + 1, m, rhs))
            acc = acc + pv_block(h, m, mx)
        o_ref[r0:r0 + HEAD_DIM, :] = (acc[0:HEAD_DIM] / acc[HEAD_DIM:HEAD_DIM + 1]).astype(BF16)
        if next_max:
            mx = functools.reduce(jnp.maximum, next_max)


def _mixer_a(qat, ka, vat, bias_ext):
    B, _, S = qat.shape
    assert S % QB == 0 and QB % CHUNK == 0 and (N_PREV_CHUNKS * CHUNK) % QB == 0
    assert QB - 1 >= REL_CLIP
    nb = BAND_BLOCKS - 1
    kspec = lambda m: pl.BlockSpec((None, QB, WIDTH_A), lambda b, j: (b, jnp.maximum(j + m - nb, 0), 0))
    vspec = lambda m: pl.BlockSpec((None, WIDTH_A, QB), lambda b, j: (b, 0, jnp.maximum(j + m - nb, 0)))
    return pl.pallas_call(
        _mixer_a_kernel,
        grid=(B, S // QB),
        in_specs=[pl.BlockSpec((None, WIDTH_A, QB), lambda b, j: (b, 0, j)),
                  kspec(0), kspec(1), kspec(2), vspec(0), vspec(1), vspec(2),
                  pl.BlockSpec(bias_ext.shape, lambda b, j: (0, 0))],
        out_specs=pl.BlockSpec((None, WIDTH_A, QB), lambda b, j: (b, 0, j)),
        out_shape=jax.ShapeDtypeStruct((B, WIDTH_A, S), BF16),
        scratch_shapes=[pltpu.VMEM((N_HEADS_A, BAND_BLOCKS * QB, QB), F32),
                        pltpu.VMEM((2, BAND_BLOCKS, QB, QB), F32)],
        compiler_params=pltpu.CompilerParams(
            dimension_semantics=("arbitrary", "arbitrary"), vmem_limit_bytes=VMEM_LIMIT),
        name="mixer_a",
    )(qat, ka, ka, ka, vat, vat, vat, bias_ext)


def _ordered_code_to_f32(u):
    bits = jnp.where(u < 0, u ^ jnp.int32(-2 ** 31), ~u)
    return lax.bitcast_convert_type(bits, F32)


def _mixer_b_kernel(qit_ref, wit_ref, qbt_ref, ki_ref, kb_ref, vbt_ref, o_ref,
                    score_scr, sb_scr, s_scr, tmax_scr, rhs_scr, acc_scr, m_scr, l_scr,
                    thr_scr, need_scr, ties_scr, *, topk):
    jb = pl.program_id(1)
    n_pairs = jb + 1
    n_tiles = 2 * n_pairs
    diag0 = pl.multiple_of(jb * 2 * KT, 2 * KT)
    q_chunk = lax.broadcasted_iota(jnp.int32, (1, QBB), 1) // CHUNK
    adm_diag = lax.broadcasted_iota(jnp.int32, (2 * KT, QBB), 0) < (q_chunk + 1) * CHUNK
    zero_rows = jnp.zeros((HEAD_DIM, QBB), BF16)

    def tile_loop(body, init):
        def wrapped(t, carry):
            return body(pl.multiple_of(t * KT, KT), carry)
        return lax.fori_loop(0, n_tiles, wrapped, init)

    def col_count(hit):
        return hit.reshape(KT // 8, 8, QBB).sum(axis=0)

    for h in range(N_IDX_HEADS):
        rhs_scr[h, 0:IDX_DIM, :] = qit_ref[h * IDX_DIM:(h + 1) * IDX_DIM, :]
        rhs_scr[h, IDX_DIM:, :] = zero_rows
    w = wit_ref[...]

    def pair_loop(body, init):
        def wrapped(t, carry):
            return body(pl.multiple_of(t * 2 * KT, 2 * KT), carry)
        return lax.fori_loop(0, n_pairs, wrapped, init)

    def score_pair(k0, carry):
        ki_t = ki_ref[pl.ds(k0, 2 * KT), :]
        acc = jnp.zeros((2 * KT, QBB), F32)
        for h in range(N_IDX_HEADS):
            acc = acc + w[h:h + 1, :] * jnp.maximum(_dot(ki_t, rhs_scr[h]), 0.0)
        score_scr[pl.ds(k0, 2 * KT), :] = acc
        return carry

    pair_loop(score_pair, 0)
    score_scr[pl.ds(diag0, 2 * KT), :] = jnp.where(adm_diag, score_scr[pl.ds(diag0, 2 * KT), :], -jnp.inf)

    def round_pair(k0, carry):
        sb_scr[pl.ds(k0, 2 * KT), :] = score_scr[pl.ds(k0, 2 * KT), :].astype(BF16)
        return carry

    pair_loop(round_pair, 0)

    def count_pair(ref, k0, cand, rows, one, zero):
        chains = [None] * FOLD_CHAINS
        view = ref.at[pl.ds(k0, 2 * KT), :]
        for n, r in enumerate(range(0, 2 * KT, rows)):
            hit = jnp.where(view[r:r + rows, :] >= cand, one, zero)
            c = n % FOLD_CHAINS
            chains[c] = hit if chains[c] is None else chains[c] + hit
        while len(chains) > 1:
            chains = [a + b for a, b in zip(chains[0::2], chains[1::2])]
        return chains[0]

    def count_ge_bf16(cand):
        cand = jnp.broadcast_to(cand.astype(BF16), (16, QBB))

        def body(k0, cnt):
            return cnt + count_pair(sb_scr, k0, cand, 16, jnp.int16(1), jnp.int16(0))
        cnt = pair_loop(body, jnp.zeros((16, QBB), jnp.int16))
        return cnt.astype(jnp.int32).sum(axis=0, keepdims=True)

    def count_ge_f32(cand):
        cand = jnp.broadcast_to(cand, (8, QBB))

        def body(k0, cnt):
            return cnt + count_pair(score_scr, k0, cand, 8, 1.0, 0.0)
        cnt = pair_loop(body, jnp.zeros((8, QBB), F32))
        return cnt.sum(axis=0, keepdims=True).astype(jnp.int32)

    k_int = jnp.int32(int(topk))
    neg_inf_code = jnp.int32(0x007FFFFF)

    def accept(cnt, code):
        return (cnt >= k_int) | ((code >= 0) & (code <= neg_inf_code))

    def coarse_step(i, t_u):
        cand_u = t_u | lax.shift_left(jnp.int32(1), 31 - i)
        cnt = count_ge_bf16(_ordered_code_to_f32(cand_u))
        return jnp.where(accept(cnt, cand_u), cand_u, t_u)

    t_coarse = lax.fori_loop(0, 16, coarse_step, jnp.zeros((1, QBB), jnp.int32))
    base = t_coarse - jnp.int32(1 << 16)

    def fine_step(i, carry):
        off, cnt_at = carry
        cand_off = off | lax.shift_left(jnp.int32(1), 16 - i)
        cand_u = base + cand_off
        cnt = count_ge_f32(_ordered_code_to_f32(cand_u))
        ok = accept(cnt, cand_u)
        return jnp.where(ok, cand_off, off), jnp.where(ok, cnt, cnt_at)

    off, cnt_at = lax.fori_loop(0, FINE_STEPS_FIRST, fine_step,
                                (jnp.zeros((1, QBB), jnp.int32), jnp.full((1, QBB), -1, jnp.int32)))
    thr_first = _ordered_code_to_f32(base + off)
    thr_scr[...] = thr_first
    ties_scr[0] = jnp.int32(0)
    settled = cnt_at == k_int
    unsettled = jnp.max(jnp.where(settled, 0.0, 1.0)) > 0.0

    def tie_analysis(thr_q):
        thr_scr[...] = thr_q

        def count_both(k0, carry):
            ge, gt = carry
            sc = score_scr[pl.ds(k0, KT), :]
            return (ge + col_count(jnp.where(sc >= thr_q, 1.0, 0.0)),
                    gt + col_count(jnp.where(sc > thr_q, 1.0, 0.0)))

        ge8, gt8 = tile_loop(count_both, (jnp.zeros((8, QBB), F32), jnp.zeros((8, QBB), F32)))
        cnt_ge = ge8.sum(axis=0, keepdims=True)
        need_scr[...] = topk - gt8.sum(axis=0, keepdims=True)
        n_inadm = (2 * KT - (q_chunk + 1) * CHUNK).astype(F32)
        n_sel = cnt_ge - jnp.where(thr_q == -jnp.inf, n_inadm, 0.0)
        ties_scr[0] = (jnp.max(jnp.where(n_sel > topk, 1.0, 0.0)) > 0.0).astype(jnp.int32)
        return cnt_ge

    @pl.when(unsettled)
    def _():
        hi_code = base + off + jnp.int32(1 << (FINE_STEPS - FINE_STEPS_FIRST))
        hi = _ordered_code_to_f32(hi_code)
        hi = jnp.where((hi != hi) & (hi_code < 0), jnp.inf, hi)

        def below_max(k0, m):
            sc = score_scr[pl.ds(k0, KT), :]
            return jnp.maximum(m, jnp.where(sc < hi, sc, -jnp.inf).reshape(KT // 8, 8, QBB).max(axis=0))

        m8 = tile_loop(below_max, jnp.full((8, QBB), -jnp.inf, F32))
        thr_try = jnp.where(settled, thr_first, m8.max(axis=0, keepdims=True))
        cnt_try = tie_analysis(thr_try)
        still_short = jnp.max(jnp.where(cnt_try >= topk, 0.0, 1.0)) > 0.0

        @pl.when(still_short)
        def _():
            off_full, _ = lax.fori_loop(FINE_STEPS_FIRST, FINE_STEPS, fine_step, (off, cnt_at))
            tie_analysis(_ordered_code_to_f32(base + off_full))

    thr = thr_scr[...]
    has_ties = ties_scr[0] > 0

    @pl.when(jnp.logical_not(has_ties))
    def _():
        def body(k0, carry):
            score_scr[pl.ds(k0, KT), :] = jnp.where(score_scr[pl.ds(k0, KT), :] >= thr, 0.0, -jnp.inf)
            return carry
        tile_loop(body, 0)

    @pl.when(has_ties)
    def _():
        tri = jnp.where(lax.broadcasted_iota(jnp.int32, (KT, KT), 1)
                        < lax.broadcasted_iota(jnp.int32, (KT, KT), 0), 1.0, 0.0).astype(BF16)

        def body(k0, need):
            sc = score_scr[pl.ds(k0, KT), :]
            eq_f = jnp.where(sc == thr, 1.0, 0.0)
            before = _dot(tri, eq_f.astype(BF16))
            take = jnp.where(sc > thr, 1.0, jnp.where(before < need, eq_f, 0.0))
            score_scr[pl.ds(k0, KT), :] = jnp.where(take > 0.0, 0.0, -jnp.inf)
            return need - eq_f.sum(axis=0, keepdims=True)
        tile_loop(body, need_scr[...])

    score_scr[pl.ds(diag0, 2 * KT), :] = jnp.where(adm_diag, score_scr[pl.ds(diag0, 2 * KT), :], -jnp.inf)

    for h in range(N_HEADS_B):
        g = h // (N_HEADS_B // N_KV_B)
        q_h = qbt_ref[h * HEAD_DIM:(h + 1) * HEAD_DIM, :]
        rhs_scr[h, 0:HEAD_DIM, :] = q_h if g == 0 else zero_rows
        rhs_scr[h, HEAD_DIM:, :] = zero_rows if g == 0 else q_h
    m_scr[...] = jnp.full_like(m_scr, NEG_BIG)
    l_scr[...] = jnp.zeros_like(l_scr)
    acc_scr[...] = jnp.zeros_like(acc_scr)
    ones_rows = jnp.ones((ONES_ROWS, 2 * KT), BF16)

    def logits_head(h, neg, kb_t):
        s = _dot(kb_t, rhs_scr[h]) + neg
        s_scr[h] = s
        tmax_scr[h:h + 1, :] = s.max(axis=0, keepdims=True)

    def softmax_pv_head(h, v_ext):
        g = h // (N_HEADS_B // N_KV_B)
        m_old = m_scr[h:h + 1, :]
        m_new = jnp.maximum(m_old, tmax_scr[h:h + 1, :])
        m_scr[h:h + 1, :] = m_new
        alpha = jnp.exp2(m_old - m_new)
        p = jnp.exp2((s_scr[h] - m_new).astype(BF16))
        pv = _dot(v_ext[g], p)
        r0 = h * HEAD_DIM
        acc_scr[r0:r0 + HEAD_DIM, :] = alpha * acc_scr[r0:r0 + HEAD_DIM, :] + pv[0:HEAD_DIM]
        l_scr[h:h + 1, :] = alpha * l_scr[h:h + 1, :] + pv[HEAD_DIM:HEAD_DIM + 1]

    def stage(p_next, p_cur):
        k_next = pl.multiple_of(jnp.minimum(p_next, n_pairs - 1) * 2 * KT, 2 * KT)
        neg = score_scr[pl.ds(k_next, 2 * KT), :]
        kb_t = kb_ref[pl.ds(k_next, 2 * KT), :]
        if p_cur is not None:
            k_cur = pl.multiple_of(p_cur * 2 * KT, 2 * KT)
            v_ext = [jnp.concatenate([vbt_ref[g * HEAD_DIM:(g + 1) * HEAD_DIM, pl.ds(k_cur, 2 * KT)],
                                      ones_rows], axis=0) for g in range(N_KV_B)]
        for h in range(N_HEADS_B):
            if p_cur is not None:
                softmax_pv_head(h, v_ext)
            logits_head(h, neg, kb_t)

    stage(0, None)

    def attn_pair(i, carry):
        stage(i + 1, i)
        return carry

    lax.fori_loop(0, n_pairs, attn_pair, 0)

    for h in range(N_HEADS_B):
        r0 = h * HEAD_DIM
        o_ref[r0:r0 + HEAD_DIM, :] = (acc_scr[r0:r0 + HEAD_DIM, :] / l_scr[h:h + 1, :]).astype(BF16)


def _mixer_b(qit, wit, qbt, ki, kb, vbt):
    B, _, S = qbt.shape
    topk = min(TOPK_MAX, S // 4)
    assert 2 * KT == QBB and S % QBB == 0 and QBB % CHUNK == 0 and topk <= KT
    qspec = lambda w: pl.BlockSpec((None, w, QBB), lambda b, j: (b, 0, j))
    return pl.pallas_call(
        functools.partial(_mixer_b_kernel, topk=float(topk)),
        grid=(B, S // QBB),
        in_specs=[qspec(WIDTH_IDX), qspec(WI_ROWS), qspec(WIDTH_B),
                  pl.BlockSpec((None, S, LANES), lambda b, j: (b, 0, 0)),
                  pl.BlockSpec((None, S, LANES), lambda b, j: (b, 0, 0)),
                  pl.BlockSpec((None, WIDTH_KV_B, S), lambda b, j: (b, 0, 0))],
        out_specs=qspec(WIDTH_B),
        out_shape=jax.ShapeDtypeStruct((B, WIDTH_B, S), BF16),
        scratch_shapes=[pltpu.VMEM((S, QBB), F32),
                        pltpu.VMEM((S, QBB), BF16),
                        pltpu.VMEM((N_HEADS_B, 2 * KT, QBB), F32),
                        pltpu.VMEM((N_HEADS_B, QBB), F32),
                        pltpu.VMEM((N_HEADS_B, 2 * HEAD_DIM, QBB), BF16),
                        pltpu.VMEM((WIDTH_B, QBB), F32),
                        pltpu.VMEM((N_HEADS_B, QBB), F32),
                        pltpu.VMEM((N_HEADS_B, QBB), F32),
                        pltpu.VMEM((1, QBB), F32),
                        pltpu.VMEM((1, QBB), F32),
                        pltpu.SMEM((1,), jnp.int32)],
        compiler_params=pltpu.CompilerParams(
            dimension_semantics=("parallel", "arbitrary"), vmem_limit_bytes=VMEM_LIMIT),
        name="mixer_b",
    )(qit, wit, qbt, ki, kb, vbt)


def _merge_kernel(x_ref, oat_ref, obt_ref, ga_ref, gb_ref, wa_ref, wb_ref, wo_ref, o_ref):
    ya = _dot_tn(oat_ref[...], wa_ref[...])
    yb = _dot_tn(obt_ref[...], wb_ref[...])
    merged = (jax.nn.sigmoid(ga_ref[...].astype(F32)) * ya
              + jax.nn.sigmoid(gb_ref[...].astype(F32)) * yb)
    o_ref[...] = x_ref[...] + _dot(merged.astype(BF16), wo_ref[...])


def _merge(x3d, oat, obt, ga, gb, wa, wb, wo):
    B, S, _ = x3d.shape
    tm = PROJ_TM
    tmaj = pl.BlockSpec((None, tm, D_MODEL), lambda b, s: (b, s, 0))
    fmaj = lambda w: pl.BlockSpec((None, w, tm), lambda b, s: (b, 0, s))
    full = lambda a: pl.BlockSpec(a.shape, lambda b, s: (0, 0))
    return pl.pallas_call(
        _merge_kernel,
        grid=(B, S // tm),
        in_specs=[tmaj, fmaj(WIDTH_A), fmaj(WIDTH_B), tmaj, tmaj, full(wa), full(wb), full(wo)],
        out_specs=tmaj,
        out_shape=jax.ShapeDtypeStruct((B, S, D_MODEL), F32),
        compiler_params=pltpu.CompilerParams(
            dimension_semantics=("parallel", "parallel"), vmem_limit_bytes=VMEM_LIMIT),
        name="merge",
    )(x3d, oat, obt, ga, gb, wa, wb, wo)


def _rope_tables(seq):
    inv_freq = jnp.power(jnp.float32(ROPE_THETA), -jnp.arange(0, ROT_DIM, 2, dtype=F32) / ROT_DIM)
    ang = jnp.arange(seq, dtype=F32)[:, None] * inv_freq[None, :]
    cos, sin = jnp.cos(ang), jnp.sin(ang)
    ones = jnp.ones((seq, HEAD_DIM - ROT_DIM), F32)
    zeros = jnp.zeros((seq, HEAD_DIM - ROT_DIM), F32)
    zh = jnp.zeros((seq, ROT_HALF), F32)
    c = jnp.concatenate([cos, cos, ones], axis=1)
    sa = jnp.concatenate([-sin, zh, zeros], axis=1)
    sb = jnp.concatenate([zh, sin, zeros], axis=1)
    rep = LANES // HEAD_DIM
    return cos.T, sin.T, jnp.tile(c, (1, rep)), jnp.tile(sa, (1, rep)), jnp.tile(sb, (1, rep))


def _band_bias_ext(rel_bias):
    n_keys = BAND_BLOCKS * QB
    n_h = rel_bias.shape[0]
    lo = jnp.broadcast_to(rel_bias[:, :1], (n_h, QB - 1 - REL_CLIP))
    hi = jnp.broadcast_to(rel_bias[:, -1:], (n_h, BIAS_SPAN - (QB - 1 - REL_CLIP) - (2 * REL_CLIP + 1)))
    ext = jnp.concatenate([lo, rel_bias, hi], axis=1)
    return jnp.roll(ext, -(n_keys - 1), axis=1).astype(F32)


def kernel(x, n1_g, ffn1_w_in, ffn1_w_out, n2_g, w_in, rel_bias, w_branch_a, w_branch_b, w_out,
           n3_g, ffn2_w_in, ffn2_w_out, nf_g):
    B, S, D = x.shape
    depth = n1_g.shape[0]
    cos_t, sin_t, c_tab, sa_tab, sb_tab = _rope_tables(S)
    nf = nf_g.reshape(1, D)
    offs = np.cumsum([0, WIDTH_A, WIDTH_A, WIDTH_A, WIDTH_B, WIDTH_KV_B, WIDTH_KV_B,
                      WIDTH_IDX, IDX_DIM, N_IDX_HEADS, D_MODEL, D_MODEL])
    seg = lambda w, i: w[:, offs[i]:offs[i + 1]]

    for l in range(depth):
        w = w_in[l]
        wt = jnp.concatenate(
            [seg(w, 0), seg(w, 2), seg(w, 3), seg(w, 5), seg(w, 6),
             jnp.pad(seg(w, 8), ((0, 0), (0, LANES - N_IDX_HEADS)))], axis=1).astype(BF16)
        wk = jnp.concatenate(
            [seg(w, 1), seg(w, 4), jnp.pad(seg(w, 7), ((0, 0), (0, LANES - IDX_DIM)))], axis=1).astype(BF16)
        wg = jnp.concatenate([seg(w, 9), seg(w, 10)], axis=1).astype(BF16)

        x2d = _ffn(x.reshape(B * S, D), n1_g[l].reshape(1, D), ffn1_w_in[l].astype(BF16),
                   ffn1_w_out[l].astype(BF16), nf, False)
        x = x2d.reshape(B, S, D)
        (qat, vat, qbt, vbt, qit, wit, ka, kb, ki, ga, gb) = _inproj(
            x, n2_g[l].reshape(1, D), wt, wk, wg, cos_t, sin_t, c_tab, sa_tab, sb_tab)
        oat = _mixer_a(qat, ka, vat, _band_bias_ext(rel_bias[l]))
        obt = _mixer_b(qit, wit, qbt, ki, kb, vbt)
        x = _merge(x, oat, obt, ga, gb, w_branch_a[l].astype(BF16), w_branch_b[l].astype(BF16),
                   w_out[l].astype(BF16))
        last = l == depth - 1
        x2d = _ffn(x.reshape(B * S, D), n3_g[l].reshape(1, D), ffn2_w_in[l].astype(BF16),
                   ffn2_w_out[l].astype(BF16), nf, last)
        x = x2d.reshape(B, S, D)
    return x
```

```python
import functools

import jax
import jax.numpy as jnp
import numpy as np
from jax import lax
from jax.experimental import pallas as pl
from jax.experimental.pallas import tpu as pltpu

F32 = jnp.float32
BF16 = jnp.bfloat16

D_MODEL = 1024
D_FF = 2816
HEAD_DIM = 64
CHUNK = 64
N_PREV_CHUNKS = 8
N_HEADS_A = 8
REL_CLIP = 128
N_HEADS_B = 8
N_KV_B = 2
N_IDX_HEADS = 8
IDX_DIM = 64
TOPK_MAX = 256
ROPE_THETA = 500000.0
ROT_DIM = HEAD_DIM // 4
ROT_HALF = ROT_DIM // 2
EPS = 1e-6
WIDTH_A = N_HEADS_A * HEAD_DIM
WIDTH_B = N_HEADS_B * HEAD_DIM
WIDTH_KV_B = N_KV_B * HEAD_DIM
WIDTH_IDX = N_IDX_HEADS * IDX_DIM

LANES = 128
LOG2E = 1.4426950408889634
QK_SCALE = HEAD_DIM ** -0.5 * LOG2E
IDX_SCALE = IDX_DIM ** -0.5

VMEM_LIMIT = 56 * 1024 * 1024

FFN_TM = 1024
FFN_TF = D_FF // 2
PROJ_TM = 1024
QB = 256
QBB = 512
KT = 256
BAND_BLOCKS = N_PREV_CHUNKS * CHUNK // QB + 1
BIAS_SPAN = (BAND_BLOCKS + 1) * QB
WI_ROWS = 16
ONES_ROWS = 16
NEG_BIG = -1e30
FINE_STEPS = 17
FINE_STEPS_FIRST = 11
FOLD_CHAINS = 8


def _dot(a, b):
    return jnp.dot(a, b, preferred_element_type=F32)


def _dot_tt(a, b):
    return lax.dot_general(a, b, (((0,), (1,)), ((), ())), preferred_element_type=F32)


def _dot_tn(a, b):
    return lax.dot_general(a, b, (((0,), (0,)), ((), ())), preferred_element_type=F32)


def _rmsnorm(x, g):
    ms = jnp.mean(x * x, axis=-1, keepdims=True)
    return x * lax.rsqrt(ms + EPS) * g


def _ffn_kernel(x_ref, g_ref, wg_ref, wu_ref, wo_ref, gf_ref, o_ref, *, final_norm):
    x = x_ref[...]
    h = _rmsnorm(x, g_ref[...]).astype(BF16)
    gate = _dot(h, wg_ref[...])
    up = _dot(h, wu_ref[...])
    a = (gate * jax.nn.sigmoid(gate) * up).astype(BF16)
    y = x + 0.5 * _dot(a, wo_ref[...])
    if final_norm:
        y = _rmsnorm(y, gf_ref[...])
    o_ref[...] = y


def _ffn(x2d, g, w_in_bf, w_out_bf, gf, final_norm):
    T = x2d.shape[0]
    resident = lambda shape, col: pl.BlockSpec(shape, lambda i: (0, col), pipeline_mode=pl.Buffered(1))
    return pl.pallas_call(
        functools.partial(_ffn_kernel, final_norm=final_norm),
        grid=(T // FFN_TM,),
        in_specs=[
            pl.BlockSpec((FFN_TM, D_MODEL), lambda i: (i, 0)),
            pl.BlockSpec((1, D_MODEL), lambda i: (0, 0)),
            resident((D_MODEL, D_FF), 0),
            resident((D_MODEL, D_FF), 1),
            resident((D_FF, D_MODEL), 0),
            pl.BlockSpec((1, D_MODEL), lambda i: (0, 0)),
        ],
        out_specs=pl.BlockSpec((FFN_TM, D_MODEL), lambda i: (i, 0)),
        out_shape=jax.ShapeDtypeStruct((T, D_MODEL), F32),
        compiler_params=pltpu.CompilerParams(
            dimension_semantics=("parallel",), vmem_limit_bytes=VMEM_LIMIT),
        name="ffn_final" if final_norm else "ffn",
    )(x2d, g, w_in_bf, w_in_bf, w_out_bf, gf)


def _rope_rows(x, cos, sin, n_heads):
    pieces = []
    for h in range(n_heads):
        r0 = h * HEAD_DIM
        t1 = x[r0:r0 + ROT_HALF]
        t2 = x[r0 + ROT_HALF:r0 + ROT_DIM]
        pieces.append(t1 * cos - t2 * sin)
        pieces.append(t2 * cos + t1 * sin)
        pieces.append(x[r0 + ROT_DIM:r0 + HEAD_DIM])
    return jnp.concatenate(pieces, axis=0)


def _rope_lanes(x, c, sa, sb):
    return x * c + pltpu.roll(x, LANES - ROT_HALF, 1) * sa + pltpu.roll(x, ROT_HALF, 1) * sb


def _inproj_kernel(x_ref, g_ref, wt_ref, wk_ref, wg_ref, cos_ref, sin_ref, c_ref, sa_ref, sb_ref,
                   qat_ref, vat_ref, qbt_ref, vbt_ref, qit_ref, wit_ref,
                   ka_ref, kb_ref, ki_ref, ga_ref, gb_ref):
    h = _rmsnorm(x_ref[...], g_ref[...]).astype(BF16)
    cos = cos_ref[...]
    sin = sin_ref[...]

    ga_ref[...] = _dot(h, wg_ref[:, 0:D_MODEL]).astype(BF16)
    gb_ref[...] = _dot(h, wg_ref[:, D_MODEL:2 * D_MODEL]).astype(BF16)

    r = 0
    t = _dot_tt(wt_ref[:, r:r + WIDTH_A], h)
    qat_ref[...] = (t * QK_SCALE).astype(BF16)
    r += WIDTH_A
    vat_ref[...] = _dot_tt(wt_ref[:, r:r + WIDTH_A], h).astype(BF16)
    r += WIDTH_A
    t = _dot_tt(wt_ref[:, r:r + WIDTH_B], h)
    qbt_ref[...] = (_rope_rows(t, cos, sin, N_HEADS_B) * QK_SCALE).astype(BF16)
    r += WIDTH_B
    vbt_ref[...] = _dot_tt(wt_ref[:, r:r + WIDTH_KV_B], h).astype(BF16)
    r += WIDTH_KV_B
    t = _dot_tt(wt_ref[:, r:r + WIDTH_IDX], h)
    qit_ref[...] = (_rope_rows(t, cos, sin, N_IDX_HEADS) * IDX_SCALE).astype(BF16)
    r += WIDTH_IDX
    wit_ref[...] = _dot_tt(wt_ref[:, r:r + LANES], h)[0:WI_ROWS] * (N_IDX_HEADS ** -0.5)

    ka_ref[...] = _dot(h, wk_ref[:, 0:WIDTH_A]).astype(BF16)
    c, sa, sb = c_ref[...], sa_ref[...], sb_ref[...]
    t = _dot(h, wk_ref[:, WIDTH_A:WIDTH_A + 2 * LANES])
    kb_ref[...] = _rope_lanes(t[:, 0:LANES], c, sa, sb).astype(BF16)
    ki_ref[...] = _rope_lanes(t[:, LANES:2 * LANES], c, sa, sb).astype(BF16)


def _inproj(x3d, g, wt, wk, wg, cos_t, sin_t, c_tab, sa_tab, sb_tab):
    B, S, _ = x3d.shape
    tm = PROJ_TM
    full = lambda shape: pl.BlockSpec(shape, lambda b, s: (0,) * len(shape), pipeline_mode=pl.Buffered(1))
    tmaj = lambda w: pl.BlockSpec((None, tm, w), lambda b, s: (b, s, 0))
    fmaj = lambda w: pl.BlockSpec((None, w, tm), lambda b, s: (b, 0, s))
    tshape = lambda w, dt: jax.ShapeDtypeStruct((B, S, w), dt)
    fshape = lambda w, dt: jax.ShapeDtypeStruct((B, w, S), dt)
    return pl.pallas_call(
        _inproj_kernel,
        grid=(B, S // tm),
        in_specs=[
            tmaj(D_MODEL), full((1, D_MODEL)), full(wt.shape), full(wk.shape), full(wg.shape),
            pl.BlockSpec((ROT_HALF, tm), lambda b, s: (0, s)),
            pl.BlockSpec((ROT_HALF, tm), lambda b, s: (0, s)),
            pl.BlockSpec((tm, LANES), lambda b, s: (s, 0)),
            pl.BlockSpec((tm, LANES), lambda b, s: (s, 0)),
            pl.BlockSpec((tm, LANES), lambda b, s: (s, 0)),
        ],
        out_specs=[fmaj(WIDTH_A), fmaj(WIDTH_A), fmaj(WIDTH_B), fmaj(WIDTH_KV_B), fmaj(WIDTH_IDX),
                   fmaj(WI_ROWS), tmaj(WIDTH_A), tmaj(LANES), tmaj(LANES), tmaj(D_MODEL), tmaj(D_MODEL)],
        out_shape=[fshape(WIDTH_A, BF16), fshape(WIDTH_A, BF16), fshape(WIDTH_B, BF16),
                   fshape(WIDTH_KV_B, BF16), fshape(WIDTH_IDX, BF16), fshape(WI_ROWS, F32),
                   tshape(WIDTH_A, BF16), tshape(LANES, BF16), tshape(LANES, BF16),
                   tshape(D_MODEL, BF16), tshape(D_MODEL, BF16)],
        compiler_params=pltpu.CompilerParams(
            dimension_semantics=("parallel", "parallel"), vmem_limit_bytes=VMEM_LIMIT),
        name="inproj",
    )(x3d, g, wt, wk, wg, cos_t, sin_t, c_tab, sa_tab, sb_tab)


def _padded_rhs(qt_h, slot):
    z = jnp.zeros_like(qt_h)
    return jnp.concatenate([qt_h, z] if slot == 0 else [z, qt_h], axis=0)


def _mixer_a_kernel(qt_ref, k0_ref, k1_ref, k2_ref, v0_ref, v1_ref, v2_ref, ext_ref, o_ref,
                    bias_scr, s_scr):
    jb = pl.program_id(1)
    k_refs = (k0_ref, k1_ref, k2_ref)
    v_refs = (v0_ref, v1_ref, v2_ref)

    @pl.when((pl.program_id(0) == 0) & (jb == 0))
    def _():
        qc = lax.broadcasted_iota(jnp.int32, (QB, QB), 1) // CHUNK
        for m in range(BAND_BLOCKS):
            kc = lax.broadcasted_iota(jnp.int32, (QB, QB), 0) // CHUNK + m * (QB // CHUNK)
            valid = (kc >= qc) & (kc <= qc + N_PREV_CHUNKS)
            for h in range(N_HEADS_A):
                rows = jnp.broadcast_to(ext_ref[h:h + 1, :], (QB, BIAS_SPAN))
                skew = pltpu.roll(rows, m * QB, 1, stride=1, stride_axis=0)
                bias_scr[h, m * QB:(m + 1) * QB, :] = jnp.where(valid, skew[:, 0:QB] * LOG2E, -jnp.inf)

    ones_rows = jnp.ones((ONES_ROWS, QB), BF16)

    def logits_block(h, m, rhs):
        lane0 = (h // 2) * LANES
        sm = _dot(k_refs[m][:, lane0:lane0 + LANES], rhs) + bias_scr[h, m * QB:(m + 1) * QB, :]
        if m < BAND_BLOCKS - 1:
            sm = jnp.where(jb + m >= BAND_BLOCKS - 1, sm, -jnp.inf)
        s_scr[h % 2, m] = sm
        return sm.max(axis=0, keepdims=True)

    def pv_block(h, m, mx):
        r0 = h * HEAD_DIM
        p = jnp.exp2((s_scr[h % 2, m] - mx).astype(BF16))
        v_ext = jnp.concatenate([v_refs[m][r0:r0 + HEAD_DIM, :], ones_rows], axis=0)
        return _dot(v_ext, p)

    def head_rhs(h):
        return _padded_rhs(qt_ref[h * HEAD_DIM:(h + 1) * HEAD_DIM, :], h % 2)

    rhs = head_rhs(0)
    mx = functools.reduce(jnp.maximum, [logits_block(0, m, rhs) for m in range(BAND_BLOCKS)])
    for h in range(N_HEADS_A):
        r0 = h * HEAD_DIM
        acc = jnp.zeros((HEAD_DIM + ONES_ROWS, QB), F32)
        next_max = []
        if h + 1 < N_HEADS_A:
            rhs = head_rhs(h + 1)
        for m in range(BAND_BLOCKS):
            if h + 1 < N_HEADS_A:
                next_max.append(logits_block(h + 1, m, rhs))
            acc = acc + pv_block(h, m, mx)
        o_ref[r0:r0 + HEAD_DIM, :] = (acc[0:HEAD_DIM] / acc[HEAD_DIM:HEAD_DIM + 1]).astype(BF16)
        if next_max:
            mx = functools.reduce(jnp.maximum, next_max)


def _mixer_a(qat, ka, vat, bias_ext):
    B, _, S = qat.shape
    assert S % QB == 0 and QB % CHUNK == 0 and (N_PREV_CHUNKS * CHUNK) % QB == 0
    assert QB - 1 >= REL_CLIP
    nb = BAND_BLOCKS - 1
    kspec = lambda m: pl.BlockSpec((None, QB, WIDTH_A), lambda b, j: (b, jnp.maximum(j + m - nb, 0), 0))
    vspec = lambda m: pl.BlockSpec((None, WIDTH_A, QB), lambda b, j: (b, 0, jnp.maximum(j + m - nb, 0)))
    return pl.pallas_call(
        _mixer_a_kernel,
        grid=(B, S // QB),
        in_specs=[pl.BlockSpec((None, WIDTH_A, QB), lambda b, j: (b, 0, j)),
                  kspec(0), kspec(1), kspec(2), vspec(0), vspec(1), vspec(2),
                  pl.BlockSpec(bias_ext.shape, lambda b, j: (0, 0))],
        out_specs=pl.BlockSpec((None, WIDTH_A, QB), lambda b, j: (b, 0, j)),
        out_shape=jax.ShapeDtypeStruct((B, WIDTH_A, S), BF16),
        scratch_shapes=[pltpu.VMEM((N_HEADS_A, BAND_BLOCKS * QB, QB), F32),
                        pltpu.VMEM((2, BAND_BLOCKS, QB, QB), F32)],
        compiler_params=pltpu.CompilerParams(
            dimension_semantics=("arbitrary", "arbitrary"), vmem_limit_bytes=VMEM_LIMIT),
        name="mixer_a",
    )(qat, ka, ka, ka, vat, vat, vat, bias_ext)


def _ordered_code_to_f32(u):
    bits = jnp.where(u < 0, u ^ jnp.int32(-2 ** 31), ~u)
    return lax.bitcast_convert_type(bits, F32)


def _mixer_b_kernel(qit_ref, wit_ref, qbt_ref, ki_ref, kb_ref, vbt_ref, o_ref,
                    score_scr, sb_scr, s_scr, tmax_scr, rhs_scr, acc_scr, m_scr, l_scr,
                    thr_scr, need_scr, ties_scr, *, topk):
    jb = pl.program_id(1)
    n_pairs = jb + 1
    n_tiles = 2 * n_pairs
    diag0 = pl.multiple_of(jb * 2 * KT, 2 * KT)
    q_chunk = lax.broadcasted_iota(jnp.int32, (1, QBB), 1) // CHUNK
    adm_diag = lax.broadcasted_iota(jnp.int32, (2 * KT, QBB), 0) < (q_chunk + 1) * CHUNK
    zero_rows = jnp.zeros((HEAD_DIM, QBB), BF16)

    def tile_loop(body, init):
        def wrapped(t, carry):
            return body(pl.multiple_of(t * KT, KT), carry)
        return lax.fori_loop(0, n_tiles, wrapped, init)

    def col_count(hit):
        return hit.reshape(KT // 8, 8, QBB).sum(axis=0)

    for h in range(N_IDX_HEADS):
        rhs_scr[h, 0:IDX_DIM, :] = qit_ref[h * IDX_DIM:(h + 1) * IDX_DIM, :]
        rhs_scr[h, IDX_DIM:, :] = zero_rows
    w = wit_ref[...]

    def pair_loop(body, init):
        def wrapped(t, carry):
            return body(pl.multiple_of(t * 2 * KT, 2 * KT), carry)
        return lax.fori_loop(0, n_pairs, wrapped, init)

    def score_pair(k0, carry):
        ki_t = ki_ref[pl.ds(k0, 2 * KT), :]
        acc = jnp.zeros((2 * KT, QBB), F32)
        for h in range(N_IDX_HEADS):
            acc = acc + w[h:h + 1, :] * jnp.maximum(_dot(ki_t, rhs_scr[h]), 0.0)
        score_scr[pl.ds(k0, 2 * KT), :] = acc
        return carry

    pair_loop(score_pair, 0)
    score_scr[pl.ds(diag0, 2 * KT), :] = jnp.where(adm_diag, score_scr[pl.ds(diag0, 2 * KT), :], -jnp.inf)

    def round_pair(k0, carry):
        sb_scr[pl.ds(k0, 2 * KT), :] = score_scr[pl.ds(k0, 2 * KT), :].astype(BF16)
        return carry

    pair_loop(round_pair, 0)

    def count_pair(ref, k0, cand, rows, one, zero):
        chains = [None] * FOLD_CHAINS
        view = ref.at[pl.ds(k0, 2 * KT), :]
        for n, r in enumerate(range(0, 2 * KT, rows)):
            hit = jnp.where(view[r:r + rows, :] >= cand, one, zero)
            c = n % FOLD_CHAINS
            chains[c] = hit if chains[c] is None else chains[c] + hit
        while len(chains) > 1:
            chains = [a + b for a, b in zip(chains[0::2], chains[1::2])]
        return chains[0]

    def count_ge_bf16(cand):
        cand = jnp.broadcast_to(cand.astype(BF16), (16, QBB))

        def body(k0, cnt):
            return cnt + count_pair(sb_scr, k0, cand, 16, jnp.int16(1), jnp.int16(0))
        cnt = pair_loop(body, jnp.zeros((16, QBB), jnp.int16))
        return cnt.astype(jnp.int32).sum(axis=0, keepdims=True)

    def count_ge_f32(cand):
        cand = jnp.broadcast_to(cand, (8, QBB))

        def body(k0, cnt):
            return cnt + count_pair(score_scr, k0, cand, 8, 1.0, 0.0)
        cnt = pair_loop(body, jnp.zeros((8, QBB), F32))
        return cnt.sum(axis=0, keepdims=True).astype(jnp.int32)

    k_int = jnp.int32(int(topk))
    neg_inf_code = jnp.int32(0x007FFFFF)

    def accept(cnt, code):
        return (cnt >= k_int) | ((code >= 0) & (code <= neg_inf_code))

    def coarse_step(i, t_u):
        cand_u = t_u | lax.shift_left(jnp.int32(1), 31 - i)
        cnt = count_ge_bf16(_ordered_code_to_f32(cand_u))
        return jnp.where(accept(cnt, cand_u), cand_u, t_u)

    t_coarse = lax.fori_loop(0, 16, coarse_step, jnp.zeros((1, QBB), jnp.int32))
    base = t_coarse - jnp.int32(1 << 16)

    def fine_step(i, carry):
        off, cnt_at = carry
        cand_off = off | lax.shift_left(jnp.int32(1), 16 - i)
        cand_u = base + cand_off
        cnt = count_ge_f32(_ordered_code_to_f32(cand_u))
        ok = accept(cnt, cand_u)
        return jnp.where(ok, cand_off, off), jnp.where(ok, cnt, cnt_at)

    off, cnt_at = lax.fori_loop(0, FINE_STEPS_FIRST, fine_step,
                                (jnp.zeros((1, QBB), jnp.int32), jnp.full((1, QBB), -1, jnp.int32)))
    thr_first = _ordered_code_to_f32(base + off)
    thr_scr[...] = thr_first
    ties_scr[0] = jnp.int32(0)
    settled = cnt_at == k_int
    unsettled = jnp.max(jnp.where(settled, 0.0, 1.0)) > 0.0

    def tie_analysis(thr_q):
        thr_scr[...] = thr_q

        def count_both(k0, carry):
            ge, gt = carry
            sc = score_scr[pl.ds(k0, KT), :]
            return (ge + col_count(jnp.where(sc >= thr_q, 1.0, 0.0)),
                    gt + col_count(jnp.where(sc > thr_q, 1.0, 0.0)))

        ge8, gt8 = tile_loop(count_both, (jnp.zeros((8, QBB), F32), jnp.zeros((8, QBB), F32)))
        cnt_ge = ge8.sum(axis=0, keepdims=True)
        need_scr[...] = topk - gt8.sum(axis=0, keepdims=True)
        n_inadm = (2 * KT - (q_chunk + 1) * CHUNK).astype(F32)
        n_sel = cnt_ge - jnp.where(thr_q == -jnp.inf, n_inadm, 0.0)
        ties_scr[0] = (jnp.max(jnp.where(n_sel > topk, 1.0, 0.0)) > 0.0).astype(jnp.int32)
        return cnt_ge

    @pl.when(unsettled)
    def _():
        hi_code = base + off + jnp.int32(1 << (FINE_STEPS - FINE_STEPS_FIRST))
        hi = _ordered_code_to_f32(hi_code)
        hi = jnp.where((hi != hi) & (hi_code < 0), jnp.inf, hi)

        def below_max(k0, m):
            sc = score_scr[pl.ds(k0, KT), :]
            return jnp.maximum(m, jnp.where(sc < hi, sc, -jnp.inf).reshape(KT // 8, 8, QBB).max(axis=0))

        m8 = tile_loop(below_max, jnp.full((8, QBB), -jnp.inf, F32))
        thr_try = jnp.where(settled, thr_first, m8.max(axis=0, keepdims=True))
        cnt_try = tie_analysis(thr_try)
        still_short = jnp.max(jnp.where(cnt_try >= topk, 0.0, 1.0)) > 0.0

        @pl.when(still_short)
        def _():
            off_full, _ = lax.fori_loop(FINE_STEPS_FIRST, FINE_STEPS, fine_step, (off, cnt_at))
            tie_analysis(_ordered_code_to_f32(base + off_full))

    thr = thr_scr[...]
    has_ties = ties_scr[0] > 0

    @pl.when(jnp.logical_not(has_ties))
    def _():
        def body(k0, carry):
            score_scr[pl.ds(k0, KT), :] = jnp.where(score_scr[pl.ds(k0, KT), :] >= thr, 0.0, -jnp.inf)
            return carry
        tile_loop(body, 0)

    @pl.when(has_ties)
    def _():
        tri = jnp.where(lax.broadcasted_iota(jnp.int32, (KT, KT), 1)
                        < lax.broadcasted_iota(jnp.int32, (KT, KT), 0), 1.0, 0.0).astype(BF16)

        def body(k0, need):
            sc = score_scr[pl.ds(k0, KT), :]
            eq_f = jnp.where(sc == thr, 1.0, 0.0)
            before = _dot(tri, eq_f.astype(BF16))
            take = jnp.where(sc > thr, 1.0, jnp.where(before < need, eq_f, 0.0))
            score_scr[pl.ds(k0, KT), :] = jnp.where(take > 0.0, 0.0, -jnp.inf)
            return need - eq_f.sum(axis=0, keepdims=True)
        tile_loop(body, need_scr[...])

    score_scr[pl.ds(diag0, 2 * KT), :] = jnp.where(adm_diag, score_scr[pl.ds(diag0, 2 * KT), :], -jnp.inf)

    for h in range(N_HEADS_B):
        g = h // (N_HEADS_B // N_KV_B)
        q_h = qbt_ref[h * HEAD_DIM:(h + 1) * HEAD_DIM, :]
        rhs_scr[h, 0:HEAD_DIM, :] = q_h if g == 0 else zero_rows
        rhs_scr[h, HEAD_DIM:, :] = zero_rows if g == 0 else q_h
    m_scr[...] = jnp.full_like(m_scr, NEG_BIG)
    l_scr[...] = jnp.zeros_like(l_scr)
    acc_scr[...] = jnp.zeros_like(acc_scr)
    ones_rows = jnp.ones((ONES_ROWS, 2 * KT), BF16)

    def logits_head(h, neg, kb_t):
        s = _dot(kb_t, rhs_scr[h]) + neg
        s_scr[h] = s
        tmax_scr[h:h + 1, :] = s.max(axis=0, keepdims=True)

    def softmax_pv_head(h, v_ext):
        g = h // (N_HEADS_B // N_KV_B)
        m_old = m_scr[h:h + 1, :]
        m_new = jnp.maximum(m_old, tmax_scr[h:h + 1, :])
        m_scr[h:h + 1, :] = m_new
        alpha = jnp.exp2(m_old - m_new)
        p = jnp.exp2((s_scr[h] - m_new).astype(BF16))
        pv = _dot(v_ext[g], p)
        r0 = h * HEAD_DIM
        acc_scr[r0:r0 + HEAD_DIM, :] = alpha * acc_scr[r0:r0 + HEAD_DIM, :] + pv[0:HEAD_DIM]
        l_scr[h:h + 1, :] = alpha * l_scr[h:h + 1, :] + pv[HEAD_DIM:HEAD_DIM + 1]

    def stage(p_next, p_cur):
        k_next = pl.multiple_of(jnp.minimum(p_next, n_pairs - 1) * 2 * KT, 2 * KT)
        neg = score_scr[pl.ds(k_next, 2 * KT), :]
        kb_t = kb_ref[pl.ds(k_next, 2 * KT), :]
        if p_cur is not None:
            k_cur = pl.multiple_of(p_cur * 2 * KT, 2 * KT)
            v_ext = [jnp.concatenate([vbt_ref[g * HEAD_DIM:(g + 1) * HEAD_DIM, pl.ds(k_cur, 2 * KT)],
                                      ones_rows], axis=0) for g in range(N_KV_B)]
        for h in range(N_HEADS_B):
            if p_cur is not None:
                softmax_pv_head(h, v_ext)
            logits_head(h, neg, kb_t)

    stage(0, None)

    def attn_pair(i, carry):
        stage(i + 1, i)
        return carry

    lax.fori_loop(0, n_pairs, attn_pair, 0)

    for h in range(N_HEADS_B):
        r0 = h * HEAD_DIM
        o_ref[r0:r0 + HEAD_DIM, :] = (acc_scr[r0:r0 + HEAD_DIM, :] / l_scr[h:h + 1, :]).astype(BF16)


def _mixer_b(qit, wit, qbt, ki, kb, vbt):
    B, _, S = qbt.shape
    topk = min(TOPK_MAX, S // 4)
    assert 2 * KT == QBB and S % QBB == 0 and QBB % CHUNK == 0 and topk <= KT
    qspec = lambda w: pl.BlockSpec((None, w, QBB), lambda b, j: (b, 0, j))
    return pl.pallas_call(
        functools.partial(_mixer_b_kernel, topk=float(topk)),
        grid=(B, S // QBB),
        in_specs=[qspec(WIDTH_IDX), qspec(WI_ROWS), qspec(WIDTH_B),
                  pl.BlockSpec((None, S, LANES), lambda b, j: (b, 0, 0)),
                  pl.BlockSpec((None, S, LANES), lambda b, j: (b, 0, 0)),
                  pl.BlockSpec((None, WIDTH_KV_B, S), lambda b, j: (b, 0, 0))],
        out_specs=qspec(WIDTH_B),
        out_shape=jax.ShapeDtypeStruct((B, WIDTH_B, S), BF16),
        scratch_shapes=[pltpu.VMEM((S, QBB), F32),
                        pltpu.VMEM((S, QBB), BF16),
                        pltpu.VMEM((N_HEADS_B, 2 * KT, QBB), F32),
                        pltpu.VMEM((N_HEADS_B, QBB), F32),
                        pltpu.VMEM((N_HEADS_B, 2 * HEAD_DIM, QBB), BF16),
                        pltpu.VMEM((WIDTH_B, QBB), F32),
                        pltpu.VMEM((N_HEADS_B, QBB), F32),
                        pltpu.VMEM((N_HEADS_B, QBB), F32),
                        pltpu.VMEM((1, QBB), F32),
                        pltpu.VMEM((1, QBB), F32),
                        pltpu.SMEM((1,), jnp.int32)],
        compiler_params=pltpu.CompilerParams(
            dimension_semantics=("parallel", "arbitrary"), vmem_limit_bytes=VMEM_LIMIT),
        name="mixer_b",
    )(qit, wit, qbt, ki, kb, vbt)


def _merge_kernel(x_ref, oat_ref, obt_ref, ga_ref, gb_ref, wa_ref, wb_ref, wo_ref, o_ref):
    ya = _dot_tn(oat_ref[...], wa_ref[...])
    yb = _dot_tn(obt_ref[...], wb_ref[...])
    merged = (jax.nn.sigmoid(ga_ref[...].astype(F32)) * ya
              + jax.nn.sigmoid(gb_ref[...].astype(F32)) * yb)
    o_ref[...] = x_ref[...] + _dot(merged.astype(BF16), wo_ref[...])


def _merge(x3d, oat, obt, ga, gb, wa, wb, wo):
    B, S, _ = x3d.shape
    tm = PROJ_TM
    tmaj = pl.BlockSpec((None, tm, D_MODEL), lambda b, s: (b, s, 0))
    fmaj = lambda w: pl.BlockSpec((None, w, tm), lambda b, s: (b, 0, s))
    full = lambda a: pl.BlockSpec(a.shape, lambda b, s: (0, 0))
    return pl.pallas_call(
        _merge_kernel,
        grid=(B, S // tm),
        in_specs=[tmaj, fmaj(WIDTH_A), fmaj(WIDTH_B), tmaj, tmaj, full(wa), full(wb), full(wo)],
        out_specs=tmaj,
        out_shape=jax.ShapeDtypeStruct((B, S, D_MODEL), F32),
        compiler_params=pltpu.CompilerParams(
            dimension_semantics=("parallel", "parallel"), vmem_limit_bytes=VMEM_LIMIT),
        name="merge",
    )(x3d, oat, obt, ga, gb, wa, wb, wo)


def _rope_tables(seq):
    inv_freq = jnp.power(jnp.float32(ROPE_THETA), -jnp.arange(0, ROT_DIM, 2, dtype=F32) / ROT_DIM)
    ang = jnp.arange(seq, dtype=F32)[:, None] * inv_freq[None, :]
    cos, sin = jnp.cos(ang), jnp.sin(ang)
    ones = jnp.ones((seq, HEAD_DIM - ROT_DIM), F32)
    zeros = jnp.zeros((seq, HEAD_DIM - ROT_DIM), F32)
    zh = jnp.zeros((seq, ROT_HALF), F32)
    c = jnp.concatenate([cos, cos, ones], axis=1)
    sa = jnp.concatenate([-sin, zh, zeros], axis=1)
    sb = jnp.concatenate([zh, sin, zeros], axis=1)
    rep = LANES // HEAD_DIM
    return cos.T, sin.T, jnp.tile(c, (1, rep)), jnp.tile(sa, (1, rep)), jnp.tile(sb, (1, rep))


def _band_bias_ext(rel_bias):
    n_keys = BAND_BLOCKS * QB
    n_h = rel_bias.shape[0]
    lo = jnp.broadcast_to(rel_bias[:, :1], (n_h, QB - 1 - REL_CLIP))
    hi = jnp.broadcast_to(rel_bias[:, -1:], (n_h, BIAS_SPAN - (QB - 1 - REL_CLIP) - (2 * REL_CLIP + 1)))
    ext = jnp.concatenate([lo, rel_bias, hi], axis=1)
    return jnp.roll(ext, -(n_keys - 1), axis=1).astype(F32)


def kernel(x, n1_g, ffn1_w_in, ffn1_w_out, n2_g, w_in, rel_bias, w_branch_a, w_branch_b, w_out,
           n3_g, ffn2_w_in, ffn2_w_out, nf_g):
    B, S, D = x.shape
    depth = n1_g.shape[0]
    cos_t, sin_t, c_tab, sa_tab, sb_tab = _rope_tables(S)
    nf = nf_g.reshape(1, D)
    offs = np.cumsum([0, WIDTH_A, WIDTH_A, WIDTH_A, WIDTH_B, WIDTH_KV_B, WIDTH_KV_B,
                      WIDTH_IDX, IDX_DIM, N_IDX_HEADS, D_MODEL, D_MODEL])
    seg = lambda w, i: w[:, offs[i]:offs[i + 1]]

    for l in range(depth):
        w = w_in[l]
        wt = jnp.concatenate(
            [seg(w, 0), seg(w, 2), seg(w, 3), seg(w, 5), seg(w, 6),
             jnp.pad(seg(w, 8), ((0, 0), (0, LANES - N_IDX_HEADS)))], axis=1).astype(BF16)
        wk = jnp.concatenate(
            [seg(w, 1), seg(w, 4), jnp.pad(seg(w, 7), ((0, 0), (0, LANES - IDX_DIM)))], axis=1).astype(BF16)
        wg = jnp.concatenate([seg(w, 9), seg(w, 10)], axis=1).astype(BF16)

        x2d = _ffn(x.reshape(B * S, D), n1_g[l].reshape(1, D), ffn1_w_in[l].astype(BF16),
                   ffn1_w_out[l].astype(BF16), nf, False)
        x = x2d.reshape(B, S, D)
        (qat, vat, qbt, vbt, qit, wit, ka, kb, ki, ga, gb) = _inproj(
            x, n2_g[l].reshape(1, D), wt, wk, wg, cos_t, sin_t, c_tab, sa_tab, sb_tab)
        oat = _mixer_a(qat, ka, vat, _band_bias_ext(rel_bias[l]))
        obt = _mixer_b(qit, wit, qbt, ki, kb, vbt)
        x = _merge(x, oat, obt, ga, gb, w_branch_a[l].astype(BF16), w_branch_b[l].astype(BF16),
                   w_out[l].astype(BF16))
        last = l == depth - 1
        x2d = _ffn(x.reshape(B * S, D), n3_g[l].reshape(1, D), ffn2_w_in[l].astype(BF16),
                   ffn2_w_out[l].astype(BF16), nf, last)
        x = x2d.reshape(B, S, D)
    return x
```

```python
import functools

import jax
import jax.numpy as jnp
import numpy as np
from jax import lax
from jax.experimental import pallas as pl
from jax.experimental.pallas import tpu as pltpu

F32 = jnp.float32
BF16 = jnp.bfloat16

D_MODEL = 1024
D_FF = 2816
HEAD_DIM = 64
CHUNK = 64
N_PREV_CHUNKS = 8
N_HEADS_A = 8
REL_CLIP = 128
N_HEADS_B = 8
N_KV_B = 2
N_IDX_HEADS = 8
IDX_DIM = 64
TOPK_MAX = 256
ROPE_THETA = 500000.0
ROT_DIM = HEAD_DIM // 4
ROT_HALF = ROT_DIM // 2
EPS = 1e-6
WIDTH_A = N_HEADS_A * HEAD_DIM
WIDTH_B = N_HEADS_B * HEAD_DIM
WIDTH_KV_B = N_KV_B * HEAD_DIM
WIDTH_IDX = N_IDX_HEADS * IDX_DIM

LANES = 128
LOG2E = 1.4426950408889634
QK_SCALE = HEAD_DIM ** -0.5 * LOG2E
IDX_SCALE = IDX_DIM ** -0.5

VMEM_LIMIT = 56 * 1024 * 1024

FFN_TM = 1024
FFN_TF = D_FF // 2
PROJ_TM = 1024
QB = 256
QBB = 512
KT = 256
BAND_BLOCKS = N_PREV_CHUNKS * CHUNK // QB + 1
BIAS_SPAN = (BAND_BLOCKS + 1) * QB
WI_ROWS = 16
ONES_ROWS = 16
NEG_BIG = -1e30
FINE_STEPS = 17
FINE_STEPS_FIRST = 11
FOLD_CHAINS = 8


def _dot(a, b):
    return jnp.dot(a, b, preferred_element_type=F32)


def _dot_tt(a, b):
    return lax.dot_general(a, b, (((0,), (1,)), ((), ())), preferred_element_type=F32)


def _dot_tn(a, b):
    return lax.dot_general(a, b, (((0,), (0,)), ((), ())), preferred_element_type=F32)


def _rmsnorm(x, g):
    ms = jnp.mean(x * x, axis=-1, keepdims=True)
    return x * lax.rsqrt(ms + EPS) * g


def _ffn_kernel(x_ref, g_ref, wg_ref, wu_ref, wo_ref, gf_ref, o_ref, *, final_norm):
    x = x_ref[...]
    h = _rmsnorm(x, g_ref[...]).astype(BF16)
    gate = _dot(h, wg_ref[...])
    up = _dot(h, wu_ref[...])
    a = (gate * jax.nn.sigmoid(gate) * up).astype(BF16)
    y = x + 0.5 * _dot(a, wo_ref[...])
    if final_norm:
        y = _rmsnorm(y, gf_ref[...])
    o_ref[...] = y


def _ffn(x2d, g, w_in_bf, w_out_bf, gf, final_norm):
    T = x2d.shape[0]
    resident = lambda shape, col: pl.BlockSpec(shape, lambda i: (0, col), pipeline_mode=pl.Buffered(1))
    return pl.pallas_call(
        functools.partial(_ffn_kernel, final_norm=final_norm),
        grid=(T // FFN_TM,),
        in_specs=[
            pl.BlockSpec((FFN_TM, D_MODEL), lambda i: (i, 0)),
            pl.BlockSpec((1, D_MODEL), lambda i: (0, 0)),
            resident((D_MODEL, D_FF), 0),
            resident((D_MODEL, D_FF), 1),
            resident((D_FF, D_MODEL), 0),
            pl.BlockSpec((1, D_MODEL), lambda i: (0, 0)),
        ],
        out_specs=pl.BlockSpec((FFN_TM, D_MODEL), lambda i: (i, 0)),
        out_shape=jax.ShapeDtypeStruct((T, D_MODEL), F32),
        compiler_params=pltpu.CompilerParams(
            dimension_semantics=("parallel",), vmem_limit_bytes=VMEM_LIMIT),
        name="ffn_final" if final_norm else "ffn",
    )(x2d, g, w_in_bf, w_in_bf, w_out_bf, gf)


def _rope_rows(x, cos, sin, n_heads):
    pieces = []
    for h in range(n_heads):
        r0 = h * HEAD_DIM
        t1 = x[r0:r0 + ROT_HALF]
        t2 = x[r0 + ROT_HALF:r0 + ROT_DIM]
        pieces.append(t1 * cos - t2 * sin)
        pieces.append(t2 * cos + t1 * sin)
        pieces.append(x[r0 + ROT_DIM:r0 + HEAD_DIM])
    return jnp.concatenate(pieces, axis=0)


def _rope_lanes(x, c, sa, sb):
    return x * c + pltpu.roll(x, LANES - ROT_HALF, 1) * sa + pltpu.roll(x, ROT_HALF, 1) * sb


def _inproj_kernel(x_ref, g_ref, wt_ref, wk_ref, wg_ref, cos_ref, sin_ref, c_ref, sa_ref, sb_ref,
                   qat_ref, vat_ref, qbt_ref, vbt_ref, qit_ref, wit_ref,
                   ka_ref, kb_ref, ki_ref, ga_ref, gb_ref):
    h = _rmsnorm(x_ref[...], g_ref[...]).astype(BF16)
    cos = cos_ref[...]
    sin = sin_ref[...]

    ga_ref[...] = _dot(h, wg_ref[:, 0:D_MODEL]).astype(BF16)
    gb_ref[...] = _dot(h, wg_ref[:, D_MODEL:2 * D_MODEL]).astype(BF16)

    ft = _dot_tt(wt_ref[...], h)
    r = 0
    qat_ref[...] = (ft[r:r + WIDTH_A] * QK_SCALE).astype(BF16)
    r += WIDTH_A
    vat_ref[...] = ft[r:r + WIDTH_A].astype(BF16)
    r += WIDTH_A
    qbt_ref[...] = (_rope_rows(ft[r:r + WIDTH_B], cos, sin, N_HEADS_B) * QK_SCALE).astype(BF16)
    r += WIDTH_B
    vbt_ref[...] = ft[r:r + WIDTH_KV_B].astype(BF16)
    r += WIDTH_KV_B
    qit_ref[...] = (_rope_rows(ft[r:r + WIDTH_IDX], cos, sin, N_IDX_HEADS) * IDX_SCALE).astype(BF16)
    r += WIDTH_IDX
    wit_ref[...] = ft[r:r + WI_ROWS] * (N_IDX_HEADS ** -0.5)

    ka_ref[...] = _dot(h, wk_ref[:, 0:WIDTH_A]).astype(BF16)
    c, sa, sb = c_ref[...], sa_ref[...], sb_ref[...]
    t = _dot(h, wk_ref[:, WIDTH_A:WIDTH_A + 2 * LANES])
    kb_ref[...] = _rope_lanes(t[:, 0:LANES], c, sa, sb).astype(BF16)
    ki_ref[...] = _rope_lanes(t[:, LANES:2 * LANES], c, sa, sb).astype(BF16)


def _inproj(x3d, g, wt, wk, wg, cos_t, sin_t, c_tab, sa_tab, sb_tab):
    B, S, _ = x3d.shape
    tm = PROJ_TM
    full = lambda shape: pl.BlockSpec(shape, lambda b, s: (0,) * len(shape), pipeline_mode=pl.Buffered(1))
    tmaj = lambda w: pl.BlockSpec((None, tm, w), lambda b, s: (b, s, 0))
    fmaj = lambda w: pl.BlockSpec((None, w, tm), lambda b, s: (b, 0, s))
    tshape = lambda w, dt: jax.ShapeDtypeStruct((B, S, w), dt)
    fshape = lambda w, dt: jax.ShapeDtypeStruct((B, w, S), dt)
    return pl.pallas_call(
        _inproj_kernel,
        grid=(B, S // tm),
        in_specs=[
            tmaj(D_MODEL), full((1, D_MODEL)), full(wt.shape), full(wk.shape), full(wg.shape),
            pl.BlockSpec((ROT_HALF, tm), lambda b, s: (0, s)),
            pl.BlockSpec((ROT_HALF, tm), lambda b, s: (0, s)),
            pl.BlockSpec((tm, LANES), lambda b, s: (s, 0)),
            pl.BlockSpec((tm, LANES), lambda b, s: (s, 0)),
            pl.BlockSpec((tm, LANES), lambda b, s: (s, 0)),
        ],
        out_specs=[fmaj(WIDTH_A), fmaj(WIDTH_A), fmaj(WIDTH_B), fmaj(WIDTH_KV_B), fmaj(WIDTH_IDX),
                   fmaj(WI_ROWS), tmaj(WIDTH_A), tmaj(LANES), tmaj(LANES), tmaj(D_MODEL), tmaj(D_MODEL)],
        out_shape=[fshape(WIDTH_A, BF16), fshape(WIDTH_A, BF16), fshape(WIDTH_B, BF16),
                   fshape(WIDTH_KV_B, BF16), fshape(WIDTH_IDX, BF16), fshape(WI_ROWS, F32),
                   tshape(WIDTH_A, BF16), tshape(LANES, BF16), tshape(LANES, BF16),
                   tshape(D_MODEL, BF16), tshape(D_MODEL, BF16)],
        compiler_params=pltpu.CompilerParams(
            dimension_semantics=("parallel", "parallel"), vmem_limit_bytes=VMEM_LIMIT),
        name="inproj",
    )(x3d, g, wt, wk, wg, cos_t, sin_t, c_tab, sa_tab, sb_tab)


def _padded_rhs(qt_h, slot):
    z = jnp.zeros_like(qt_h)
    return jnp.concatenate([qt_h, z] if slot == 0 else [z, qt_h], axis=0)


def _mixer_a_kernel(qt_ref, k0_ref, k1_ref, k2_ref, v0_ref, v1_ref, v2_ref, ext_ref, o_ref,
                    bias_scr, s_scr):
    jb = pl.program_id(1)
    k_refs = (k0_ref, k1_ref, k2_ref)
    v_refs = (v0_ref, v1_ref, v2_ref)

    @pl.when((pl.program_id(0) == 0) & (jb == 0))
    def _():
        qc = lax.broadcasted_iota(jnp.int32, (QB, QB), 1) // CHUNK
        for m in range(BAND_BLOCKS):
            kc = lax.broadcasted_iota(jnp.int32, (QB, QB), 0) // CHUNK + m * (QB // CHUNK)
            valid = (kc >= qc) & (kc <= qc + N_PREV_CHUNKS)
            for h in range(N_HEADS_A):
                rows = jnp.broadcast_to(ext_ref[h:h + 1, :], (QB, BIAS_SPAN))
                skew = pltpu.roll(rows, m * QB, 1, stride=1, stride_axis=0)
                bias_scr[h, m * QB:(m + 1) * QB, :] = jnp.where(valid, skew[:, 0:QB] * LOG2E, -jnp.inf)

    ones_rows = jnp.ones((ONES_ROWS, QB), BF16)

    def logits_block(h, m, rhs):
        lane0 = (h // 2) * LANES
        sm = _dot(k_refs[m][:, lane0:lane0 + LANES], rhs) + bias_scr[h, m * QB:(m + 1) * QB, :]
        if m < BAND_BLOCKS - 1:
            sm = jnp.where(jb + m >= BAND_BLOCKS - 1, sm, -jnp.inf)
        s_scr[h % 2, m] = sm
        return sm.max(axis=0, keepdims=True)

    def pv_block(h, m, mx):
        r0 = h * HEAD_DIM
        p = jnp.exp2((s_scr[h % 2, m] - mx).astype(BF16))
        v_ext = jnp.concatenate([v_refs[m][r0:r0 + HEAD_DIM, :], ones_rows], axis=0)
        return _dot(v_ext, p)

    def head_rhs(h):
        return _padded_rhs(qt_ref[h * HEAD_DIM:(h + 1) * HEAD_DIM, :], h % 2)

    rhs = head_rhs(0)
    mx = functools.reduce(jnp.maximum, [logits_block(0, m, rhs) for m in range(BAND_BLOCKS)])
    for h in range(N_HEADS_A):
        r0 = h * HEAD_DIM
        acc = jnp.zeros((HEAD_DIM + ONES_ROWS, QB), F32)
        next_max = []
        if h + 1 < N_HEADS_A:
            rhs = head_rhs(h + 1)
        for m in range(BAND_BLOCKS):
            if h + 1 < N_HEADS_A:
                next_max.append(logits_block(h + 1, m, rhs))
            acc = acc + pv_block(h, m, mx)
        o_ref[r0:r0 + HEAD_DIM, :] = (acc[0:HEAD_DIM] / acc[HEAD_DIM:HEAD_DIM + 1]).astype(BF16)
        if next_max:
            mx = functools.reduce(jnp.maximum, next_max)


def _mixer_a(qat, ka, vat, bias_ext):
    B, _, S = qat.shape
    assert S % QB == 0 and QB % CHUNK == 0 and (N_PREV_CHUNKS * CHUNK) % QB == 0
    assert QB - 1 >= REL_CLIP
    nb = BAND_BLOCKS - 1
    kspec = lambda m: pl.BlockSpec((None, QB, WIDTH_A), lambda b, j: (b, jnp.maximum(j + m - nb, 0), 0))
    vspec = lambda m: pl.BlockSpec((None, WIDTH_A, QB), lambda b, j: (b, 0, jnp.maximum(j + m - nb, 0)))
    return pl.pallas_call(
        _mixer_a_kernel,
        grid=(B, S // QB),
        in_specs=[pl.BlockSpec((None, WIDTH_A, QB), lambda b, j: (b, 0, j)),
                  kspec(0), kspec(1), kspec(2), vspec(0), vspec(1), vspec(2),
                  pl.BlockSpec(bias_ext.shape, lambda b, j: (0, 0))],
        out_specs=pl.BlockSpec((None, WIDTH_A, QB), lambda b, j: (b, 0, j)),
        out_shape=jax.ShapeDtypeStruct((B, WIDTH_A, S), BF16),
        scratch_shapes=[pltpu.VMEM((N_HEADS_A, BAND_BLOCKS * QB, QB), F32),
                        pltpu.VMEM((2, BAND_BLOCKS, QB, QB), F32)],
        compiler_params=pltpu.CompilerParams(
            dimension_semantics=("arbitrary", "arbitrary"), vmem_limit_bytes=VMEM_LIMIT),
        name="mixer_a",
    )(qat, ka, ka, ka, vat, vat, vat, bias_ext)


def _ordered_code_to_f32(u):
    bits = jnp.where(u < 0, u ^ jnp.int32(-2 ** 31), ~u)
    return lax.bitcast_convert_type(bits, F32)


def _mixer_b_kernel(qit_ref, wit_ref, qbt_ref, ki_ref, kb_ref, vbt_ref, o_ref,
                    score_scr, sb_scr, s_scr, tmax_scr, rhs_scr, acc_scr, m_scr, l_scr,
                    thr_scr, need_scr, ties_scr, *, topk):
    jb = pl.program_id(1)
    n_pairs = jb + 1
    n_tiles = 2 * n_pairs
    diag0 = pl.multiple_of(jb * 2 * KT, 2 * KT)
    q_chunk = lax.broadcasted_iota(jnp.int32, (1, QBB), 1) // CHUNK
    adm_diag = lax.broadcasted_iota(jnp.int32, (2 * KT, QBB), 0) < (q_chunk + 1) * CHUNK
    zero_rows = jnp.zeros((HEAD_DIM, QBB), BF16)

    def tile_loop(body, init):
        def wrapped(t, carry):
            return body(pl.multiple_of(t * KT, KT), carry)
        return lax.fori_loop(0, n_tiles, wrapped, init)

    def col_count(hit):
        return hit.reshape(KT // 8, 8, QBB).sum(axis=0)

    for h in range(N_IDX_HEADS):
        rhs_scr[h, 0:IDX_DIM, :] = qit_ref[h * IDX_DIM:(h + 1) * IDX_DIM, :]
        rhs_scr[h, IDX_DIM:, :] = zero_rows
    w = wit_ref[...]

    def pair_loop(body, init):
        def wrapped(t, carry):
            return body(pl.multiple_of(t * 2 * KT, 2 * KT), carry)
        return lax.fori_loop(0, n_pairs, wrapped, init)

    def score_pair(k0, carry):
        ki_t = ki_ref[pl.ds(k0, 2 * KT), :]
        acc = jnp.zeros((2 * KT, QBB), F32)
        for h in range(N_IDX_HEADS):
            acc = acc + w[h:h + 1, :] * jnp.maximum(_dot(ki_t, rhs_scr[h]), 0.0)
        score_scr[pl.ds(k0, 2 * KT), :] = acc
        return carry

    pair_loop(score_pair, 0)
    score_scr[pl.ds(diag0, 2 * KT), :] = jnp.where(adm_diag, score_scr[pl.ds(diag0, 2 * KT), :], -jnp.inf)

    def round_pair(k0, carry):
        sb_scr[pl.ds(k0, 2 * KT), :] = score_scr[pl.ds(k0, 2 * KT), :].astype(BF16)
        return carry

    pair_loop(round_pair, 0)

    def count_pair(ref, k0, cand, rows, one, zero):
        chains = [None] * FOLD_CHAINS
        view = ref.at[pl.ds(k0, 2 * KT), :]
        for n, r in enumerate(range(0, 2 * KT, rows)):
            hit = jnp.where(view[r:r + rows, :] >= cand, one, zero)
            c = n % FOLD_CHAINS
            chains[c] = hit if chains[c] is None else chains[c] + hit
        while len(chains) > 1:
            chains = [a + b for a, b in zip(chains[0::2], chains[1::2])]
        return chains[0]

    def count_ge_bf16(cand):
        cand = jnp.broadcast_to(cand.astype(BF16), (16, QBB))

        def body(k0, cnt):
            return cnt + count_pair(sb_scr, k0, cand, 16, jnp.int16(1), jnp.int16(0))
        cnt = pair_loop(body, jnp.zeros((16, QBB), jnp.int16))
        return cnt.astype(jnp.int32).sum(axis=0, keepdims=True)

    def count_ge_f32(cand):
        cand = jnp.broadcast_to(cand, (8, QBB))

        def body(k0, cnt):
            return cnt + count_pair(score_scr, k0, cand, 8, 1.0, 0.0)
        cnt = pair_loop(body, jnp.zeros((8, QBB), F32))
        return cnt.sum(axis=0, keepdims=True).astype(jnp.int32)

    k_int = jnp.int32(int(topk))
    neg_inf_code = jnp.int32(0x007FFFFF)

    def accept(cnt, code):
        return (cnt >= k_int) | ((code >= 0) & (code <= neg_inf_code))

    def coarse_step(i, t_u):
        cand_u = t_u | lax.shift_left(jnp.int32(1), 31 - i)
        cnt = count_ge_bf16(_ordered_code_to_f32(cand_u))
        return jnp.where(accept(cnt, cand_u), cand_u, t_u)

    t_coarse = lax.fori_loop(0, 16, coarse_step, jnp.zeros((1, QBB), jnp.int32))
    base = t_coarse - jnp.int32(1 << 16)

    def fine_step(i, carry):
        off, cnt_at = carry
        cand_off = off | lax.shift_left(jnp.int32(1), 16 - i)
        cand_u = base + cand_off
        cnt = count_ge_f32(_ordered_code_to_f32(cand_u))
        ok = accept(cnt, cand_u)
        return jnp.where(ok, cand_off, off), jnp.where(ok, cnt, cnt_at)

    off, cnt_at = lax.fori_loop(0, FINE_STEPS_FIRST, fine_step,
                                (jnp.zeros((1, QBB), jnp.int32), jnp.full((1, QBB), -1, jnp.int32)))
    thr_first = _ordered_code_to_f32(base + off)
    thr_scr[...] = thr_first
    ties_scr[0] = jnp.int32(0)
    settled = cnt_at == k_int
    unsettled = jnp.max(jnp.where(settled, 0.0, 1.0)) > 0.0

    def tie_analysis(thr_q):
        thr_scr[...] = thr_q

        def count_both(k0, carry):
            ge, gt = carry
            sc = score_scr[pl.ds(k0, KT), :]
            return (ge + col_count(jnp.where(sc >= thr_q, 1.0, 0.0)),
                    gt + col_count(jnp.where(sc > thr_q, 1.0, 0.0)))

        ge8, gt8 = tile_loop(count_both, (jnp.zeros((8, QBB), F32), jnp.zeros((8, QBB), F32)))
        cnt_ge = ge8.sum(axis=0, keepdims=True)
        need_scr[...] = topk - gt8.sum(axis=0, keepdims=True)
        n_inadm = (2 * KT - (q_chunk + 1) * CHUNK).astype(F32)
        n_sel = cnt_ge - jnp.where(thr_q == -jnp.inf, n_inadm, 0.0)
        ties_scr[0] = (jnp.max(jnp.where(n_sel > topk, 1.0, 0.0)) > 0.0).astype(jnp.int32)
        return cnt_ge

    @pl.when(unsettled)
    def _():
        hi_code = base + off + jnp.int32(1 << (FINE_STEPS - FINE_STEPS_FIRST))
        hi = _ordered_code_to_f32(hi_code)
        hi = jnp.where((hi != hi) & (hi_code < 0), jnp.inf, hi)

        def below_max(k0, m):
            sc = score_scr[pl.ds(k0, KT), :]
            return jnp.maximum(m, jnp.where(sc < hi, sc, -jnp.inf).reshape(KT // 8, 8, QBB).max(axis=0))

        m8 = tile_loop(below_max, jnp.full((8, QBB), -jnp.inf, F32))
        thr_try = jnp.where(settled, thr_first, m8.max(axis=0, keepdims=True))
        cnt_try = tie_analysis(thr_try)
        still_short = jnp.max(jnp.where(cnt_try >= topk, 0.0, 1.0)) > 0.0

        @pl.when(still_short)
        def _():
            off_full, _ = lax.fori_loop(FINE_STEPS_FIRST, FINE_STEPS, fine_step, (off, cnt_at))
            tie_analysis(_ordered_code_to_f32(base + off_full))

    thr = thr_scr[...]
    has_ties = ties_scr[0] > 0

    @pl.when(jnp.logical_not(has_ties))
    def _():
        def body(k0, carry):
            score_scr[pl.ds(k0, KT), :] = jnp.where(score_scr[pl.ds(k0, KT), :] >= thr, 0.0, -jnp.inf)
            return carry
        tile_loop(body, 0)

    @pl.when(has_ties)
    def _():
        tri = jnp.where(lax.broadcasted_iota(jnp.int32, (KT, KT), 1)
                        < lax.broadcasted_iota(jnp.int32, (KT, KT), 0), 1.0, 0.0).astype(BF16)

        def body(k0, need):
            sc = score_scr[pl.ds(k0, KT), :]
            eq_f = jnp.where(sc == thr, 1.0, 0.0)
            before = _dot(tri, eq_f.astype(BF16))
            take = jnp.where(sc > thr, 1.0, jnp.where(before < need, eq_f, 0.0))
            score_scr[pl.ds(k0, KT), :] = jnp.where(take > 0.0, 0.0, -jnp.inf)
            return need - eq_f.sum(axis=0, keepdims=True)
        tile_loop(body, need_scr[...])

    score_scr[pl.ds(diag0, 2 * KT), :] = jnp.where(adm_diag, score_scr[pl.ds(diag0, 2 * KT), :], -jnp.inf)

    for h in range(N_HEADS_B):
        g = h // (N_HEADS_B // N_KV_B)
        q_h = qbt_ref[h * HEAD_DIM:(h + 1) * HEAD_DIM, :]
        rhs_scr[h, 0:HEAD_DIM, :] = q_h if g == 0 else zero_rows
        rhs_scr[h, HEAD_DIM:, :] = zero_rows if g == 0 else q_h
    m_scr[...] = jnp.full_like(m_scr, NEG_BIG)
    l_scr[...] = jnp.zeros_like(l_scr)
    acc_scr[...] = jnp.zeros_like(acc_scr)
    ones_rows = jnp.ones((ONES_ROWS, 2 * KT), BF16)

    def logits_head(h, neg, kb_t):
        s = _dot(kb_t, rhs_scr[h]) + neg
        s_scr[h] = s
        tmax_scr[h:h + 1, :] = s.max(axis=0, keepdims=True)

    def softmax_pv_head(h, v_ext):
        g = h // (N_HEADS_B // N_KV_B)
        m_old = m_scr[h:h + 1, :]
        m_new = jnp.maximum(m_old, tmax_scr[h:h + 1, :])
        m_scr[h:h + 1, :] = m_new
        alpha = jnp.exp2(m_old - m_new)
        p = jnp.exp2((s_scr[h] - m_new).astype(BF16))
        pv = _dot(v_ext[g], p)
        r0 = h * HEAD_DIM
        acc_scr[r0:r0 + HEAD_DIM, :] = alpha * acc_scr[r0:r0 + HEAD_DIM, :] + pv[0:HEAD_DIM]
        l_scr[h:h + 1, :] = alpha * l_scr[h:h + 1, :] + pv[HEAD_DIM:HEAD_DIM + 1]

    def stage(p_next, p_cur):
        k_next = pl.multiple_of(jnp.minimum(p_next, n_pairs - 1) * 2 * KT, 2 * KT)
        neg = score_scr[pl.ds(k_next, 2 * KT), :]
        kb_t = kb_ref[pl.ds(k_next, 2 * KT), :]
        if p_cur is not None:
            k_cur = pl.multiple_of(p_cur * 2 * KT, 2 * KT)
            v_ext = [jnp.concatenate([vbt_ref[g * HEAD_DIM:(g + 1) * HEAD_DIM, pl.ds(k_cur, 2 * KT)],
                                      ones_rows], axis=0) for g in range(N_KV_B)]
        for h in range(N_HEADS_B):
            if p_cur is not None:
                softmax_pv_head(h, v_ext)
            logits_head(h, neg, kb_t)

    stage(0, None)

    def attn_pair(i, carry):
        stage(i + 1, i)
        return carry

    lax.fori_loop(0, n_pairs, attn_pair, 0)

    for h in range(N_HEADS_B):
        r0 = h * HEAD_DIM
        o_ref[r0:r0 + HEAD_DIM, :] = (acc_scr[r0:r0 + HEAD_DIM, :] / l_scr[h:h + 1, :]).astype(BF16)


def _mixer_b(qit, wit, qbt, ki, kb, vbt):
    B, _, S = qbt.shape
    topk = min(TOPK_MAX, S // 4)
    assert 2 * KT == QBB and S % QBB == 0 and QBB % CHUNK == 0 and topk <= KT
    qspec = lambda w: pl.BlockSpec((None, w, QBB), lambda b, j: (b, 0, j))
    return pl.pallas_call(
        functools.partial(_mixer_b_kernel, topk=float(topk)),
        grid=(B, S // QBB),
        in_specs=[qspec(WIDTH_IDX), qspec(WI_ROWS), qspec(WIDTH_B),
                  pl.BlockSpec((None, S, LANES), lambda b, j: (b, 0, 0)),
                  pl.BlockSpec((None, S, LANES), lambda b, j: (b, 0, 0)),
                  pl.BlockSpec((None, WIDTH_KV_B, S), lambda b, j: (b, 0, 0))],
        out_specs=qspec(WIDTH_B),
        out_shape=jax.ShapeDtypeStruct((B, WIDTH_B, S), BF16),
        scratch_shapes=[pltpu.VMEM((S, QBB), F32),
                        pltpu.VMEM((S, QBB), BF16),
                        pltpu.VMEM((N_HEADS_B, 2 * KT, QBB), F32),
                        pltpu.VMEM((N_HEADS_B, QBB), F32),
                        pltpu.VMEM((N_HEADS_B, 2 * HEAD_DIM, QBB), BF16),
                        pltpu.VMEM((WIDTH_B, QBB), F32),
                        pltpu.VMEM((N_HEADS_B, QBB), F32),
                        pltpu.VMEM((N_HEADS_B, QBB), F32),
                        pltpu.VMEM((1, QBB), F32),
                        pltpu.VMEM((1, QBB), F32),
                        pltpu.SMEM((1,), jnp.int32)],
        compiler_params=pltpu.CompilerParams(
            dimension_semantics=("parallel", "arbitrary"), vmem_limit_bytes=VMEM_LIMIT),
        name="mixer_b",
    )(qit, wit, qbt, ki, kb, vbt)


def _merge_kernel(x_ref, oat_ref, obt_ref, ga_ref, gb_ref, wa_ref, wb_ref, wo_ref, o_ref):
    ya = _dot_tn(oat_ref[...], wa_ref[...])
    yb = _dot_tn(obt_ref[...], wb_ref[...])
    merged = (jax.nn.sigmoid(ga_ref[...].astype(F32)) * ya
              + jax.nn.sigmoid(gb_ref[...].astype(F32)) * yb)
    o_ref[...] = x_ref[...] + _dot(merged.astype(BF16), wo_ref[...])


def _merge(x3d, oat, obt, ga, gb, wa, wb, wo):
    B, S, _ = x3d.shape
    tm = PROJ_TM
    tmaj = pl.BlockSpec((None, tm, D_MODEL), lambda b, s: (b, s, 0))
    fmaj = lambda w: pl.BlockSpec((None, w, tm), lambda b, s: (b, 0, s))
    full = lambda a: pl.BlockSpec(a.shape, lambda b, s: (0, 0))
    return pl.pallas_call(
        _merge_kernel,
        grid=(B, S // tm),
        in_specs=[tmaj, fmaj(WIDTH_A), fmaj(WIDTH_B), tmaj, tmaj, full(wa), full(wb), full(wo)],
        out_specs=tmaj,
        out_shape=jax.ShapeDtypeStruct((B, S, D_MODEL), F32),
        compiler_params=pltpu.CompilerParams(
            dimension_semantics=("parallel", "parallel"), vmem_limit_bytes=VMEM_LIMIT),
        name="merge",
    )(x3d, oat, obt, ga, gb, wa, wb, wo)


def _rope_tables(seq):
    inv_freq = jnp.power(jnp.float32(ROPE_THETA), -jnp.arange(0, ROT_DIM, 2, dtype=F32) / ROT_DIM)
    ang = jnp.arange(seq, dtype=F32)[:, None] * inv_freq[None, :]
    cos, sin = jnp.cos(ang), jnp.sin(ang)
    ones = jnp.ones((seq, HEAD_DIM - ROT_DIM), F32)
    zeros = jnp.zeros((seq, HEAD_DIM - ROT_DIM), F32)
    zh = jnp.zeros((seq, ROT_HALF), F32)
    c = jnp.concatenate([cos, cos, ones], axis=1)
    sa = jnp.concatenate([-sin, zh, zeros], axis=1)
    sb = jnp.concatenate([zh, sin, zeros], axis=1)
    rep = LANES // HEAD_DIM
    return cos.T, sin.T, jnp.tile(c, (1, rep)), jnp.tile(sa, (1, rep)), jnp.tile(sb, (1, rep))


def _band_bias_ext(rel_bias):
    n_keys = BAND_BLOCKS * QB
    n_h = rel_bias.shape[0]
    lo = jnp.broadcast_to(rel_bias[:, :1], (n_h, QB - 1 - REL_CLIP))
    hi = jnp.broadcast_to(rel_bias[:, -1:], (n_h, BIAS_SPAN - (QB - 1 - REL_CLIP) - (2 * REL_CLIP + 1)))
    ext = jnp.concatenate([lo, rel_bias, hi], axis=1)
    return jnp.roll(ext, -(n_keys - 1), axis=1).astype(F32)


def kernel(x, n1_g, ffn1_w_in, ffn1_w_out, n2_g, w_in, rel_bias, w_branch_a, w_branch_b, w_out,
           n3_g, ffn2_w_in, ffn2_w_out, nf_g):
    B, S, D = x.shape
    depth = n1_g.shape[0]
    cos_t, sin_t, c_tab, sa_tab, sb_tab = _rope_tables(S)
    nf = nf_g.reshape(1, D)
    offs = np.cumsum([0, WIDTH_A, WIDTH_A, WIDTH_A, WIDTH_B, WIDTH_KV_B, WIDTH_KV_B,
                      WIDTH_IDX, IDX_DIM, N_IDX_HEADS, D_MODEL, D_MODEL])
    seg = lambda w, i: w[:, offs[i]:offs[i + 1]]

    for l in range(depth):
        w = w_in[l]
        wt = jnp.concatenate(
            [seg(w, 0), seg(w, 2), seg(w, 3), seg(w, 5), seg(w, 6),
             jnp.pad(seg(w, 8), ((0, 0), (0, LANES - N_IDX_HEADS)))], axis=1).astype(BF16)
        wk = jnp.concatenate(
            [seg(w, 1), seg(w, 4), jnp.pad(seg(w, 7), ((0, 0), (0, LANES - IDX_DIM)))], axis=1).astype(BF16)
        wg = jnp.concatenate([seg(w, 9), seg(w, 10)], axis=1).astype(BF16)

        x2d = _ffn(x.reshape(B * S, D), n1_g[l].reshape(1, D), ffn1_w_in[l].astype(BF16),
                   ffn1_w_out[l].astype(BF16), nf, False)
        x = x2d.reshape(B, S, D)
        (qat, vat, qbt, vbt, qit, wit, ka, kb, ki, ga, gb) = _inproj(
            x, n2_g[l].reshape(1, D), wt, wk, wg, cos_t, sin_t, c_tab, sa_tab, sb_tab)
        oat = _mixer_a(qat, ka, vat, _band_bias_ext(rel_bias[l]))
        obt = _mixer_b(qit, wit, qbt, ki, kb, vbt)
        x = _merge(x, oat, obt, ga, gb, w_branch_a[l].astype(BF16), w_branch_b[l].astype(BF16),
                   w_out[l].astype(BF16))
        last = l == depth - 1
        x2d = _ffn(x.reshape(B * S, D), n3_g[l].reshape(1, D), ffn2_w_in[l].astype(BF16),
                   ffn2_w_out[l].astype(BF16), nf, last)
        x = x2d.reshape(B, S, D)
    return x
```
